```python
import jax, jax.numpy as jnp
from jax import lax
import numpy as np

D_MODEL = 1024
BATCH = 8
SEQ = 2048
DEPTH = 1

CHUNK = 64
N_META = 16
Q_BLOCK = 128
N_HEADS_A = 8
HEAD_DIM_A = 64
ATTN_WIDTH = N_HEADS_A * HEAD_DIM_A
KV_LATENT = 128
IDX_HEADS = 8
IDX_DIM = 32
TOPK_MAX = 256
ATTN_SCALE = HEAD_DIM_A ** -0.5
IDX_SCALE = (IDX_HEADS ** -0.5) * (IDX_DIM ** -0.5)
POOL_WINDOWS = (2, 4, 8, 16)
POOL_WIDTH = 512
POOL_GROUP = POOL_WIDTH // len(POOL_WINDOWS)
IN_SIZES = (ATTN_WIDTH, KV_LATENT, IDX_HEADS * IDX_DIM, IDX_DIM, IDX_HEADS, POOL_WIDTH, 2 * D_MODEL)
IN_WIDTH = sum(IN_SIZES)
N_GROUPS = 4
EXPERTS_PER_GROUP = 8
TOP_E = 2
EXPERT_HIDDEN = 256
EPS = 1e-6

kernel_name = 'hybrid_dsa_pool_hmoe_stream_block'


def _rmsnorm(x, g):
    xf = x.astype(jnp.float32)
    y = xf * lax.rsqrt(jnp.mean(xf * xf, axis=-1, keepdims=True) + EPS)
    return (y * g.astype(jnp.float32)).astype(x.dtype)


def _chunk_ids(n):
    p = jnp.arange(n, dtype=jnp.int32)
    return jnp.where(p < N_META, 0, 1 + (p - N_META) // CHUNK)


def _sparse_attention(q_abs, iq, iw, c, ik, w_uv, k_top):
    B, T, _ = c.shape
    nb = -(-T // Q_BLOCK)
    pad = nb * Q_BLOCK - T
    key_chunk = _chunk_ids(T)
    q_chunk = _chunk_ids(nb * Q_BLOCK).reshape(nb, Q_BLOCK)

    def to_blocks(a):
        a = jnp.pad(a, [(0, 0), (0, pad)] + [(0, 0)] * (a.ndim - 2))
        return jnp.moveaxis(a.reshape((B, nb, Q_BLOCK) + a.shape[2:]), 1, 0)

    def block(args):
        qa, qi, qw, qc = args
        admissible = key_chunk[None, :] <= qc[:, None]
        rel = jax.nn.relu(jnp.einsum('bqhd,bsd->bqhs', qi, ik))
        score = jnp.einsum('bqhs,bqh->bqs', rel, qw)
        score = jnp.where(admissible[None], score, -jnp.inf)
        _, sel = lax.top_k(score, k_top)
        valid = key_chunk[sel] <= qc[None, :, None]
        c_sel = jax.vmap(lambda cb, ib: cb[ib])(c, sel)
        logits = jnp.einsum('bqhc,bqkc->bqhk', qa, c_sel).astype(jnp.float32) * ATTN_SCALE
        logits = jnp.where(valid[:, :, None, :], logits, -jnp.inf)
        p = jax.nn.softmax(logits, axis=-1).astype(c.dtype)
        o_lat = jnp.einsum('bqhk,bqkc->bqhc', p, c_sel)
        o = jnp.einsum('bqhc,chd->bqhd', o_lat, w_uv)
        return o.reshape(B, Q_BLOCK, ATTN_WIDTH)

    out = lax.map(block, (to_blocks(q_abs), to_blocks(iq), to_blocks(iw), q_chunk))
    return jnp.moveaxis(out, 0, 1).reshape(B, nb * Q_BLOCK, ATTN_WIDTH)[:, :T]


def _multiscale_pool(v, w_pool, pool_scale):
    B, T, _ = v.shape
    G = len(POOL_WINDOWS)
    vg = v.reshape(B, T, G, POOL_GROUP).astype(jnp.float32)
    cs = jnp.concatenate([jnp.zeros((B, 1, G, POOL_GROUP), jnp.float32), jnp.cumsum(vg, axis=1)], axis=1)
    pos = jnp.arange(1, T + 1, dtype=jnp.float32)
    outs = []
    for g, w in enumerate(POOL_WINDOWS):
        csg = cs[:, :, g]
        lower = jnp.concatenate([jnp.zeros((B, w - 1, POOL_GROUP), jnp.float32), csg[:, :T - w + 1]], axis=1)
        mean = (csg[:, 1:] - lower) / jnp.minimum(pos, float(w))[None, :, None]
        outs.append(mean - vg[:, :, g])
    d = jnp.stack(outs, axis=2).astype(v.dtype)
    y = jnp.einsum('btgc,gcd->btgd', d, w_pool).reshape(B, T, POOL_WIDTH)
    return y * pool_scale


def _hier_moe(u, w_gr, b_gr, w_er, b_er, w_gate, w_up, w_down):
    B, T, D = u.shape
    uf = u.reshape(B * T, D)
    g_logits = (uf @ w_gr).astype(jnp.float32) + b_gr.astype(jnp.float32)
    g_probs = jax.nn.softmax(g_logits, axis=-1)
    g_sel = jnp.argmax(g_logits, axis=-1)
    p_g = jnp.take_along_axis(g_probs, g_sel[:, None], axis=1)[:, 0]
    e_logits = jnp.einsum('nd,dge->nge', uf, w_er).astype(jnp.float32) + b_er.astype(jnp.float32)
    e_sel = jnp.take_along_axis(e_logits, g_sel[:, None, None], axis=1)[:, 0]
    top_v, top_i = lax.top_k(e_sel, TOP_E)
    p_e = jax.nn.softmax(top_v, axis=-1)
    within = jnp.sum(jax.nn.one_hot(top_i, EXPERTS_PER_GROUP, dtype=jnp.float32) * p_e[..., None], axis=1)
    gate = (jax.nn.one_hot(g_sel, N_GROUPS, dtype=jnp.float32)[:, :, None]
            * within[:, None, :] * p_g[:, None, None]).astype(u.dtype)
    y = jnp.zeros_like(uf)
    for g in range(N_GROUPS):
        hg = jax.nn.silu(jnp.einsum('nd,edf->nef', uf, w_gate[g])) * jnp.einsum('nd,edf->nef', uf, w_up[g])
        y = y + jnp.einsum('nef,efd->nd', hg * gate[:, g, :, None], w_down[g])
    return y.reshape(B, T, D)


def setup_inputs(seed: int = 0) -> dict:
    key = jax.random.key(seed)
    ks = jax.random.split(key, 21)
    f32 = jnp.float32
    D, L = D_MODEL, DEPTH
    nrm = lambda k, shape, fan: jax.random.normal(k, shape, f32) * (fan ** -0.5)
    gain = lambda k, shape: 1.0 + 0.02 * jax.random.normal(k, shape, f32)
    return {
        'x': jax.random.normal(ks[0], (BATCH, SEQ, D), f32),
        'meta_tokens': jax.random.normal(ks[1], (N_META, D), f32),
        'norm1_g': gain(ks[2], (L, D)),
        'w_in': nrm(ks[3], (L, D, IN_WIDTH), D),
        'kv_norm_g': gain(ks[4], (L, KV_LATENT)),
        'w_uk': nrm(ks[5], (L, KV_LATENT, N_HEADS_A, HEAD_DIM_A), KV_LATENT),
        'w_uv': nrm(ks[6], (L, KV_LATENT, N_HEADS_A, HEAD_DIM_A), KV_LATENT),
        'w_pool': nrm(ks[7], (L, len(POOL_WINDOWS), POOL_GROUP, POOL_GROUP), POOL_GROUP),
        'pool_scale': gain(ks[8], (L, POOL_WIDTH)),
        'w_branch_attn': nrm(ks[9], (L, ATTN_WIDTH, D), ATTN_WIDTH),
        'w_branch_pool': nrm(ks[10], (L, POOL_WIDTH, D), POOL_WIDTH),
        'w_out': nrm(ks[11], (L, D, D), D),
        'norm2_g': gain(ks[12], (L, D)),
        'w_group_router': nrm(ks[13], (L, D, N_GROUPS), D),
        'b_group_router': 0.01 * jax.random.normal(ks[14], (L, N_GROUPS), f32),
        'w_expert_router': nrm(ks[15], (L, D, N_GROUPS, EXPERTS_PER_GROUP), D),
        'b_expert_router': 0.01 * jax.random.normal(ks[16], (L, N_GROUPS, EXPERTS_PER_GROUP), f32),
        'w_expert_gate': nrm(ks[17], (L, N_GROUPS, EXPERTS_PER_GROUP, D, EXPERT_HIDDEN), D),
        'w_expert_up': nrm(ks[18], (L, N_GROUPS, EXPERTS_PER_GROUP, D, EXPERT_HIDDEN), D),
        'w_expert_down': nrm(ks[19], (L, N_GROUPS, EXPERTS_PER_GROUP, EXPERT_HIDDEN, D), EXPERT_HIDDEN),
        'final_norm_g': gain(ks[20], (D,)),
    }


def reference(x, meta_tokens, norm1_g, w_in, kv_norm_g, w_uk, w_uv, w_pool, pool_scale,
              w_branch_attn, w_branch_pool, w_out, norm2_g, w_group_router, b_group_router,
              w_expert_router, b_expert_router, w_expert_gate, w_expert_up, w_expert_down,
              final_norm_g):
    B, S, D = x.shape
    k_top = min(TOPK_MAX, S // 4)
    meta = jnp.broadcast_to(meta_tokens.astype(x.dtype)[None], (B, N_META, D))
    h = jnp.concatenate([meta, x], axis=1)
    T = S + N_META
    split_at = [int(v) for v in np.cumsum(IN_SIZES)[:-1]]
    for l in range(DEPTH):
        u = _rmsnorm(h, norm1_g[l])
        q, c, iq, ik, iw, pv, gates = jnp.split(u @ w_in[l], split_at, axis=-1)
        q = q.reshape(B, T, N_HEADS_A, HEAD_DIM_A)
        c = _rmsnorm(c, kv_norm_g[l])
        q_abs = jnp.einsum('bthd,chd->bthc', q, w_uk[l])
        iq = iq.reshape(B, T, IDX_HEADS, IDX_DIM)
        iw = iw * IDX_SCALE
        attn = _sparse_attention(q_abs, iq, iw, c, ik, w_uv[l], k_top)
        pool = _multiscale_pool(pv, w_pool[l], pool_scale[l])
        g_attn, g_pool = jnp.split(jax.nn.sigmoid(gates), 2, axis=-1)
        merged = g_attn * (attn @ w_branch_attn[l]) + g_pool * (pool @ w_branch_pool[l])
        h = h + merged @ w_out[l]
        h = h + _hier_moe(_rmsnorm(h, norm2_g[l]), w_group_router[l], b_group_router[l],
                          w_expert_router[l], b_expert_router[l], w_expert_gate[l],
                          w_expert_up[l], w_expert_down[l])
    return _rmsnorm(h, final_norm_g)[:, N_META:]
```

```python
import functools

import jax
import jax.numpy as jnp
from jax import lax
from jax.experimental import pallas as pl
from jax.experimental.pallas import tpu as pltpu

F32 = jnp.float32
BF16 = jnp.bfloat16

D_MODEL = 1024
CHUNK = 64
N_META = 16
N_HEADS = 8
HEAD_DIM = 64
ATTN_WIDTH = N_HEADS * HEAD_DIM
KV_LATENT = 128
IDX_HEADS = 8
IDX_DIM = 32
TOPK_MAX = 256
ATTN_SCALE = HEAD_DIM ** -0.5
IDX_SCALE = (IDX_HEADS ** -0.5) * (IDX_DIM ** -0.5)
POOL_WINDOWS = (2, 4, 8, 16)
POOL_WIDTH = 512
POOL_GROUP = 128
N_GROUPS = 4
EXPERTS_PER_GROUP = 8
N_EXPERTS = N_GROUPS * EXPERTS_PER_GROUP
EXPERT_HIDDEN = 256
EPS = 1e-6

LANES = 128
W1_WIDTH = 1536
NEG = -1e30
POS = 1e30
VMEM_LIMIT = 56 * 1024 * 1024

TM_PROJ = 512
QB = 256
KC = 128
N_BISECT = 32
TM_MOE = 1024


def _rms(x, g):
    return x * lax.rsqrt(jnp.mean(x * x, axis=-1, keepdims=True) + EPS) * g


def _dot(a, b):
    return jnp.dot(a, b, preferred_element_type=F32)


def _dot_nt(a, b):
    return lax.dot_general(a, b, (((1,), (1,)), ((), ())), preferred_element_type=F32)


def _meta_kernel(m_ref, g1_ref, w1_ref, kvg_ref, c_ref, tail_ref, pv_ref):
    u = _rms(m_ref[...], g1_ref[...]).astype(BF16)
    p = _dot(u, w1_ref[...])
    c_ref[...] = _rms(p[:, 512:640], kvg_ref[...])
    pv_ref[...] = p[:, 896:1408]
    tail_ref[...] = p[:, 1408:1536]


def _proj_kernel(x_ref, g1_ref, w1_ref, kvg_ref, pvmeta_ref, wpool_ref, pscale_ref,
                 q_ref, c_ref, ct_ref, iq_ref, ik_ref, iwt_ref, yp_ref, ext_ref):
    tm = x_ref.shape[0]
    u = _rms(x_ref[...], g1_ref[...]).astype(BF16)
    p = _dot(u, w1_ref[...])
    q_ref[...] = p[:, 0:512].astype(BF16)
    c = _rms(p[:, 512:640], kvg_ref[...])
    c_ref[...] = c.astype(BF16)
    ct_ref[...] = c.T.astype(BF16)
    iq_ref[...] = p[:, 640:896].astype(BF16)
    tail = p[:, 1408:1536]
    ik_ref[...] = tail[:, 0:IDX_DIM].astype(BF16)
    iwt_ref[...] = tail.T[IDX_DIM:IDX_DIM + IDX_HEADS, :] * IDX_SCALE
    pv = p[:, 896:1408]

    @pl.when(pl.program_id(1) == 0)
    def _():
        ext_ref[0:N_META, :] = pvmeta_ref[...]

    ext_ref[N_META:N_META + tm, :] = pv
    for g, w in enumerate(POOL_WINDOWS):
        cols = slice(g * POOL_GROUP, (g + 1) * POOL_GROUP)
        acc = pv[:, cols]
        for k in range(1, w):
            acc = acc + ext_ref[N_META - k:N_META - k + tm, cols]
        d = acc * (1.0 / w) - pv[:, cols]
        y = _dot(d.astype(BF16), wpool_ref[g]) * pscale_ref[:, cols]
        yp_ref[:, cols] = y.astype(BF16)
    ext_ref[0:N_META, :] = ext_ref[tm:tm + N_META, :]


def _attn_kernel(ik_ref, c_ref, ct_ref, iq_ref, iwt_ref, q_ref, wukt_ref, wuvt_ref, o_ref,
                 s_scr, lg_scr, p_scr, ot_scr, *, n_real, k_top):
    kp = ik_ref.shape[0]
    qb = iq_ref.shape[0]
    nch = kp // KC
    j = pl.program_id(1)
    qchunk = (j * qb + lax.broadcasted_iota(jnp.int32, (1, qb), 1)) // CHUNK

    def rows(kc):
        return pl.ds(pl.multiple_of(kc * KC, KC), KC)

    def key_chunk_id(kc):
        row = kc * KC + lax.broadcasted_iota(jnp.int32, (KC, 1), 0)
        return jnp.where(row < n_real, row // CHUNK,
                         jnp.where(row < n_real + N_META, -1, 1 << 20))

    iq_heads = [iq_ref[:, h * IDX_DIM:(h + 1) * IDX_DIM] for h in range(IDX_HEADS)]

    def score_body(kc, carry):
        mn, mx = carry
        ikc = ik_ref[rows(kc), :]
        acc = jnp.zeros((KC, qb), F32)
        for h in range(IDX_HEADS):
            rel = _dot_nt(ikc, iq_heads[h])
            acc = acc + jnp.maximum(rel, 0.0) * iwt_ref[h:h + 1, :]
        adm = key_chunk_id(kc) <= qchunk
        s_scr[rows(kc), :] = jnp.where(adm, acc, NEG)
        mn = jnp.minimum(mn, jnp.min(jnp.where(adm, acc, POS), axis=0, keepdims=True))
        mx = jnp.maximum(mx, jnp.max(jnp.where(adm, acc, NEG), axis=0, keepdims=True))
        return mn, mx

    mn, mx = lax.fori_loop(0, nch, score_body,
                           (jnp.full((1, qb), POS, F32), jnp.full((1, qb), NEG, F32)))

    def count_ge(th):
        def body(kc, cnt):
            return cnt + jnp.sum(jnp.where(s_scr[rows(kc), :] >= th, 1.0, 0.0),
                                 axis=0, keepdims=True)
        return lax.fori_loop(0, nch, body, jnp.zeros((1, qb), F32))

    def bisect_body(_, carry):
        lo, hi = carry
        mid = lo + (hi - lo) * 0.5
        ge = count_ge(mid) >= k_top
        return jnp.where(ge, mid, lo), jnp.where(ge, hi, mid)

    hi0 = mx + (mx - mn) + (jnp.abs(mx) * (2.0 ** -10) + 1e-30)
    lo, hi = lax.fori_loop(0, N_BISECT, bisect_body, (mn, hi0))

    need = k_top - count_ge(hi)
    tri = jnp.where(lax.broadcasted_iota(jnp.int32, (KC, KC), 0)
                    >= lax.broadcasted_iota(jnp.int32, (KC, KC), 1), 1.0, 0.0).astype(BF16)

    def select_body(t, before):
        kc = jnp.where(t == 0, nch - 1, t - 1)
        sm = s_scr[rows(kc), :]
        band = jnp.where(sm >= lo, jnp.where(sm < hi, 1.0, 0.0), 0.0)
        rank = _dot(tri, band.astype(BF16)) + before
        take = jnp.where(rank <= need, band, 0.0)
        s_scr[rows(kc), :] = jnp.where(sm >= hi, 0.0, jnp.where(take > 0.5, 0.0, NEG))
        return before + jnp.sum(band, axis=0, keepdims=True)

    lax.fori_loop(0, nch, select_body, jnp.zeros((1, qb), F32))

    for h in range(N_HEADS):
        qa = _dot(q_ref[:, h * HEAD_DIM:(h + 1) * HEAD_DIM], wukt_ref[h]).astype(BF16)

        def logit_body(kc, m):
            lg = _dot_nt(c_ref[rows(kc), :], qa) * ATTN_SCALE + s_scr[rows(kc), :]
            lg_scr[rows(kc), :] = lg
            return jnp.maximum(m, jnp.max(lg, axis=0, keepdims=True))

        m = lax.fori_loop(0, nch, logit_body, jnp.full((1, qb), NEG, F32))

        def prob_body(kc, l):
            p = jnp.exp(lg_scr[rows(kc), :] - m)
            p_scr[rows(kc), :] = p.astype(BF16)
            return l + jnp.sum(p, axis=0, keepdims=True)

        l = lax.fori_loop(0, nch, prob_body, jnp.zeros((1, qb), F32))
        olat = _dot(ct_ref[...], p_scr[...]) / l
        ot_scr[h * HEAD_DIM:(h + 1) * HEAD_DIM, :] = _dot(wuvt_ref[h], olat.astype(BF16))
    o_ref[...] = ot_scr[...].T.astype(BF16)


def _mix_kernel(x_ref, g1_ref, wgate_ref, attn_ref, yp_ref, wba_ref, wbp_ref, wout_ref,
                g2_ref, wr_ref, br_ref, h1_ref, u2_ref, gate_ref):
    x = x_ref[...]
    u = _rms(x, g1_ref[...]).astype(BF16)
    gates = 1.0 / (1.0 + jnp.exp(-_dot(u, wgate_ref[...])))
    a = _dot(attn_ref[...], wba_ref[...])
    bp = _dot(yp_ref[...], wbp_ref[...])
    merged = gates[:, :D_MODEL] * a + gates[:, D_MODEL:] * bp
    h1 = x + _dot(merged.astype(BF16), wout_ref[...])
    h1_ref[...] = h1
    u2 = _rms(h1, g2_ref[...])
    u2_ref[...] = u2.astype(BF16)

    lg = jnp.dot(u2, wr_ref[...], preferred_element_type=F32,
                 precision=lax.Precision.HIGHEST) + br_ref[...]
    lane = lax.broadcasted_iota(jnp.int32, lg.shape, 1)
    is_g = lane < N_GROUPS
    gl = jnp.where(is_g, lg, NEG)
    gmax = jnp.max(gl, axis=1, keepdims=True)
    gidx = jnp.min(jnp.where(gl == gmax, lane, LANES), axis=1, keepdims=True)
    p_g = 1.0 / jnp.sum(jnp.where(is_g, jnp.exp(gl - gmax), 0.0), axis=1, keepdims=True)
    e_lane = lane - N_GROUPS
    lane_grp = jnp.where(e_lane >= 0,
                         jnp.where(e_lane < N_EXPERTS, e_lane // EXPERTS_PER_GROUP, -1), -1)
    in_grp = lane_grp == gidx
    el = jnp.where(in_grp, lg, NEG)
    t1 = jnp.max(el, axis=1, keepdims=True)
    i1 = jnp.min(jnp.where(el == t1, lane, LANES), axis=1, keepdims=True)
    el2 = jnp.where(lane == i1, NEG, el)
    t2 = jnp.max(el2, axis=1, keepdims=True)
    i2 = jnp.min(jnp.where(el2 == t2, lane, LANES), axis=1, keepdims=True)
    r = jnp.exp(t2 - t1)
    p1 = 1.0 / (1.0 + r)
    p2 = r * p1
    gate_ref[...] = jnp.where(lane == i1, p1 * p_g, jnp.where(lane == i2, p2 * p_g, 0.0))


def _moe_kernel(u2_ref, gate_ref, h1_ref, wg_ref, wu_ref, wd_ref, gf_ref, o_ref, acc_ref):
    e = pl.program_id(1)

    @pl.when(e == 0)
    def _():
        acc_ref[...] = h1_ref[...]

    x = u2_ref[...]
    a = _dot(x, wg_ref[...])
    b = _dot(x, wu_ref[...])
    gate = gate_ref[...]
    lane = lax.broadcasted_iota(jnp.int32, gate.shape, 1)
    gcol = jnp.sum(jnp.where(lane == e + N_GROUPS, gate, 0.0), axis=1, keepdims=True)
    hg = a * (1.0 / (1.0 + jnp.exp(-a))) * b * gcol
    acc_ref[...] += _dot(hg.astype(BF16), wd_ref[...])

    @pl.when(e == pl.num_programs(1) - 1)
    def _():
        o_ref[...] = _rms(acc_ref[...], gf_ref[...])


def _const_spec(shape):
    nd = len(shape)
    return pl.BlockSpec(shape, lambda *_: (0,) * nd)


def _params(n_axes):
    return pltpu.CompilerParams(dimension_semantics=("arbitrary",) * n_axes,
                                vmem_limit_bytes=VMEM_LIMIT)


def kernel(x, meta_tokens, norm1_g, w_in, kv_norm_g, w_uk, w_uv, w_pool, pool_scale,
           w_branch_attn, w_branch_pool, w_out, norm2_g, w_group_router, b_group_router,
           w_expert_router, b_expert_router, w_expert_gate, w_expert_up, w_expert_down,
           final_norm_g):
    B, S, D = x.shape
    assert D == D_MODEL and S % QB == 0 and S % TM_PROJ == 0 and w_in.shape[0] == 1
    N = B * S
    k_top = min(TOPK_MAX, S // 4)
    kp = S + LANES
    xr = x.reshape(N, D)

    wi = w_in[0]
    w1 = jnp.concatenate(
        [wi[:, 0:640], wi[:, 640:896], wi[:, 936:1448], wi[:, 896:936],
         jnp.zeros((D, W1_WIDTH - 1448), F32)], axis=1).astype(BF16)
    wgate = wi[:, 1448:].astype(BF16)
    g1 = norm1_g[0].reshape(1, D)
    kvg = kv_norm_g[0].reshape(1, KV_LATENT)
    wpool = w_pool[0].astype(BF16)
    pscale = pool_scale[0].reshape(1, POOL_WIDTH)
    wukt = jnp.transpose(w_uk[0], (1, 2, 0)).astype(BF16)
    wuvt = jnp.transpose(w_uv[0], (1, 2, 0)).astype(BF16)
    wr = jnp.concatenate(
        [w_group_router[0], w_expert_router[0].reshape(D, N_EXPERTS),
         jnp.zeros((D, LANES - N_GROUPS - N_EXPERTS), F32)], axis=1)
    br = jnp.concatenate(
        [b_group_router[0], b_expert_router[0].reshape(N_EXPERTS),
         jnp.zeros((LANES - N_GROUPS - N_EXPERTS,), F32)]).reshape(1, LANES)

    c_m, tail_m, pv_m = pl.pallas_call(
        _meta_kernel,
        out_shape=(jax.ShapeDtypeStruct((N_META, KV_LATENT), F32),
                   jax.ShapeDtypeStruct((N_META, LANES), F32),
                   jax.ShapeDtypeStruct((N_META, POOL_WIDTH), F32)),
        name="meta",
    )(meta_tokens, g1, w1, kvg)

    tpb = S // TM_PROJ
    tok = lambda w: pl.BlockSpec((TM_PROJ, w), lambda b, i: (b * tpb + i, 0))
    q, c, ct, iq, ik, iwt, yp = pl.pallas_call(
        _proj_kernel,
        grid=(B, tpb),
        in_specs=[tok(D), _const_spec((1, D)), _const_spec((D, W1_WIDTH)),
                  _const_spec((1, KV_LATENT)), _const_spec((N_META, POOL_WIDTH)),
                  _const_spec((len(POOL_WINDOWS), POOL_GROUP, POOL_GROUP)),
                  _const_spec((1, POOL_WIDTH))],
        out_specs=[tok(ATTN_WIDTH), tok(KV_LATENT),
                   pl.BlockSpec((None, KV_LATENT, TM_PROJ), lambda b, i: (b, 0, i)),
                   tok(IDX_HEADS * IDX_DIM), tok(IDX_DIM),
                   pl.BlockSpec((IDX_HEADS, TM_PROJ), lambda b, i: (0, b * tpb + i)),
                   tok(POOL_WIDTH)],
        out_shape=(jax.ShapeDtypeStruct((N, ATTN_WIDTH), BF16),
                   jax.ShapeDtypeStruct((N, KV_LATENT), BF16),
                   jax.ShapeDtypeStruct((B, KV_LATENT, S), BF16),
                   jax.ShapeDtypeStruct((N, IDX_HEADS * IDX_DIM), BF16),
                   jax.ShapeDtypeStruct((N, IDX_DIM), BF16),
                   jax.ShapeDtypeStruct((IDX_HEADS, N), F32),
                   jax.ShapeDtypeStruct((N, POOL_WIDTH), BF16)),
        scratch_shapes=[pltpu.VMEM((TM_PROJ + N_META, POOL_WIDTH), F32)],
        compiler_params=_params(2),
        name="proj",
    )(xr, g1, w1, kvg, pv_m, wpool, pscale)

    pad = kp - S - N_META
    c_mb = c_m.astype(BF16)
    c_all = jnp.concatenate(
        [c.reshape(B, S, KV_LATENT), jnp.broadcast_to(c_mb[None], (B, N_META, KV_LATENT)),
         jnp.zeros((B, pad, KV_LATENT), BF16)], axis=1)
    ct_all = jnp.concatenate(
        [ct, jnp.broadcast_to(c_mb.T[None], (B, KV_LATENT, N_META)),
         jnp.zeros((B, KV_LATENT, pad), BF16)], axis=2)
    ik_all = jnp.concatenate(
        [ik.reshape(B, S, IDX_DIM),
         jnp.broadcast_to(tail_m[:, :IDX_DIM].astype(BF16)[None], (B, N_META, IDX_DIM)),
         jnp.zeros((B, pad, IDX_DIM), BF16)], axis=1)

    nqb = S // QB
    qtok = lambda w: pl.BlockSpec((QB, w), lambda b, j: (b * nqb + j, 0))
    attn = pl.pallas_call(
        functools.partial(_attn_kernel, n_real=S, k_top=float(k_top)),
        grid=(B, nqb),
        in_specs=[pl.BlockSpec((None, kp, IDX_DIM), lambda b, j: (b, 0, 0)),
                  pl.BlockSpec((None, kp, KV_LATENT), lambda b, j: (b, 0, 0)),
                  pl.BlockSpec((None, KV_LATENT, kp), lambda b, j: (b, 0, 0)),
                  qtok(IDX_HEADS * IDX_DIM),
                  pl.BlockSpec((IDX_HEADS, QB), lambda b, j: (0, b * nqb + j)),
                  qtok(ATTN_WIDTH),
                  _const_spec((N_HEADS, HEAD_DIM, KV_LATENT)),
                  _const_spec((N_HEADS, HEAD_DIM, KV_LATENT))],
        out_specs=qtok(ATTN_WIDTH),
        out_shape=jax.ShapeDtypeStruct((N, ATTN_WIDTH), BF16),
        scratch_shapes=[pltpu.VMEM((kp, QB), F32), pltpu.VMEM((kp, QB), F32),
                        pltpu.VMEM((kp, QB), BF16), pltpu.VMEM((ATTN_WIDTH, QB), F32)],
        compiler_params=_params(2),
        name="attn",
    )(ik_all, c_all, ct_all, iq, iwt, q, wukt, wuvt)

    nt = N // TM_PROJ
    tok1 = lambda w: pl.BlockSpec((TM_PROJ, w), lambda i: (i, 0))
    h1, u2, gate = pl.pallas_call(
        _mix_kernel,
        grid=(nt,),
        in_specs=[tok1(D), _const_spec((1, D)), _const_spec((D, 2 * D)), tok1(ATTN_WIDTH),
                  tok1(POOL_WIDTH), _const_spec((ATTN_WIDTH, D)), _const_spec((POOL_WIDTH, D)),
                  _const_spec((D, D)), _const_spec((1, D)), _const_spec((D, LANES)),
                  _const_spec((1, LANES))],
        out_specs=[tok1(D), tok1(D), tok1(LANES)],
        out_shape=(jax.ShapeDtypeStruct((N, D), F32), jax.ShapeDtypeStruct((N, D), BF16),
                   jax.ShapeDtypeStruct((N, LANES), F32)),
        compiler_params=_params(1),
        name="mix",
    )(xr, g1, wgate, attn, yp, w_branch_attn[0].astype(BF16), w_branch_pool[0].astype(BF16),
      w_out[0].astype(BF16), norm2_g[0].reshape(1, D), wr, br)

    weg = w_expert_gate[0].reshape(N_EXPERTS, D, EXPERT_HIDDEN).astype(BF16)
    weu = w_expert_up[0].reshape(N_EXPERTS, D, EXPERT_HIDDEN).astype(BF16)
    wed = w_expert_down[0].reshape(N_EXPERTS, EXPERT_HIDDEN, D).astype(BF16)
    mtok = lambda w: pl.BlockSpec((TM_MOE, w), lambda i, e: (i, 0))
    out = pl.pallas_call(
        _moe_kernel,
        grid=(N // TM_MOE, N_EXPERTS),
        in_specs=[mtok(D), mtok(LANES), mtok(D),
                  pl.BlockSpec((None, D, EXPERT_HIDDEN), lambda i, e: (e, 0, 0)),
                  pl.BlockSpec((None, D, EXPERT_HIDDEN), lambda i, e: (e, 0, 0)),
                  pl.BlockSpec((None, EXPERT_HIDDEN, D), lambda i, e: (e, 0, 0)),
                  pl.BlockSpec((1, D), lambda i, e: (0, 0))],
        out_specs=mtok(D),
        out_shape=jax.ShapeDtypeStruct((N, D), F32),
        scratch_shapes=[pltpu.VMEM((TM_MOE, D), F32)],
        compiler_params=_params(2),
        name="moe",
    )(u2, gate, h1, weg, weu, wed, final_norm_g.reshape(1, D))
    return out.reshape(B, S, D)
```

```python
import functools

import jax
import jax.numpy as jnp
from jax import lax
from jax.experimental import pallas as pl
from jax.experimental.pallas import tpu as pltpu

F32 = jnp.float32
BF16 = jnp.bfloat16

D_MODEL = 1024
CHUNK = 64
N_META = 16
N_HEADS = 8
HEAD_DIM = 64
ATTN_WIDTH = N_HEADS * HEAD_DIM
KV_LATENT = 128
IDX_HEADS = 8
IDX_DIM = 32
TOPK_MAX = 256
ATTN_SCALE = HEAD_DIM ** -0.5
IDX_SCALE = (IDX_HEADS ** -0.5) * (IDX_DIM ** -0.5)
POOL_WINDOWS = (2, 4, 8, 16)
POOL_WIDTH = 512
POOL_GROUP = 128
N_GROUPS = 4
EXPERTS_PER_GROUP = 8
N_EXPERTS = N_GROUPS * EXPERTS_PER_GROUP
EXPERT_HIDDEN = 256
EPS = 1e-6

LANES = 128
SUBLANES = 8
W1_WIDTH = 1536
NEG = -1e30
POS = 1e30
VMEM_LIMIT = 56 * 1024 * 1024

TM_PROJ = 512
QB = 256
KC = 256
MAX_BISECT = 40
BISECT_PER_CHECK = 4
TM_MOE = 1024


def _rms(x, g):
    return x * lax.rsqrt(jnp.mean(x * x, axis=-1, keepdims=True) + EPS) * g


def _dot(a, b):
    return jnp.dot(a, b, preferred_element_type=F32)


def _meta_kernel(m_ref, g1_ref, w1_ref, kvg_ref, c_ref, tail_ref, pv_ref):
    u = _rms(m_ref[...], g1_ref[...]).astype(BF16)
    p = _dot(u, w1_ref[...])
    c_ref[...] = _rms(p[:, 512:640], kvg_ref[...])
    pv_ref[...] = p[:, 896:1408]
    tail_ref[...] = p[:, 1408:1536]


def _proj_kernel(x_ref, g1_ref, w1_ref, kvg_ref, pvmeta_ref, wpool_ref, pscale_ref,
                 qt_ref, c_ref, ct_ref, iqt_ref, ik_ref, iwt_ref, yp_ref, ext_ref):
    tm = x_ref.shape[0]
    u = _rms(x_ref[...], g1_ref[...]).astype(BF16)
    p = _dot(u, w1_ref[...])
    qt_ref[...] = p[:, 0:512].T.astype(BF16)
    c = _rms(p[:, 512:640], kvg_ref[...])
    c_ref[...] = c.astype(BF16)
    ct = c.T.astype(BF16)
    for k in range(tm // KC):
        ct_ref[k] = ct[:, k * KC:(k + 1) * KC]
    iqt_ref[...] = p[:, 640:896].T.astype(BF16)
    tail = p[:, 1408:1536]
    ik_ref[...] = tail[:, 0:IDX_DIM].astype(BF16)
    iwt_ref[...] = tail.T[IDX_DIM:IDX_DIM + IDX_HEADS, :] * IDX_SCALE
    pv = p[:, 896:1408]

    @pl.when(pl.program_id(1) == 0)
    def _():
        ext_ref[0:N_META, :] = pvmeta_ref[...]

    ext_ref[N_META:N_META + tm, :] = pv
    for g, w in enumerate(POOL_WINDOWS):
        cols = slice(g * POOL_GROUP, (g + 1) * POOL_GROUP)
        acc = pv[:, cols]
        for k in range(1, w):
            acc = acc + ext_ref[N_META - k:N_META - k + tm, cols]
        d = acc * (1.0 / w) - pv[:, cols]
        y = _dot(d.astype(BF16), wpool_ref[g]) * pscale_ref[:, cols]
        yp_ref[:, cols] = y.astype(BF16)
    ext_ref[0:N_META, :] = ext_ref[tm:tm + N_META, :]


def _attn_kernel(ik_ref, ikm_ref, c_ref, cm_ref, ct_ref, cmt_ref, iqt_ref, iwt_ref, qt_ref,
                 wuk_ref, wuvt_ref, o_ref,
                 s_scr, qat_scr, lg_scr, p_scr, m_scr, l_scr, a_scr, acc_scr, ot_scr, *, k_top):
    n_real = c_ref.shape[0]
    qb = iqt_ref.shape[1]
    j = pl.program_id(1)
    nkc = j + 1
    qchunk = (j * qb + lax.broadcasted_iota(jnp.int32, (1, qb), 1)) // CHUNK
    meta_rows = pl.ds(n_real, N_META)

    def rows(kc):
        return pl.ds(pl.multiple_of(kc * KC, KC), KC)

    def col_sum(x):
        r = x.shape[0]
        if r > SUBLANES:
            x = jnp.sum(x.reshape(r // SUBLANES, SUBLANES, qb), axis=0)
        return jnp.sum(x, axis=0, keepdims=True)

    iqt_heads = [iqt_ref[h * IDX_DIM:(h + 1) * IDX_DIM, :] for h in range(IDX_HEADS)]

    def scores(ik_rows):
        acc = None
        for h in range(IDX_HEADS):
            t = jnp.maximum(_dot(ik_rows, iqt_heads[h]), 0.0) * iwt_ref[h:h + 1, :]
            acc = t if acc is None else acc + t
        return acc

    sm = scores(ikm_ref[...])
    s_scr[meta_rows, :] = sm
    mn0 = jnp.min(sm, axis=0, keepdims=True)
    mx0 = jnp.max(sm, axis=0, keepdims=True)

    def score_body(kc, carry):
        mn, mx = carry
        sc = scores(ik_ref[rows(kc), :])
        kchunk = (kc * KC + lax.broadcasted_iota(jnp.int32, (KC, 1), 0)) // CHUNK
        adm = kchunk <= qchunk
        s_scr[rows(kc), :] = jnp.where(adm, sc, NEG)
        mn = jnp.minimum(mn, jnp.min(jnp.where(adm, sc, POS), axis=0, keepdims=True))
        mx = jnp.maximum(mx, jnp.max(jnp.where(adm, sc, NEG), axis=0, keepdims=True))
        return mn, mx

    mn, mx = lax.fori_loop(0, nkc, score_body, (mn0, mx0))

    def count_ge(th):
        def body(kc, cnt):
            return cnt + jnp.sum(
                jnp.where(s_scr[rows(kc), :] >= th, 1.0, 0.0).reshape(KC // SUBLANES, SUBLANES, qb),
                axis=0)
        cnt = lax.fori_loop(0, nkc, body, jnp.zeros((SUBLANES, qb), F32))
        return (jnp.sum(cnt, axis=0, keepdims=True)
                + jnp.sum(jnp.where(s_scr[meta_rows, :] >= th, 1.0, 0.0), axis=0, keepdims=True))

    def pending_of(cnt_lo):
        return jnp.max(jnp.where(cnt_lo > k_top, 1, 0))

    def bisect_cond(carry):
        it, _, _, _, pending = carry
        return jnp.logical_and(it < MAX_BISECT, pending > 0)

    def bisect_body(carry):
        it, lo, hi, cnt_lo, _ = carry
        for _ in range(BISECT_PER_CHECK):
            mid = lo + (hi - lo) * 0.5
            cnt = count_ge(mid)
            ge = cnt >= k_top
            lo = jnp.where(ge, mid, lo)
            hi = jnp.where(ge, hi, mid)
            cnt_lo = jnp.where(ge, cnt, cnt_lo)
        return it + BISECT_PER_CHECK, lo, hi, cnt_lo, pending_of(cnt_lo)

    n_adm = (N_META + CHUNK * (qchunk + 1)).astype(F32)
    hi0 = mx + (mx - mn) + (jnp.abs(mx) * (2.0 ** -10) + 1e-30)
    _, lo, hi, _, pending = lax.while_loop(
        bisect_cond, bisect_body, (jnp.int32(0), mn, hi0, n_adm, pending_of(n_adm)))

    @pl.when(pending == 0)
    def _():
        def body(kc, _):
            s_scr[rows(kc), :] = jnp.where(s_scr[rows(kc), :] >= lo, 0.0, NEG)
            return 0
        lax.fori_loop(0, nkc, body, 0)
        s_scr[meta_rows, :] = jnp.where(s_scr[meta_rows, :] >= lo, 0.0, NEG)

    @pl.when(pending > 0)
    def _():
        need = k_top - count_ge(hi)

        def pick(sv, tri, before):
            band = jnp.where(sv >= lo, jnp.where(sv < hi, 1.0, 0.0), 0.0)
            rank = _dot(tri, band.astype(BF16)) + before
            take = jnp.where(rank <= need, band, 0.0)
            bias = jnp.where(sv >= hi, 0.0, jnp.where(take > 0.5, 0.0, NEG))
            return bias, before + jnp.sum(band, axis=0, keepdims=True)

        def tri(n):
            return jnp.where(lax.broadcasted_iota(jnp.int32, (n, n), 0)
                             >= lax.broadcasted_iota(jnp.int32, (n, n), 1), 1.0, 0.0).astype(BF16)

        bias_m, before = pick(s_scr[meta_rows, :], tri(N_META), jnp.zeros((1, qb), F32))
        s_scr[meta_rows, :] = bias_m
        tri_kc = tri(KC)

        def body(kc, before):
            bias, before = pick(s_scr[rows(kc), :], tri_kc, before)
            s_scr[rows(kc), :] = bias
            return before
        lax.fori_loop(0, nkc, body, before)

    def head(h):
        return slice(h * qb, (h + 1) * qb)

    for h in range(N_HEADS):
        qat_scr[:, head(h)] = (_dot(wuk_ref[h], qt_ref[h * HEAD_DIM:(h + 1) * HEAD_DIM, :])
                               * ATTN_SCALE).astype(BF16)
    m_scr[...] = jnp.full(m_scr.shape, 0.5 * NEG, F32)
    l_scr[...] = jnp.zeros(l_scr.shape, F32)
    acc_scr[...] = jnp.zeros(acc_scr.shape, F32)

    def attend(c_rows, ct_cols, bias):
        r = c_rows.shape[0]
        lg_scr[0:r, :] = _dot(c_rows, qat_scr[...])
        for h in range(N_HEADS):
            lg = lg_scr[0:r, head(h)] + bias
            m_old = m_scr[:, head(h)]
            m_new = jnp.maximum(m_old, jnp.max(lg, axis=0, keepdims=True))
            p = jnp.exp(lg - m_new)
            alpha = jnp.exp(m_old - m_new)
            l_scr[:, head(h)] = l_scr[:, head(h)] * alpha + col_sum(p)
            p_scr[0:r, head(h)] = p.astype(BF16)
            a_scr[:, head(h)] = alpha
            m_scr[:, head(h)] = m_new
        acc_scr[...] = acc_scr[...] * a_scr[...] + _dot(ct_cols, p_scr[0:r, :])

    attend(cm_ref[...], cmt_ref[...], s_scr[meta_rows, :])

    def attend_body(kc, _):
        attend(c_ref[rows(kc), :], ct_ref[kc], s_scr[rows(kc), :])
        return 0
    lax.fori_loop(0, nkc, attend_body, 0)

    for h in range(N_HEADS):
        olat = acc_scr[:, head(h)] / l_scr[:, head(h)]
        ot_scr[h * HEAD_DIM:(h + 1) * HEAD_DIM, :] = _dot(wuvt_ref[h], olat.astype(BF16))
    o_ref[...] = ot_scr[...].T.astype(BF16)


def _mix_kernel(x_ref, g1_ref, wgate_ref, attn_ref, yp_ref, wba_ref, wbp_ref, wout_ref,
                g2_ref, wr_ref, br_ref, h1_ref, u2_ref, gate_ref):
    x = x_ref[...]
    u = _rms(x, g1_ref[...]).astype(BF16)
    gates = 1.0 / (1.0 + jnp.exp(-_dot(u, wgate_ref[...])))
    a = _dot(attn_ref[...], wba_ref[...])
    bp = _dot(yp_ref[...], wbp_ref[...])
    merged = gates[:, :D_MODEL] * a + gates[:, D_MODEL:] * bp
    h1 = x + _dot(merged.astype(BF16), wout_ref[...])
    h1_ref[...] = h1
    u2 = _rms(h1, g2_ref[...])
    u2_ref[...] = u2.astype(BF16)

    lg = jnp.dot(u2, wr_ref[...], preferred_element_type=F32,
                 precision=lax.Precision.HIGHEST) + br_ref[...]
    lane = lax.broadcasted_iota(jnp.int32, lg.shape, 1)
    is_g = lane < N_GROUPS
    gl = jnp.where(is_g, lg, NEG)
    gmax = jnp.max(gl, axis=1, keepdims=True)
    gidx = jnp.min(jnp.where(gl == gmax, lane, LANES), axis=1, keepdims=True)
    p_g = 1.0 / jnp.sum(jnp.where(is_g, jnp.exp(gl - gmax), 0.0), axis=1, keepdims=True)
    e_lane = lane - N_GROUPS
    lane_grp = jnp.where(e_lane >= 0,
                         jnp.where(e_lane < N_EXPERTS, e_lane // EXPERTS_PER_GROUP, -1), -1)
    in_grp = lane_grp == gidx
    el = jnp.where(in_grp, lg, NEG)
    t1 = jnp.max(el, axis=1, keepdims=True)
    i1 = jnp.min(jnp.where(el == t1, lane, LANES), axis=1, keepdims=True)
    el2 = jnp.where(lane == i1, NEG, el)
    t2 = jnp.max(el2, axis=1, keepdims=True)
    i2 = jnp.min(jnp.where(el2 == t2, lane, LANES), axis=1, keepdims=True)
    r = jnp.exp(t2 - t1)
    p1 = 1.0 / (1.0 + r)
    p2 = r * p1
    gate_ref[...] = jnp.where(lane == i1, p1 * p_g, jnp.where(lane == i2, p2 * p_g, 0.0))


def _moe_kernel(u2_ref, gate_ref, h1_ref, wg_ref, wu_ref, wd_ref, gf_ref, o_ref, acc_ref):
    e = pl.program_id(1)

    @pl.when(e == 0)
    def _():
        acc_ref[...] = h1_ref[...]

    x = u2_ref[...]
    a = _dot(x, wg_ref[...])
    b = _dot(x, wu_ref[...])
    gate = gate_ref[...]
    lane = lax.broadcasted_iota(jnp.int32, gate.shape, 1)
    gcol = jnp.sum(jnp.where(lane == e + N_GROUPS, gate, 0.0), axis=1, keepdims=True)
    hg = a * (1.0 / (1.0 + jnp.exp(-a))) * b * gcol
    acc_ref[...] += _dot(hg.astype(BF16), wd_ref[...])

    @pl.when(e == pl.num_programs(1) - 1)
    def _():
        o_ref[...] = _rms(acc_ref[...], gf_ref[...])


def _const_spec(shape):
    nd = len(shape)
    return pl.BlockSpec(shape, lambda *_: (0,) * nd)


def _params(n_axes):
    return pltpu.CompilerParams(dimension_semantics=("arbitrary",) * n_axes,
                                vmem_limit_bytes=VMEM_LIMIT)


def kernel(x, meta_tokens, norm1_g, w_in, kv_norm_g, w_uk, w_uv, w_pool, pool_scale,
           w_branch_attn, w_branch_pool, w_out, norm2_g, w_group_router, b_group_router,
           w_expert_router, b_expert_router, w_expert_gate, w_expert_up, w_expert_down,
           final_norm_g):
    B, S, D = x.shape
    assert D == D_MODEL and S % QB == 0 and S % TM_PROJ == 0 and w_in.shape[0] == 1
    assert QB == KC and TM_PROJ % KC == 0 and QB % CHUNK == 0
    N = B * S
    k_top = min(TOPK_MAX, S // 4)
    xr = x.reshape(N, D)

    wi = w_in[0]
    w1 = jnp.concatenate(
        [wi[:, 0:640], wi[:, 640:896], wi[:, 936:1448], wi[:, 896:936],
         jnp.zeros((D, W1_WIDTH - 1448), F32)], axis=1).astype(BF16)
    wgate = wi[:, 1448:].astype(BF16)
    g1 = norm1_g[0].reshape(1, D)
    kvg = kv_norm_g[0].reshape(1, KV_LATENT)
    wpool = w_pool[0].astype(BF16)
    pscale = pool_scale[0].reshape(1, POOL_WIDTH)
    wuk = jnp.transpose(w_uk[0], (1, 0, 2)).astype(BF16)
    wuvt = jnp.transpose(w_uv[0], (1, 2, 0)).astype(BF16)
    wr = jnp.concatenate(
        [w_group_router[0], w_expert_router[0].reshape(D, N_EXPERTS),
         jnp.zeros((D, LANES - N_GROUPS - N_EXPERTS), F32)], axis=1)
    br = jnp.concatenate(
        [b_group_router[0], b_expert_router[0].reshape(N_EXPERTS),
         jnp.zeros((LANES - N_GROUPS - N_EXPERTS,), F32)]).reshape(1, LANES)

    c_m, tail_m, pv_m = pl.pallas_call(
        _meta_kernel,
        out_shape=(jax.ShapeDtypeStruct((N_META, KV_LATENT), F32),
                   jax.ShapeDtypeStruct((N_META, LANES), F32),
                   jax.ShapeDtypeStruct((N_META, POOL_WIDTH), F32)),
        name="meta",
    )(meta_tokens, g1, w1, kvg)
    cm = c_m.astype(BF16)
    cmt = cm.T
    ikm = tail_m[:, :IDX_DIM].astype(BF16)

    tpb = S // TM_PROJ
    tok = lambda w: pl.BlockSpec((TM_PROJ, w), lambda b, i: (b * tpb + i, 0))
    tok_t = lambda w: pl.BlockSpec((w, TM_PROJ), lambda b, i: (0, b * tpb + i))
    qt, c, ct, iqt, ik, iwt, yp = pl.pallas_call(
        _proj_kernel,
        grid=(B, tpb),
        in_specs=[tok(D), _const_spec((1, D)), _const_spec((D, W1_WIDTH)),
                  _const_spec((1, KV_LATENT)), _const_spec((N_META, POOL_WIDTH)),
                  _const_spec((len(POOL_WINDOWS), POOL_GROUP, POOL_GROUP)),
                  _const_spec((1, POOL_WIDTH))],
        out_specs=[tok_t(ATTN_WIDTH), tok(KV_LATENT),
                   pl.BlockSpec((None, TM_PROJ // KC, KV_LATENT, KC), lambda b, i: (b, i, 0, 0)),
                   tok_t(IDX_HEADS * IDX_DIM), tok(IDX_DIM), tok_t(IDX_HEADS), tok(POOL_WIDTH)],
        out_shape=(jax.ShapeDtypeStruct((ATTN_WIDTH, N), BF16),
                   jax.ShapeDtypeStruct((N, KV_LATENT), BF16),
                   jax.ShapeDtypeStruct((B, S // KC, KV_LATENT, KC), BF16),
                   jax.ShapeDtypeStruct((IDX_HEADS * IDX_DIM, N), BF16),
                   jax.ShapeDtypeStruct((N, IDX_DIM), BF16),
                   jax.ShapeDtypeStruct((IDX_HEADS, N), F32),
                   jax.ShapeDtypeStruct((N, POOL_WIDTH), BF16)),
        scratch_shapes=[pltpu.VMEM((TM_PROJ + N_META, POOL_WIDTH), F32)],
        compiler_params=_params(2),
        name="proj",
    )(xr, g1, w1, kvg, pv_m, wpool, pscale)

    nqb = S // QB
    qcol = lambda w: pl.BlockSpec((w, QB), lambda b, j: (0, b * nqb + j))
    attn = pl.pallas_call(
        functools.partial(_attn_kernel, k_top=float(k_top)),
        grid=(B, nqb),
        in_specs=[pl.BlockSpec((S, IDX_DIM), lambda b, j: (b, 0)),
                  _const_spec((N_META, IDX_DIM)),
                  pl.BlockSpec((S, KV_LATENT), lambda b, j: (b, 0)),
                  _const_spec((N_META, KV_LATENT)),
                  pl.BlockSpec((None, S // KC, KV_LATENT, KC), lambda b, j: (b, 0, 0, 0)),
                  _const_spec((KV_LATENT, N_META)),
                  qcol(IDX_HEADS * IDX_DIM), qcol(IDX_HEADS), qcol(ATTN_WIDTH),
                  _const_spec((N_HEADS, KV_LATENT, HEAD_DIM)),
                  _const_spec((N_HEADS, HEAD_DIM, KV_LATENT))],
        out_specs=pl.BlockSpec((QB, ATTN_WIDTH), lambda b, j: (b * nqb + j, 0)),
        out_shape=jax.ShapeDtypeStruct((N, ATTN_WIDTH), BF16),
        scratch_shapes=[pltpu.VMEM((S + N_META, QB), F32),
                        pltpu.VMEM((KV_LATENT, N_HEADS * QB), BF16),
                        pltpu.VMEM((KC, N_HEADS * QB), F32),
                        pltpu.VMEM((KC, N_HEADS * QB), BF16),
                        pltpu.VMEM((1, N_HEADS * QB), F32), pltpu.VMEM((1, N_HEADS * QB), F32),
                        pltpu.VMEM((1, N_HEADS * QB), F32),
                        pltpu.VMEM((KV_LATENT, N_HEADS * QB), F32),
                        pltpu.VMEM((ATTN_WIDTH, QB), F32)],
        compiler_params=_params(2),
        name="attn",
    )(ik, ikm, c, cm, ct, cmt, iqt, iwt, qt, wuk, wuvt)

    nt = N // TM_PROJ
    tok1 = lambda w: pl.BlockSpec((TM_PROJ, w), lambda i: (i, 0))
    h1, u2, gate = pl.pallas_call(
        _mix_kernel,
        grid=(nt,),
        in_specs=[tok1(D), _const_spec((1, D)), _const_spec((D, 2 * D)), tok1(ATTN_WIDTH),
                  tok1(POOL_WIDTH), _const_spec((ATTN_WIDTH, D)), _const_spec((POOL_WIDTH, D)),
                  _const_spec((D, D)), _const_spec((1, D)), _const_spec((D, LANES)),
                  _const_spec((1, LANES))],
        out_specs=[tok1(D), tok1(D), tok1(LANES)],
        out_shape=(jax.ShapeDtypeStruct((N, D), F32), jax.ShapeDtypeStruct((N, D), BF16),
                   jax.ShapeDtypeStruct((N, LANES), F32)),
        compiler_params=_params(1),
        name="mix",
    )(xr, g1, wgate, attn, yp, w_branch_attn[0].astype(BF16), w_branch_pool[0].astype(BF16),
      w_out[0].astype(BF16), norm2_g[0].reshape(1, D), wr, br)

    weg = w_expert_gate[0].reshape(N_EXPERTS, D, EXPERT_HIDDEN).astype(BF16)
    weu = w_expert_up[0].reshape(N_EXPERTS, D, EXPERT_HIDDEN).astype(BF16)
    wed = w_expert_down[0].reshape(N_EXPERTS, EXPERT_HIDDEN, D).astype(BF16)
    mtok = lambda w: pl.BlockSpec((TM_MOE, w), lambda i, e: (i, 0))
    out = pl.pallas_call(
        _moe_kernel,
        grid=(N // TM_MOE, N_EXPERTS),
        in_specs=[mtok(D), mtok(LANES), mtok(D),
                  pl.BlockSpec((None, D, EXPERT_HIDDEN), lambda i, e: (e, 0, 0)),
                  pl.BlockSpec((None, D, EXPERT_HIDDEN), lambda i, e: (e, 0, 0)),
                  pl.BlockSpec((None, EXPERT_HIDDEN, D), lambda i, e: (e, 0, 0)),
                  pl.BlockSpec((1, D), lambda i, e: (0, 0))],
        out_specs=mtok(D),
        out_shape=jax.ShapeDtypeStruct((N, D), F32),
        scratch_shapes=[pltpu.VMEM((TM_MOE, D), F32)],
        compiler_params=_params(2),
        name="moe",
    )(u2, gate, h1, weg, weu, wed, final_norm_g.reshape(1, D))
    return out.reshape(B, S, D)
```

```python
import functools

import jax
import jax.numpy as jnp
from jax import lax
from jax.experimental import pallas as pl
from jax.experimental.pallas import tpu as pltpu

F32 = jnp.float32
BF16 = jnp.bfloat16

D_MODEL = 1024
CHUNK = 64
N_META = 16
N_HEADS = 8
HEAD_DIM = 64
ATTN_WIDTH = N_HEADS * HEAD_DIM
KV_LATENT = 128
IDX_HEADS = 8
IDX_DIM = 32
TOPK_MAX = 256
ATTN_SCALE = HEAD_DIM ** -0.5
IDX_SCALE = (IDX_HEADS ** -0.5) * (IDX_DIM ** -0.5)
POOL_WINDOWS = (2, 4, 8, 16)
POOL_WIDTH = 512
POOL_GROUP = 128
N_GROUPS = 4
EXPERTS_PER_GROUP = 8
N_EXPERTS = N_GROUPS * EXPERTS_PER_GROUP
EXPERT_HIDDEN = 256
EPS = 1e-6

LANES = 128
SUBLANES = 8
W1_WIDTH = 1536
NEG = -1e30
POS = 1e30
VMEM_LIMIT = 56 * 1024 * 1024

TM_PROJ = 512
QB = 256
KC = 256
MAX_BISECT = 40
BISECT_PER_CHECK = 4
TS_MOE = 2048
TMX_MOE = 128
LIST_PAD = 1024
TM_FINAL = 1024


def _rms(x, g):
    return x * lax.rsqrt(jnp.mean(x * x, axis=-1, keepdims=True) + EPS) * g


def _dot(a, b):
    return jnp.dot(a, b, preferred_element_type=F32)


def _meta_kernel(m_ref, g1_ref, w1_ref, kvg_ref, c_ref, tail_ref, pv_ref):
    u = _rms(m_ref[...], g1_ref[...]).astype(BF16)
    p = _dot(u, w1_ref[...])
    c_ref[...] = _rms(p[:, 512:640], kvg_ref[...])
    pv_ref[...] = p[:, 896:1408]
    tail_ref[...] = p[:, 1408:1536]


def _proj_kernel(x_ref, g1_ref, w1_ref, kvg_ref, pvmeta_ref, wpool_ref, pscale_ref,
                 qt_ref, c_ref, ct_ref, iqt_ref, ik_ref, iwt_ref, yp_ref, ext_ref):
    tm = x_ref.shape[0]
    u = _rms(x_ref[...], g1_ref[...]).astype(BF16)
    p = _dot(u, w1_ref[...])
    qt_ref[...] = p[:, 0:512].T.astype(BF16)
    c = _rms(p[:, 512:640], kvg_ref[...])
    c_ref[...] = c.astype(BF16)
    ct = c.T.astype(BF16)
    for k in range(tm // KC):
        ct_ref[k] = ct[:, k * KC:(k + 1) * KC]
    iqt_ref[...] = p[:, 640:896].T.astype(BF16)
    tail = p[:, 1408:1536]
    ik_ref[...] = tail[:, 0:IDX_DIM].astype(BF16)
    iwt_ref[...] = tail.T[IDX_DIM:IDX_DIM + IDX_HEADS, :] * IDX_SCALE
    pv = p[:, 896:1408]

    @pl.when(pl.program_id(1) == 0)
    def _():
        ext_ref[0:N_META, :] = pvmeta_ref[...]

    ext_ref[N_META:N_META + tm, :] = pv
    for g, w in enumerate(POOL_WINDOWS):
        cols = slice(g * POOL_GROUP, (g + 1) * POOL_GROUP)
        acc = pv[:, cols]
        for k in range(1, w):
            acc = acc + ext_ref[N_META - k:N_META - k + tm, cols]
        d = acc * (1.0 / w) - pv[:, cols]
        y = _dot(d.astype(BF16), wpool_ref[g]) * pscale_ref[:, cols]
        yp_ref[:, cols] = y.astype(BF16)
    ext_ref[0:N_META, :] = ext_ref[tm:tm + N_META, :]


def _attn_kernel(ik_ref, ikm_ref, c_ref, cm_ref, ct_ref, cmt_ref, iqt_ref, iwt_ref, qt_ref,
                 wuk_ref, wuvt_ref, o_ref,
                 s_scr, qat_scr, lg_scr, p_scr, m_scr, l_scr, a_scr, acc_scr, ot_scr, *, k_top):
    n_real = c_ref.shape[0]
    qb = iqt_ref.shape[1]
    j = pl.program_id(1)
    nkc = j + 1
    qchunk = (j * qb + lax.broadcasted_iota(jnp.int32, (1, qb), 1)) // CHUNK
    meta_rows = pl.ds(n_real, N_META)

    def rows(kc):
        return pl.ds(pl.multiple_of(kc * KC, KC), KC)

    def col_sum(x):
        r = x.shape[0]
        if r > SUBLANES:
            x = jnp.sum(x.reshape(r // SUBLANES, SUBLANES, qb), axis=0)
        return jnp.sum(x, axis=0, keepdims=True)

    iqt_heads = [iqt_ref[h * IDX_DIM:(h + 1) * IDX_DIM, :] for h in range(IDX_HEADS)]

    def scores(ik_rows):
        acc = None
        for h in range(IDX_HEADS):
            t = jnp.maximum(_dot(ik_rows, iqt_heads[h]), 0.0) * iwt_ref[h:h + 1, :]
            acc = t if acc is None else acc + t
        return acc

    sm = scores(ikm_ref[...])
    s_scr[meta_rows, :] = sm
    mn0 = jnp.min(sm, axis=0, keepdims=True)
    mx0 = jnp.max(sm, axis=0, keepdims=True)

    def score_body(kc, carry):
        mn, mx = carry
        sc = scores(ik_ref[rows(kc), :])
        kchunk = (kc * KC + lax.broadcasted_iota(jnp.int32, (KC, 1), 0)) // CHUNK
        adm = kchunk <= qchunk
        s_scr[rows(kc), :] = jnp.where(adm, sc, NEG)
        mn = jnp.minimum(mn, jnp.min(jnp.where(adm, sc, POS), axis=0, keepdims=True))
        mx = jnp.maximum(mx, jnp.max(jnp.where(adm, sc, NEG), axis=0, keepdims=True))
        return mn, mx

    mn, mx = lax.fori_loop(0, nkc, score_body, (mn0, mx0))

    def count_ge(th):
        def body(kc, cnt):
            return cnt + jnp.sum(
                jnp.where(s_scr[rows(kc), :] >= th, 1.0, 0.0).reshape(KC // SUBLANES, SUBLANES, qb),
                axis=0)
        cnt = lax.fori_loop(0, nkc, body, jnp.zeros((SUBLANES, qb), F32))
        return (jnp.sum(cnt, axis=0, keepdims=True)
                + jnp.sum(jnp.where(s_scr[meta_rows, :] >= th, 1.0, 0.0), axis=0, keepdims=True))

    def pending_of(cnt_lo):
        return jnp.max(jnp.where(cnt_lo > k_top, 1, 0))

    def bisect_cond(carry):
        it, _, _, _, pending = carry
        return jnp.logical_and(it < MAX_BISECT, pending > 0)

    def bisect_body(carry):
        it, lo, hi, cnt_lo, _ = carry
        for _ in range(BISECT_PER_CHECK):
            mid = lo + (hi - lo) * 0.5
            cnt = count_ge(mid)
            ge = cnt >= k_top
            lo = jnp.where(ge, mid, lo)
            hi = jnp.where(ge, hi, mid)
            cnt_lo = jnp.where(ge, cnt, cnt_lo)
        return it + BISECT_PER_CHECK, lo, hi, cnt_lo, pending_of(cnt_lo)

    n_adm = (N_META + CHUNK * (qchunk + 1)).astype(F32)
    hi0 = mx + (mx - mn) + (jnp.abs(mx) * (2.0 ** -10) + 1e-30)
    _, lo, hi, _, pending = lax.while_loop(
        bisect_cond, bisect_body, (jnp.int32(0), mn, hi0, n_adm, pending_of(n_adm)))

    @pl.when(pending == 0)
    def _():
        def body(kc, _):
            s_scr[rows(kc), :] = jnp.where(s_scr[rows(kc), :] >= lo, 0.0, NEG)
            return 0
        lax.fori_loop(0, nkc, body, 0)
        s_scr[meta_rows, :] = jnp.where(s_scr[meta_rows, :] >= lo, 0.0, NEG)

    @pl.when(pending > 0)
    def _():
        need = k_top - count_ge(hi)

        def pick(sv, tri, before):
            band = jnp.where(sv >= lo, jnp.where(sv < hi, 1.0, 0.0), 0.0)
            rank = _dot(tri, band.astype(BF16)) + before
            take = jnp.where(rank <= need, band, 0.0)
            bias = jnp.where(sv >= hi, 0.0, jnp.where(take > 0.5, 0.0, NEG))
            return bias, before + jnp.sum(band, axis=0, keepdims=True)

        def tri(n):
            return jnp.where(lax.broadcasted_iota(jnp.int32, (n, n), 0)
                             >= lax.broadcasted_iota(jnp.int32, (n, n), 1), 1.0, 0.0).astype(BF16)

        bias_m, before = pick(s_scr[meta_rows, :], tri(N_META), jnp.zeros((1, qb), F32))
        s_scr[meta_rows, :] = bias_m
        tri_kc = tri(KC)

        def body(kc, before):
            bias, before = pick(s_scr[rows(kc), :], tri_kc, before)
            s_scr[rows(kc), :] = bias
            return before
        lax.fori_loop(0, nkc, body, before)

    def head(h):
        return slice(h * qb, (h + 1) * qb)

    for h in range(N_HEADS):
        qat_scr[:, head(h)] = (_dot(wuk_ref[h], qt_ref[h * HEAD_DIM:(h + 1) * HEAD_DIM, :])
                               * ATTN_SCALE).astype(BF16)
    m_scr[...] = jnp.full(m_scr.shape, 0.5 * NEG, F32)
    l_scr[...] = jnp.zeros(l_scr.shape, F32)
    acc_scr[...] = jnp.zeros(acc_scr.shape, F32)

    def attend(c_rows, ct_cols, bias):
        r = c_rows.shape[0]
        lg_scr[0:r, :] = _dot(c_rows, qat_scr[...])
        for h in range(N_HEADS):
            lg = lg_scr[0:r, head(h)] + bias
            m_old = m_scr[:, head(h)]
            m_new = jnp.maximum(m_old, jnp.max(lg, axis=0, keepdims=True))
            p = jnp.exp(lg - m_new)
            alpha = jnp.exp(m_old - m_new)
            l_scr[:, head(h)] = l_scr[:, head(h)] * alpha + col_sum(p)
            p_scr[0:r, head(h)] = p.astype(BF16)
            a_scr[:, head(h)] = alpha
            m_scr[:, head(h)] = m_new
        acc_scr[...] = acc_scr[...] * a_scr[...] + _dot(ct_cols, p_scr[0:r, :])

    attend(cm_ref[...], cmt_ref[...], s_scr[meta_rows, :])

    def attend_body(kc, _):
        attend(c_ref[rows(kc), :], ct_ref[kc], s_scr[rows(kc), :])
        return 0
    lax.fori_loop(0, nkc, attend_body, 0)

    for h in range(N_HEADS):
        olat = acc_scr[:, head(h)] / l_scr[:, head(h)]
        ot_scr[h * HEAD_DIM:(h + 1) * HEAD_DIM, :] = _dot(wuvt_ref[h], olat.astype(BF16))
    o_ref[...] = ot_scr[...].T.astype(BF16)


def _mix_kernel(x_ref, g1_ref, wgate_ref, attn_ref, yp_ref, wba_ref, wbp_ref, wout_ref,
                g2_ref, wr_ref, br_ref, h1_ref, u2_ref, route_ref):
    x = x_ref[...]
    u = _rms(x, g1_ref[...]).astype(BF16)
    gates = 1.0 / (1.0 + jnp.exp(-_dot(u, wgate_ref[...])))
    a = _dot(attn_ref[...], wba_ref[...])
    bp = _dot(yp_ref[...], wbp_ref[...])
    merged = gates[:, :D_MODEL] * a + gates[:, D_MODEL:] * bp
    h1 = x + _dot(merged.astype(BF16), wout_ref[...])
    h1_ref[...] = h1
    u2 = _rms(h1, g2_ref[...])
    u2_ref[...] = u2.astype(BF16)

    lg = jnp.dot(u2, wr_ref[...], preferred_element_type=F32,
                 precision=lax.Precision.HIGHEST) + br_ref[...]
    lane = lax.broadcasted_iota(jnp.int32, lg.shape, 1)
    is_g = lane < N_GROUPS
    gl = jnp.where(is_g, lg, NEG)
    gmax = jnp.max(gl, axis=1, keepdims=True)
    gidx = jnp.min(jnp.where(gl == gmax, lane, LANES), axis=1, keepdims=True)
    p_g = 1.0 / jnp.sum(jnp.where(is_g, jnp.exp(gl - gmax), 0.0), axis=1, keepdims=True)
    e_lane = lane - N_GROUPS
    lane_grp = jnp.where(e_lane >= 0,
                         jnp.where(e_lane < N_EXPERTS, e_lane // EXPERTS_PER_GROUP, -1), -1)
    in_grp = lane_grp == gidx
    el = jnp.where(in_grp, lg, NEG)
    t1 = jnp.max(el, axis=1, keepdims=True)
    i1 = jnp.min(jnp.where(el == t1, lane, LANES), axis=1, keepdims=True)
    el2 = jnp.where(lane == i1, NEG, el)
    t2 = jnp.max(el2, axis=1, keepdims=True)
    i2 = jnp.min(jnp.where(el2 == t2, lane, LANES), axis=1, keepdims=True)
    r = jnp.exp(t2 - t1)
    p1 = 1.0 / (1.0 + r)
    p2 = r * p1
    e1 = (i1 - N_GROUPS).astype(F32)
    e2 = (i2 - N_GROUPS).astype(F32)
    route_ref[...] = jnp.where(lane == 0, e1, jnp.where(lane == 1, e2, jnp.where(
        lane == 2, p1 * p_g, jnp.where(lane == 3, p2 * p_g, 0.0))))


def _moe_kernel(start_ref, cnt_ref, tok_ref, slot_ref, wt_ref, u2_ref, wg_ref, wu_ref, wd_ref,
                y_ref, x2_scr, r2_scr, g2_scr, rt_scr):
    ts = u2_ref.shape[0]
    tmx = g2_scr.shape[0] // SUBLANES
    st = pl.program_id(0)
    e = pl.program_id(1)
    n_e = pl.num_programs(1)
    nsub = D_MODEL // LANES

    def vreg_rows(i):
        return pl.ds(pl.multiple_of(i * SUBLANES, SUBLANES), SUBLANES)

    @pl.when(e == 0)
    def _():
        for s in range(nsub):
            x2_scr[pl.ds(s, ts, stride=nsub), :] = (
                u2_ref[:, s * LANES:(s + 1) * LANES].astype(F32))

    start = start_ref[st * n_e + e]
    cnt = cnt_ref[st * n_e + e]

    def tile_body(i, _):
        base = start + i * tmx

        def gather_body(r8, _):
            for u in range(SUBLANES):
                r = r8 * SUBLANES + u
                g2_scr[vreg_rows(r), :] = x2_scr[vreg_rows(tok_ref[base + r]), :]
            return 0
        lax.fori_loop(0, tmx // SUBLANES, gather_body, 0)

        xg = jnp.concatenate([g2_scr[pl.ds(s, tmx, stride=nsub), :] for s in range(nsub)],
                             axis=1).astype(BF16)
        a = _dot(xg, wg_ref[...])
        b = _dot(xg, wu_ref[...])
        hg = (a * (1.0 / (1.0 + jnp.exp(-a))) * b).astype(BF16)
        yr = _dot(hg, wd_ref[...])
        for s in range(nsub):
            rt_scr[pl.ds(s, tmx, stride=nsub), :] = yr[:, s * LANES:(s + 1) * LANES]
        r2_scr[pl.ds(pl.multiple_of(base * SUBLANES, SUBLANES), tmx * SUBLANES), :] = rt_scr[...]
        return 0

    lax.fori_loop(0, (cnt + tmx - 1) // tmx, tile_body, 0)

    @pl.when(e == n_e - 1)
    def _():
        def combine_body(t8, _):
            for u in range(SUBLANES):
                t = t8 * SUBLANES + u
                x2_scr[vreg_rows(t), :] = (
                    wt_ref[2 * t] * r2_scr[vreg_rows(slot_ref[2 * t]), :]
                    + wt_ref[2 * t + 1] * r2_scr[vreg_rows(slot_ref[2 * t + 1]), :])
            return 0
        lax.fori_loop(0, ts // SUBLANES, combine_body, 0)
        for s in range(nsub):
            y_ref[:, s * LANES:(s + 1) * LANES] = x2_scr[pl.ds(s, ts, stride=nsub), :]


def _final_kernel(h1_ref, y_ref, gf_ref, o_ref):
    o_ref[...] = _rms(h1_ref[...] + y_ref[...], gf_ref[...])


def _const_spec(shape):
    nd = len(shape)
    return pl.BlockSpec(shape, lambda *_: (0,) * nd)


def _params(n_axes):
    return pltpu.CompilerParams(dimension_semantics=("arbitrary",) * n_axes,
                                vmem_limit_bytes=VMEM_LIMIT)


def kernel(x, meta_tokens, norm1_g, w_in, kv_norm_g, w_uk, w_uv, w_pool, pool_scale,
           w_branch_attn, w_branch_pool, w_out, norm2_g, w_group_router, b_group_router,
           w_expert_router, b_expert_router, w_expert_gate, w_expert_up, w_expert_down,
           final_norm_g):
    B, S, D = x.shape
    assert D == D_MODEL and S % QB == 0 and S % TM_PROJ == 0 and w_in.shape[0] == 1
    assert QB == KC and TM_PROJ % KC == 0 and QB % CHUNK == 0
    N = B * S
    k_top = min(TOPK_MAX, S // 4)
    xr = x.reshape(N, D)

    wi = w_in[0]
    w1 = jnp.concatenate(
        [wi[:, 0:640], wi[:, 640:896], wi[:, 936:1448], wi[:, 896:936],
         jnp.zeros((D, W1_WIDTH - 1448), F32)], axis=1).astype(BF16)
    wgate = wi[:, 1448:].astype(BF16)
    g1 = norm1_g[0].reshape(1, D)
    kvg = kv_norm_g[0].reshape(1, KV_LATENT)
    wpool = w_pool[0].astype(BF16)
    pscale = pool_scale[0].reshape(1, POOL_WIDTH)
    wuk = jnp.transpose(w_uk[0], (1, 0, 2)).astype(BF16)
    wuvt = jnp.transpose(w_uv[0], (1, 2, 0)).astype(BF16)
    wr = jnp.concatenate(
        [w_group_router[0], w_expert_router[0].reshape(D, N_EXPERTS),
         jnp.zeros((D, LANES - N_GROUPS - N_EXPERTS), F32)], axis=1)
    br = jnp.concatenate(
        [b_group_router[0], b_expert_router[0].reshape(N_EXPERTS),
         jnp.zeros((LANES - N_GROUPS - N_EXPERTS,), F32)]).reshape(1, LANES)

    c_m, tail_m, pv_m = pl.pallas_call(
        _meta_kernel,
        out_shape=(jax.ShapeDtypeStruct((N_META, KV_LATENT), F32),
                   jax.ShapeDtypeStruct((N_META, LANES), F32),
                   jax.ShapeDtypeStruct((N_META, POOL_WIDTH), F32)),
        name="meta",
    )(meta_tokens, g1, w1, kvg)
    cm = c_m.astype(BF16)
    cmt = cm.T
    ikm = tail_m[:, :IDX_DIM].astype(BF16)

    tpb = S // TM_PROJ
    tok = lambda w: pl.BlockSpec((TM_PROJ, w), lambda b, i: (b * tpb + i, 0))
    tok_t = lambda w: pl.BlockSpec((w, TM_PROJ), lambda b, i: (0, b * tpb + i))
    qt, c, ct, iqt, ik, iwt, yp = pl.pallas_call(
        _proj_kernel,
        grid=(B, tpb),
        in_specs=[tok(D), _const_spec((1, D)), _const_spec((D, W1_WIDTH)),
                  _const_spec((1, KV_LATENT)), _const_spec((N_META, POOL_WIDTH)),
                  _const_spec((len(POOL_WINDOWS), POOL_GROUP, POOL_GROUP)),
                  _const_spec((1, POOL_WIDTH))],
        out_specs=[tok_t(ATTN_WIDTH), tok(KV_LATENT),
                   pl.BlockSpec((None, TM_PROJ // KC, KV_LATENT, KC), lambda b, i: (b, i, 0, 0)),
                   tok_t(IDX_HEADS * IDX_DIM), tok(IDX_DIM), tok_t(IDX_HEADS), tok(POOL_WIDTH)],
        out_shape=(jax.ShapeDtypeStruct((ATTN_WIDTH, N), BF16),
                   jax.ShapeDtypeStruct((N, KV_LATENT), BF16),
                   jax.ShapeDtypeStruct((B, S // KC, KV_LATENT, KC), BF16),
                   jax.ShapeDtypeStruct((IDX_HEADS * IDX_DIM, N), BF16),
                   jax.ShapeDtypeStruct((N, IDX_DIM), BF16),
                   jax.ShapeDtypeStruct((IDX_HEADS, N), F32),
                   jax.ShapeDtypeStruct((N, POOL_WIDTH), BF16)),
        scratch_shapes=[pltpu.VMEM((TM_PROJ + N_META, POOL_WIDTH), F32)],
        compiler_params=_params(2),
        name="proj",
    )(xr, g1, w1, kvg, pv_m, wpool, pscale)

    nqb = S // QB
    qcol = lambda w: pl.BlockSpec((w, QB), lambda b, j: (0, b * nqb + j))
    attn = pl.pallas_call(
        functools.partial(_attn_kernel, k_top=float(k_top)),
        grid=(B, nqb),
        in_specs=[pl.BlockSpec((S, IDX_DIM), lambda b, j: (b, 0)),
                  _const_spec((N_META, IDX_DIM)),
                  pl.BlockSpec((S, KV_LATENT), lambda b, j: (b, 0)),
                  _const_spec((N_META, KV_LATENT)),
                  pl.BlockSpec((None, S // KC, KV_LATENT, KC), lambda b, j: (b, 0, 0, 0)),
                  _const_spec((KV_LATENT, N_META)),
                  qcol(IDX_HEADS * IDX_DIM), qcol(IDX_HEADS), qcol(ATTN_WIDTH),
                  _const_spec((N_HEADS, KV_LATENT, HEAD_DIM)),
                  _const_spec((N_HEADS, HEAD_DIM, KV_LATENT))],
        out_specs=pl.BlockSpec((QB, ATTN_WIDTH), lambda b, j: (b * nqb + j, 0)),
        out_shape=jax.ShapeDtypeStruct((N, ATTN_WIDTH), BF16),
        scratch_shapes=[pltpu.VMEM((S + N_META, QB), F32),
                        pltpu.VMEM((KV_LATENT, N_HEADS * QB), BF16),
                        pltpu.VMEM((KC, N_HEADS * QB), F32),
                        pltpu.VMEM((KC, N_HEADS * QB), BF16),
                        pltpu.VMEM((1, N_HEADS * QB), F32), pltpu.VMEM((1, N_HEADS * QB), F32),
                        pltpu.VMEM((1, N_HEADS * QB), F32),
                        pltpu.VMEM((KV_LATENT, N_HEADS * QB), F32),
                        pltpu.VMEM((ATTN_WIDTH, QB), F32)],
        compiler_params=_params(2),
        name="attn",
    )(ik, ikm, c, cm, ct, cmt, iqt, iwt, qt, wuk, wuvt)

    nt = N // TM_PROJ
    tok1 = lambda w: pl.BlockSpec((TM_PROJ, w), lambda i: (i, 0))
    h1, u2, route = pl.pallas_call(
        _mix_kernel,
        grid=(nt,),
        in_specs=[tok1(D), _const_spec((1, D)), _const_spec((D, 2 * D)), tok1(ATTN_WIDTH),
                  tok1(POOL_WIDTH), _const_spec((ATTN_WIDTH, D)), _const_spec((POOL_WIDTH, D)),
                  _const_spec((D, D)), _const_spec((1, D)), _const_spec((D, LANES)),
                  _const_spec((1, LANES))],
        out_specs=[tok1(D), tok1(D), tok1(LANES)],
        out_shape=(jax.ShapeDtypeStruct((N, D), F32), jax.ShapeDtypeStruct((N, D), BF16),
                   jax.ShapeDtypeStruct((N, LANES), F32)),
        compiler_params=_params(1),
        name="mix",
    )(xr, g1, wgate, attn, yp, w_branch_attn[0].astype(BF16), w_branch_pool[0].astype(BF16),
      w_out[0].astype(BF16), norm2_g[0].reshape(1, D), wr, br)

    nst = N // TS_MOE
    n_asg = 2 * TS_MOE
    eid = route[:, 0:2].astype(jnp.int32).reshape(nst, n_asg)
    wts = route[:, 2:4].reshape(nst * n_asg)
    order = jnp.argsort(eid, axis=1, stable=True).astype(jnp.int32)
    slot = jnp.argsort(order, axis=1).astype(jnp.int32).reshape(nst * n_asg)
    tok_sorted = jnp.pad(order // 2, ((0, 0), (0, LIST_PAD))).reshape(nst * (n_asg + LIST_PAD))
    counts = jnp.sum(eid[:, :, None] == jnp.arange(N_EXPERTS, dtype=jnp.int32)[None, None, :],
                     axis=1, dtype=jnp.int32)
    starts = (jnp.cumsum(counts, axis=1) - counts).reshape(nst * N_EXPERTS)
    counts = counts.reshape(nst * N_EXPERTS)

    weg = w_expert_gate[0].reshape(N_EXPERTS, D, EXPERT_HIDDEN).astype(BF16)
    weu = w_expert_up[0].reshape(N_EXPERTS, D, EXPERT_HIDDEN).astype(BF16)
    wed = w_expert_down[0].reshape(N_EXPERTS, EXPERT_HIDDEN, D).astype(BF16)
    smem = lambda n: pl.BlockSpec((n,), lambda s, e, *_: (s,), memory_space=pltpu.SMEM)
    y = pl.pallas_call(
        _moe_kernel,
        grid_spec=pltpu.PrefetchScalarGridSpec(
            num_scalar_prefetch=2,
            grid=(nst, N_EXPERTS),
            in_specs=[smem(n_asg + LIST_PAD), smem(n_asg), smem(n_asg),
                      pl.BlockSpec((TS_MOE, D), lambda s, e, *_: (s, 0)),
                      pl.BlockSpec((None, D, EXPERT_HIDDEN), lambda s, e, *_: (e, 0, 0)),
                      pl.BlockSpec((None, D, EXPERT_HIDDEN), lambda s, e, *_: (e, 0, 0)),
                      pl.BlockSpec((None, EXPERT_HIDDEN, D), lambda s, e, *_: (e, 0, 0))],
            out_specs=pl.BlockSpec((TS_MOE, D), lambda s, e, *_: (s, 0)),
            scratch_shapes=[pltpu.VMEM((TS_MOE * SUBLANES, LANES), F32),
                            pltpu.VMEM(((n_asg + TMX_MOE) * SUBLANES, LANES), F32),
                            pltpu.VMEM((TMX_MOE * SUBLANES, LANES), F32),
                            pltpu.VMEM((TMX_MOE * SUBLANES, LANES), F32)]),
        out_shape=jax.ShapeDtypeStruct((N, D), F32),
        compiler_params=_params(2),
        name="moe",
    )(starts, counts, tok_sorted, slot, wts, u2, weg, weu, wed)

    ftok = pl.BlockSpec((TM_FINAL, D), lambda i: (i, 0))
    out = pl.pallas_call(
        _final_kernel,
        grid=(N // TM_FINAL,),
        in_specs=[ftok, ftok, _const_spec((1, D))],
        out_specs=ftok,
        out_shape=jax.ShapeDtypeStruct((N, D), F32),
        compiler_params=_params(1),
        name="final",
    )(h1, y, final_norm_g.reshape(1, D))
    return out.reshape(B, S, D)
```

```python
import functools

import jax
import jax.numpy as jnp
from jax import lax
from jax.experimental import pallas as pl
from jax.experimental.pallas import tpu as pltpu

F32 = jnp.float32
BF16 = jnp.bfloat16

D_MODEL = 1024
CHUNK = 64
N_META = 16
N_HEADS = 8
HEAD_DIM = 64
ATTN_WIDTH = N_HEADS * HEAD_DIM
KV_LATENT = 128
IDX_HEADS = 8
IDX_DIM = 32
TOPK_MAX = 256
ATTN_SCALE = HEAD_DIM ** -0.5
IDX_SCALE = (IDX_HEADS ** -0.5) * (IDX_DIM ** -0.5)
POOL_WINDOWS = (2, 4, 8, 16)
POOL_WIDTH = 512
POOL_GROUP = 128
N_GROUPS = 4
EXPERTS_PER_GROUP = 8
N_EXPERTS = N_GROUPS * EXPERTS_PER_GROUP
EXPERT_HIDDEN = 256
EPS = 1e-6

LANES = 128
SUBLANES = 8
W1_WIDTH = 1536
NEG = -1e30
POS = 1e30
VMEM_LIMIT = 56 * 1024 * 1024

TM_PROJ = 512
QB = 256
KC = 256
MAX_BISECT = 40
BISECT_PER_CHECK = 4
FIRST_TIE_CHECK = 16
CT_ROWS = KV_LATENT + 16
LOG2E = 1.4426950408889634
TS_MOE = 2048
TMX_MOE = 128
LIST_PAD = 1024
TM_FINAL = 1024


def _rms(x, g):
    return x * lax.rsqrt(jnp.mean(x * x, axis=-1, keepdims=True) + EPS) * g


def _dot(a, b):
    return jnp.dot(a, b, preferred_element_type=F32)


def _meta_kernel(m_ref, g1_ref, w1_ref, kvg_ref, c_ref, tail_ref, pv_ref):
    u = _rms(m_ref[...], g1_ref[...]).astype(BF16)
    p = _dot(u, w1_ref[...])
    c_ref[...] = _rms(p[:, 512:640], kvg_ref[...])
    pv_ref[...] = p[:, 896:1408]
    tail_ref[...] = p[:, 1408:1536]


def _proj_kernel(x_ref, g1_ref, w1_ref, kvg_ref, pvmeta_ref, wpool_ref, pscale_ref,
                 qt_ref, c_ref, ct_ref, iqt_ref, ik_ref, iwt_ref, yp_ref, ext_ref):
    tm = x_ref.shape[0]
    u = _rms(x_ref[...], g1_ref[...]).astype(BF16)
    p = _dot(u, w1_ref[...])
    qt_ref[...] = p[:, 0:512].T.astype(BF16)
    c = _rms(p[:, 512:640], kvg_ref[...])
    c_ref[...] = c.astype(BF16)
    ct = jnp.concatenate([c.T, jnp.ones((1, tm), F32),
                          jnp.zeros((CT_ROWS - KV_LATENT - 1, tm), F32)], axis=0).astype(BF16)
    for k in range(tm // KC):
        ct_ref[k] = ct[:, k * KC:(k + 1) * KC]
    iqt_ref[...] = p[:, 640:896].T.astype(BF16)
    tail = p[:, 1408:1536]
    ik_ref[...] = tail[:, 0:IDX_DIM].astype(BF16)
    iwt_ref[...] = tail.T[IDX_DIM:IDX_DIM + IDX_HEADS, :] * IDX_SCALE
    pv = p[:, 896:1408]

    @pl.when(pl.program_id(1) == 0)
    def _():
        ext_ref[0:N_META, :] = pvmeta_ref[...]

    ext_ref[N_META:N_META + tm, :] = pv
    for g, w in enumerate(POOL_WINDOWS):
        cols = slice(g * POOL_GROUP, (g + 1) * POOL_GROUP)
        acc = pv[:, cols]
        for k in range(1, w):
            acc = acc + ext_ref[N_META - k:N_META - k + tm, cols]
        d = acc * (1.0 / w) - pv[:, cols]
        y = _dot(d.astype(BF16), wpool_ref[g]) * pscale_ref[:, cols]
        yp_ref[:, cols] = y.astype(BF16)
    ext_ref[0:N_META, :] = ext_ref[tm:tm + N_META, :]


def _attn_kernel(ik_ref, ikm_ref, c_ref, cm_ref, ct_ref, cmt_ref, iqt_ref, iwt_ref, qt_ref,
                 wuk_ref, wuvt_ref, o_ref,
                 s_scr, qat_scr, lg_scr, p_scr, m_scr, a_scr, acc_scr, ot_scr, *, k_top):
    n_real = c_ref.shape[0]
    qb = iqt_ref.shape[1]
    j = pl.program_id(1)
    nkc = j + 1
    qchunk = (j * qb + lax.broadcasted_iota(jnp.int32, (1, qb), 1)) // CHUNK
    meta_rows = pl.ds(n_real, N_META)

    def rows(kc):
        return pl.ds(pl.multiple_of(kc * KC, KC), KC)

    def fold(x, op):
        groups = x.shape[0] // SUBLANES
        chains = 4 if groups % 4 == 0 else 1
        x = x.reshape(groups // chains, chains, SUBLANES, qb)
        return op(op(x, axis=0), axis=0)

    iqt_heads = [iqt_ref[h * IDX_DIM:(h + 1) * IDX_DIM, :] for h in range(IDX_HEADS)]

    def scores(ik_rows):
        acc = None
        for h in range(IDX_HEADS):
            t = jnp.maximum(_dot(ik_rows, iqt_heads[h]), 0.0) * iwt_ref[h:h + 1, :]
            acc = t if acc is None else acc + t
        return acc

    sm = scores(ikm_ref[...])
    s_scr[meta_rows, :] = sm
    mn0 = jnp.min(sm, axis=0, keepdims=True)
    mx0 = jnp.max(sm, axis=0, keepdims=True)

    def score_body(kc, carry):
        mn, mx = carry
        sc = scores(ik_ref[rows(kc), :])
        s_scr[rows(kc), :] = sc
        return jnp.minimum(mn, fold(sc, jnp.min)), jnp.maximum(mx, fold(sc, jnp.max))

    mn8, mx8 = lax.fori_loop(0, j, score_body, (jnp.full((SUBLANES, qb), POS, F32),
                                                jnp.full((SUBLANES, qb), NEG, F32)))
    sc = scores(ik_ref[rows(j), :])
    adm = (j * KC + lax.broadcasted_iota(jnp.int32, (KC, 1), 0)) // CHUNK <= qchunk
    s_scr[rows(j), :] = jnp.where(adm, sc, NEG)
    mn8 = jnp.minimum(mn8, fold(jnp.where(adm, sc, POS), jnp.min))
    mx8 = jnp.maximum(mx8, fold(jnp.where(adm, sc, NEG), jnp.max))
    mn = jnp.minimum(mn0, jnp.min(mn8, axis=0, keepdims=True))
    mx = jnp.maximum(mx0, jnp.max(mx8, axis=0, keepdims=True))

    def count_ge(th):
        def body(kc, cnt):
            return cnt + fold(jnp.where(s_scr[rows(kc), :] >= th, 1.0, 0.0), jnp.sum)
        cnt = lax.fori_loop(0, nkc, body,
                            fold(jnp.where(s_scr[meta_rows, :] >= th, 1.0, 0.0), jnp.sum))
        return jnp.sum(cnt, axis=0, keepdims=True)

    def band_extent(lo, hi):
        def ext(sv):
            return (fold(jnp.where(sv >= lo, sv, POS), jnp.min),
                    fold(jnp.where(sv < hi, sv, NEG), jnp.max))

        def body(kc, carry):
            bmin, bmax = ext(s_scr[rows(kc), :])
            return jnp.minimum(carry[0], bmin), jnp.maximum(carry[1], bmax)
        bmin, bmax = lax.fori_loop(0, nkc, body, ext(s_scr[meta_rows, :]))
        return jnp.min(bmin, axis=0, keepdims=True), jnp.max(bmax, axis=0, keepdims=True)

    def any_lane(flags):
        return jnp.max(flags)

    def bisect_cond(carry):
        it, _, _, _, pending = carry
        return jnp.logical_and(it < MAX_BISECT, pending > 0)

    def bisect_body(carry):
        it, lo, hi, cnt_lo, _ = carry
        for _ in range(BISECT_PER_CHECK):
            mid = lo + (hi - lo) * 0.5
            cnt = count_ge(mid)
            ge = cnt >= k_top
            lo = jnp.where(ge, mid, lo)
            hi = jnp.where(ge, hi, mid)
            cnt_lo = jnp.where(ge, cnt, cnt_lo)
        it = it + BISECT_PER_CHECK
        over = jnp.where(cnt_lo > k_top, 1, 0)
        pending = any_lane(over)

        def tied_check():
            bmin, bmax = band_extent(lo, hi)
            return any_lane(jnp.where(bmin < bmax, over, 0))

        pending = lax.cond(jnp.logical_and(pending > 0, it >= FIRST_TIE_CHECK),
                           tied_check, lambda: pending)
        return it, lo, hi, cnt_lo, pending

    n_adm = (N_META + CHUNK * (qchunk + 1)).astype(F32)
    hi0 = mx + (mx - mn) + (jnp.abs(mx) * (2.0 ** -10) + 1e-30)
    _, lo, hi, cnt_lo, _ = lax.while_loop(
        bisect_cond, bisect_body,
        (jnp.int32(0), mn, hi0, n_adm, any_lane(jnp.where(n_adm > k_top, 1, 0))))
    pending = any_lane(jnp.where(cnt_lo > k_top, 1, 0))

    @pl.when(pending == 0)
    def _():
        def body(kc, _):
            s_scr[rows(kc), :] = jnp.where(s_scr[rows(kc), :] >= lo, 0.0, NEG)
            return 0
        lax.fori_loop(0, nkc, body, 0)
        s_scr[meta_rows, :] = jnp.where(s_scr[meta_rows, :] >= lo, 0.0, NEG)

    @pl.when(pending > 0)
    def _():
        need = k_top - count_ge(hi)

        def pick(sv, tri, before):
            band = jnp.where(sv >= lo, jnp.where(sv < hi, 1.0, 0.0), 0.0)
            rank = _dot(tri, band.astype(BF16)) + before
            take = jnp.where(rank <= need, band, 0.0)
            bias = jnp.where(sv >= hi, 0.0, jnp.where(take > 0.5, 0.0, NEG))
            return bias, before + jnp.sum(fold(band, jnp.sum), axis=0, keepdims=True)

        def tri(n):
            return jnp.where(lax.broadcasted_iota(jnp.int32, (n, n), 0)
                             >= lax.broadcasted_iota(jnp.int32, (n, n), 1), 1.0, 0.0).astype(BF16)

        bias_m, before = pick(s_scr[meta_rows, :], tri(N_META), jnp.zeros((1, qb), F32))
        s_scr[meta_rows, :] = bias_m
        tri_kc = tri(KC)

        def body(kc, before):
            bias, before = pick(s_scr[rows(kc), :], tri_kc, before)
            s_scr[rows(kc), :] = bias
            return before
        lax.fori_loop(0, nkc, body, before)

    def head(h):
        return slice(h * qb, (h + 1) * qb)

    for h in range(N_HEADS):
        qat_scr[:, head(h)] = (_dot(wuk_ref[h], qt_ref[h * HEAD_DIM:(h + 1) * HEAD_DIM, :])
                               * (ATTN_SCALE * LOG2E)).astype(BF16)
    m_scr[...] = jnp.full(m_scr.shape, 0.5 * NEG, F32)
    acc_scr[...] = jnp.zeros(acc_scr.shape, F32)

    def attend(c_rows, ct_cols, bias):
        r = c_rows.shape[0]
        lg_scr[0:r, :] = _dot(c_rows, qat_scr[...])
        for h in range(N_HEADS):
            lg = lg_scr[0:r, head(h)] + bias
            m_old = m_scr[:, head(h)]
            m_new = jnp.maximum(m_old, jnp.max(lg, axis=0, keepdims=True))
            p_scr[0:r, head(h)] = jnp.exp2(lg - m_new).astype(BF16)
            a_scr[:, head(h)] = jnp.exp2(m_old - m_new)
            m_scr[:, head(h)] = m_new
        acc_scr[...] = acc_scr[...] * a_scr[...] + _dot(ct_cols, p_scr[0:r, :])

    attend(cm_ref[...], cmt_ref[...], s_scr[meta_rows, :])

    def attend_body(kc, _):
        attend(c_ref[rows(kc), :], ct_ref[kc], s_scr[rows(kc), :])
        return 0
    lax.fori_loop(0, nkc, attend_body, 0)

    for h in range(N_HEADS):
        olat = acc_scr[0:KV_LATENT, head(h)] / acc_scr[KV_LATENT:KV_LATENT + 1, head(h)]
        ot_scr[h * HEAD_DIM:(h + 1) * HEAD_DIM, :] = _dot(wuvt_ref[h], olat.astype(BF16))
    o_ref[...] = ot_scr[...].T.astype(BF16)


def _mix_kernel(x_ref, g1_ref, wgate_ref, attn_ref, yp_ref, wba_ref, wbp_ref, wout_ref,
                g2_ref, wr_ref, br_ref, h1_ref, u2_ref, route_ref):
    x = x_ref[...]
    u = _rms(x, g1_ref[...]).astype(BF16)
    gates = 1.0 / (1.0 + jnp.exp(-_dot(u, wgate_ref[...])))
    a = _dot(attn_ref[...], wba_ref[...])
    bp = _dot(yp_ref[...], wbp_ref[...])
    merged = gates[:, :D_MODEL] * a + gates[:, D_MODEL:] * bp
    h1 = x + _dot(merged.astype(BF16), wout_ref[...])
    h1_ref[...] = h1
    u2 = _rms(h1, g2_ref[...])
    u2_ref[...] = u2.astype(BF16)

    lg = jnp.dot(u2, wr_ref[...], preferred_element_type=F32,
                 precision=lax.Precision.HIGHEST) + br_ref[...]
    lane = lax.broadcasted_iota(jnp.int32, lg.shape, 1)
    is_g = lane < N_GROUPS
    gl = jnp.where(is_g, lg, NEG)
    gmax = jnp.max(gl, axis=1, keepdims=True)
    gidx = jnp.min(jnp.where(gl == gmax, lane, LANES), axis=1, keepdims=True)
    p_g = 1.0 / jnp.sum(jnp.where(is_g, jnp.exp(gl - gmax), 0.0), axis=1, keepdims=True)
    e_lane = lane - N_GROUPS
    lane_grp = jnp.where(e_lane >= 0,
                         jnp.where(e_lane < N_EXPERTS, e_lane // EXPERTS_PER_GROUP, -1), -1)
    in_grp = lane_grp == gidx
    el = jnp.where(in_grp, lg, NEG)
    t1 = jnp.max(el, axis=1, keepdims=True)
    i1 = jnp.min(jnp.where(el == t1, lane, LANES), axis=1, keepdims=True)
    el2 = jnp.where(lane == i1, NEG, el)
    t2 = jnp.max(el2, axis=1, keepdims=True)
    i2 = jnp.min(jnp.where(el2 == t2, lane, LANES), axis=1, keepdims=True)
    r = jnp.exp(t2 - t1)
    p1 = 1.0 / (1.0 + r)
    p2 = r * p1
    e1 = (i1 - N_GROUPS).astype(F32)
    e2 = (i2 - N_GROUPS).astype(F32)
    route_ref[...] = jnp.where(lane == 0, e1, jnp.where(lane == 1, e2, jnp.where(
        lane == 2, p1 * p_g, jnp.where(lane == 3, p2 * p_g, 0.0))))


def _moe_kernel(start_ref, cnt_ref, tok_ref, slot_ref, wt_ref, u2_ref, wg_ref, wu_ref, wd_ref,
                y_ref, x2_scr, r2_scr, g2_scr, rt_scr):
    ts = u2_ref.shape[0]
    tmx = g2_scr.shape[0] // SUBLANES
    st = pl.program_id(0)
    e = pl.program_id(1)
    n_e = pl.num_programs(1)
    nsub = D_MODEL // LANES

    def vreg_rows(i):
        return pl.ds(pl.multiple_of(i * SUBLANES, SUBLANES), SUBLANES)

    @pl.when(e == 0)
    def _():
        for s in range(nsub):
            x2_scr[pl.ds(s, ts, stride=nsub), :] = (
                u2_ref[:, s * LANES:(s + 1) * LANES].astype(F32))

    start = start_ref[st * n_e + e]
    cnt = cnt_ref[st * n_e + e]

    def tile_body(i, _):
        base = start + i * tmx

        def gather_body(r8, _):
            for u in range(SUBLANES):
                r = r8 * SUBLANES + u
                g2_scr[vreg_rows(r), :] = x2_scr[vreg_rows(tok_ref[base + r]), :]
            return 0
        lax.fori_loop(0, tmx // SUBLANES, gather_body, 0)

        xg = jnp.concatenate([g2_scr[pl.ds(s, tmx, stride=nsub), :] for s in range(nsub)],
                             axis=1).astype(BF16)
        a = _dot(xg, wg_ref[...])
        b = _dot(xg, wu_ref[...])
        hg = (a * (1.0 / (1.0 + jnp.exp(-a))) * b).astype(BF16)
        yr = _dot(hg, wd_ref[...])
        for s in range(nsub):
            rt_scr[pl.ds(s, tmx, stride=nsub), :] = yr[:, s * LANES:(s + 1) * LANES]
        r2_scr[pl.ds(pl.multiple_of(base * SUBLANES, SUBLANES), tmx * SUBLANES), :] = rt_scr[...]
        return 0

    lax.fori_loop(0, (cnt + tmx - 1) // tmx, tile_body, 0)

    @pl.when(e == n_e - 1)
    def _():
        def combine_body(t8, _):
            for u in range(SUBLANES):
                t = t8 * SUBLANES + u
                x2_scr[vreg_rows(t), :] = (
                    wt_ref[2 * t] * r2_scr[vreg_rows(slot_ref[2 * t]), :]
                    + wt_ref[2 * t + 1] * r2_scr[vreg_rows(slot_ref[2 * t + 1]), :])
            return 0
        lax.fori_loop(0, ts // SUBLANES, combine_body, 0)
        for s in range(nsub):
            y_ref[:, s * LANES:(s + 1) * LANES] = x2_scr[pl.ds(s, ts, stride=nsub), :]


def _final_kernel(h1_ref, y_ref, gf_ref, o_ref):
    o_ref[...] = _rms(h1_ref[...] + y_ref[...], gf_ref[...])


def _const_spec(shape):
    nd = len(shape)
    return pl.BlockSpec(shape, lambda *_: (0,) * nd)


def _params(n_axes):
    return pltpu.CompilerParams(dimension_semantics=("arbitrary",) * n_axes,
                                vmem_limit_bytes=VMEM_LIMIT)


def kernel(x, meta_tokens, norm1_g, w_in, kv_norm_g, w_uk, w_uv, w_pool, pool_scale,
           w_branch_attn, w_branch_pool, w_out, norm2_g, w_group_router, b_group_router,
           w_expert_router, b_expert_router, w_expert_gate, w_expert_up, w_expert_down,
           final_norm_g):
    B, S, D = x.shape
    assert D == D_MODEL and S % QB == 0 and S % TM_PROJ == 0 and w_in.shape[0] == 1
    assert QB == KC and TM_PROJ % KC == 0 and QB % CHUNK == 0
    N = B * S
    k_top = min(TOPK_MAX, S // 4)
    xr = x.reshape(N, D)

    wi = w_in[0]
    w1 = jnp.concatenate(
        [wi[:, 0:640], wi[:, 640:896], wi[:, 936:1448], wi[:, 896:936],
         jnp.zeros((D, W1_WIDTH - 1448), F32)], axis=1).astype(BF16)
    wgate = wi[:, 1448:].astype(BF16)
    g1 = norm1_g[0].reshape(1, D)
    kvg = kv_norm_g[0].reshape(1, KV_LATENT)
    wpool = w_pool[0].astype(BF16)
    pscale = pool_scale[0].reshape(1, POOL_WIDTH)
    wuk = jnp.transpose(w_uk[0], (1, 0, 2)).astype(BF16)
    wuvt = jnp.transpose(w_uv[0], (1, 2, 0)).astype(BF16)
    wr = jnp.concatenate(
        [w_group_router[0], w_expert_router[0].reshape(D, N_EXPERTS),
         jnp.zeros((D, LANES - N_GROUPS - N_EXPERTS), F32)], axis=1)
    br = jnp.concatenate(
        [b_group_router[0], b_expert_router[0].reshape(N_EXPERTS),
         jnp.zeros((LANES - N_GROUPS - N_EXPERTS,), F32)]).reshape(1, LANES)

    c_m, tail_m, pv_m = pl.pallas_call(
        _meta_kernel,
        out_shape=(jax.ShapeDtypeStruct((N_META, KV_LATENT), F32),
                   jax.ShapeDtypeStruct((N_META, LANES), F32),
                   jax.ShapeDtypeStruct((N_META, POOL_WIDTH), F32)),
        name="meta",
    )(meta_tokens, g1, w1, kvg)
    cm = c_m.astype(BF16)
    cmt = jnp.concatenate([cm.T, jnp.ones((1, N_META), BF16),
                           jnp.zeros((CT_ROWS - KV_LATENT - 1, N_META), BF16)], axis=0)
    ikm = tail_m[:, :IDX_DIM].astype(BF16)

    tpb = S // TM_PROJ
    tok = lambda w: pl.BlockSpec((TM_PROJ, w), lambda b, i: (b * tpb + i, 0))
    tok_t = lambda w: pl.BlockSpec((w, TM_PROJ), lambda b, i: (0, b * tpb + i))
    qt, c, ct, iqt, ik, iwt, yp = pl.pallas_call(
        _proj_kernel,
        grid=(B, tpb),
        in_specs=[tok(D), _const_spec((1, D)), _const_spec((D, W1_WIDTH)),
                  _const_spec((1, KV_LATENT)), _const_spec((N_META, POOL_WIDTH)),
                  _const_spec((len(POOL_WINDOWS), POOL_GROUP, POOL_GROUP)),
                  _const_spec((1, POOL_WIDTH))],
        out_specs=[tok_t(ATTN_WIDTH), tok(KV_LATENT),
                   pl.BlockSpec((None, TM_PROJ // KC, CT_ROWS, KC), lambda b, i: (b, i, 0, 0)),
                   tok_t(IDX_HEADS * IDX_DIM), tok(IDX_DIM), tok_t(IDX_HEADS), tok(POOL_WIDTH)],
        out_shape=(jax.ShapeDtypeStruct((ATTN_WIDTH, N), BF16),
                   jax.ShapeDtypeStruct((N, KV_LATENT), BF16),
                   jax.ShapeDtypeStruct((B, S // KC, CT_ROWS, KC), BF16),
                   jax.ShapeDtypeStruct((IDX_HEADS * IDX_DIM, N), BF16),
                   jax.ShapeDtypeStruct((N, IDX_DIM), BF16),
                   jax.ShapeDtypeStruct((IDX_HEADS, N), F32),
                   jax.ShapeDtypeStruct((N, POOL_WIDTH), BF16)),
        scratch_shapes=[pltpu.VMEM((TM_PROJ + N_META, POOL_WIDTH), F32)],
        compiler_params=_params(2),
        name="proj",
    )(xr, g1, w1, kvg, pv_m, wpool, pscale)

    nqb = S // QB
    qcol = lambda w: pl.BlockSpec((w, QB), lambda b, j: (0, b * nqb + j))
    attn = pl.pallas_call(
        functools.partial(_attn_kernel, k_top=float(k_top)),
        grid=(B, nqb),
        in_specs=[pl.BlockSpec((S, IDX_DIM), lambda b, j: (b, 0)),
                  _const_spec((N_META, IDX_DIM)),
                  pl.BlockSpec((S, KV_LATENT), lambda b, j: (b, 0)),
                  _const_spec((N_META, KV_LATENT)),
                  pl.BlockSpec((None, S // KC, CT_ROWS, KC), lambda b, j: (b, 0, 0, 0)),
                  _const_spec((CT_ROWS, N_META)),
                  qcol(IDX_HEADS * IDX_DIM), qcol(IDX_HEADS), qcol(ATTN_WIDTH),
                  _const_spec((N_HEADS, KV_LATENT, HEAD_DIM)),
                  _const_spec((N_HEADS, HEAD_DIM, KV_LATENT))],
        out_specs=pl.BlockSpec((QB, ATTN_WIDTH), lambda b, j: (b * nqb + j, 0)),
        out_shape=jax.ShapeDtypeStruct((N, ATTN_WIDTH), BF16),
        scratch_shapes=[pltpu.VMEM((S + N_META, QB), F32),
                        pltpu.VMEM((KV_LATENT, N_HEADS * QB), BF16),
                        pltpu.VMEM((KC, N_HEADS * QB), F32),
                        pltpu.VMEM((KC, N_HEADS * QB), BF16),
                        pltpu.VMEM((1, N_HEADS * QB), F32), pltpu.VMEM((1, N_HEADS * QB), F32),
                        pltpu.VMEM((CT_ROWS, N_HEADS * QB), F32),
                        pltpu.VMEM((ATTN_WIDTH, QB), F32)],
        compiler_params=_params(2),
        name="attn",
    )(ik, ikm, c, cm, ct, cmt, iqt, iwt, qt, wuk, wuvt)

    nt = N // TM_PROJ
    tok1 = lambda w: pl.BlockSpec((TM_PROJ, w), lambda i: (i, 0))
    h1, u2, route = pl.pallas_call(
        _mix_kernel,
        grid=(nt,),
        in_specs=[tok1(D), _const_spec((1, D)), _const_spec((D, 2 * D)), tok1(ATTN_WIDTH),
                  tok1(POOL_WIDTH), _const_spec((ATTN_WIDTH, D)), _const_spec((POOL_WIDTH, D)),
                  _const_spec((D, D)), _const_spec((1, D)), _const_spec((D, LANES)),
                  _const_spec((1, LANES))],
        out_specs=[tok1(D), tok1(D), tok1(LANES)],
        out_shape=(jax.ShapeDtypeStruct((N, D), F32), jax.ShapeDtypeStruct((N, D), BF16),
                   jax.ShapeDtypeStruct((N, LANES), F32)),
        compiler_params=_params(1),
        name="mix",
    )(xr, g1, wgate, attn, yp, w_branch_attn[0].astype(BF16), w_branch_pool[0].astype(BF16),
      w_out[0].astype(BF16), norm2_g[0].reshape(1, D), wr, br)

    nst = N // TS_MOE
    n_asg = 2 * TS_MOE
    eid = route[:, 0:2].astype(jnp.int32).reshape(nst, n_asg)
    wts = route[:, 2:4].reshape(nst * n_asg)
    order = jnp.argsort(eid, axis=1, stable=True).astype(jnp.int32)
    slot = jnp.argsort(order, axis=1).astype(jnp.int32).reshape(nst * n_asg)
    tok_sorted = jnp.pad(order // 2, ((0, 0), (0, LIST_PAD))).reshape(nst * (n_asg + LIST_PAD))
    counts = jnp.sum(eid[:, :, None] == jnp.arange(N_EXPERTS, dtype=jnp.int32)[None, None, :],
                     axis=1, dtype=jnp.int32)
    starts = (jnp.cumsum(counts, axis=1) - counts).reshape(nst * N_EXPERTS)
    counts = counts.reshape(nst * N_EXPERTS)

    weg = w_expert_gate[0].reshape(N_EXPERTS, D, EXPERT_HIDDEN).astype(BF16)
    weu = w_expert_up[0].reshape(N_EXPERTS, D, EXPERT_HIDDEN).astype(BF16)
    wed = w_expert_down[0].reshape(N_EXPERTS, EXPERT_HIDDEN, D).astype(BF16)
    smem = lambda n: pl.BlockSpec((n,), lambda s, e, *_: (s,), memory_space=pltpu.SMEM)
    y = pl.pallas_call(
        _moe_kernel,
        grid_spec=pltpu.PrefetchScalarGridSpec(
            num_scalar_prefetch=2,
            grid=(nst, N_EXPERTS),
            in_specs=[smem(n_asg + LIST_PAD), smem(n_asg), smem(n_asg),
                      pl.BlockSpec((TS_MOE, D), lambda s, e, *_: (s, 0)),
                      pl.BlockSpec((None, D, EXPERT_HIDDEN), lambda s, e, *_: (e, 0, 0)),
                      pl.BlockSpec((None, D, EXPERT_HIDDEN), lambda s, e, *_: (e, 0, 0)),
                      pl.BlockSpec((None, EXPERT_HIDDEN, D), lambda s, e, *_: (e, 0, 0))],
            out_specs=pl.BlockSpec((TS_MOE, D), lambda s, e, *_: (s, 0)),
            scratch_shapes=[pltpu.VMEM((TS_MOE * SUBLANES, LANES), F32),
                            pltpu.VMEM(((n_asg + TMX_MOE) * SUBLANES, LANES), F32),
                            pltpu.VMEM((TMX_MOE * SUBLANES, LANES), F32),
                            pltpu.VMEM((TMX_MOE * SUBLANES, LANES), F32)]),
        out_shape=jax.ShapeDtypeStruct((N, D), F32),
        compiler_params=_params(2),
        name="moe",
    )(starts, counts, tok_sorted, slot, wts, u2, weg, weu, wed)

    ftok = pl.BlockSpec((TM_FINAL, D), lambda i: (i, 0))
    out = pl.pallas_call(
        _final_kernel,
        grid=(N // TM_FINAL,),
        in_specs=[ftok, ftok, _const_spec((1, D))],
        out_specs=ftok,
        out_shape=jax.ShapeDtypeStruct((N, D), F32),
        compiler_params=_params(1),
        name="final",
    )(h1, y, final_norm_g.reshape(1, D))
    return out.reshape(B, S, D)
```

```python
import functools

import jax
import jax.numpy as jnp
from jax import lax
from jax.experimental import pallas as pl
from jax.experimental.pallas import tpu as pltpu

F32 = jnp.float32
BF16 = jnp.bfloat16

D_MODEL = 1024
CHUNK = 64
N_META = 16
N_HEADS = 8
HEAD_DIM = 64
ATTN_WIDTH = N_HEADS * HEAD_DIM
KV_LATENT = 128
IDX_HEADS = 8
IDX_DIM = 32
TOPK_MAX = 256
ATTN_SCALE = HEAD_DIM ** -0.5
IDX_SCALE = (IDX_HEADS ** -0.5) * (IDX_DIM ** -0.5)
POOL_WINDOWS = (2, 4, 8, 16)
POOL_WIDTH = 512
POOL_GROUP = 128
N_GROUPS = 4
EXPERTS_PER_GROUP = 8
N_EXPERTS = N_GROUPS * EXPERTS_PER_GROUP
EXPERT_HIDDEN = 256
EPS = 1e-6

LANES = 128
SUBLANES = 8
W1_WIDTH = 1536
NEG = -1e30
POS = 1e30
VMEM_LIMIT = 56 * 1024 * 1024

TM_PROJ = 512
TM_MIX = 256
MIX_COLS = 256
QB = 256
KC = 256
MAX_BISECT = 40
BISECT_PER_CHECK = 4
FIRST_TIE_CHECK = 16
CT_ROWS = KV_LATENT + 16
LOG2E = 1.4426950408889634
TS_MOE = 2048
TMX_MOE = 160
LIST_PAD = 1024
TM_FINAL = 1024


def _rms(x, g):
    return x * lax.rsqrt(jnp.mean(x * x, axis=-1, keepdims=True) + EPS) * g


def _dot(a, b):
    return jnp.dot(a, b, preferred_element_type=F32)


def _meta_kernel(m_ref, g1_ref, w1_ref, kvg_ref, c_ref, tail_ref, pv_ref):
    u = _rms(m_ref[...], g1_ref[...]).astype(BF16)
    p = _dot(u, w1_ref[...])
    c_ref[...] = _rms(p[:, 512:640], kvg_ref[...])
    pv_ref[...] = p[:, 896:1408]
    tail_ref[...] = p[:, 1408:1536]


def _proj_kernel(x_ref, g1_ref, w1_ref, kvg_ref, pvmeta_ref, wpool_ref, pscale_ref,
                 qt_ref, c_ref, ct_ref, iqt_ref, ik_ref, iwt_ref, yp_ref, ext_ref):
    tm = x_ref.shape[0]
    u = _rms(x_ref[...], g1_ref[...]).astype(BF16)
    p = _dot(u, w1_ref[...])
    qt_ref[...] = p[:, 0:512].T.astype(BF16)
    c = _rms(p[:, 512:640], kvg_ref[...])
    c_ref[...] = c.astype(BF16)
    ct = jnp.concatenate([c.T, jnp.ones((1, tm), F32),
                          jnp.zeros((CT_ROWS - KV_LATENT - 1, tm), F32)], axis=0).astype(BF16)
    for k in range(tm // KC):
        ct_ref[k] = ct[:, k * KC:(k + 1) * KC]
    iqt_ref[...] = p[:, 640:896].T.astype(BF16)
    tail = p[:, 1408:1536]
    ik_ref[...] = tail[:, 0:IDX_DIM].astype(BF16)
    iwt_ref[...] = tail.T[IDX_DIM:IDX_DIM + IDX_HEADS, :] * IDX_SCALE
    pv = p[:, 896:1408]

    @pl.when(pl.program_id(1) == 0)
    def _():
        ext_ref[0:N_META, :] = pvmeta_ref[...]

    ext_ref[N_META:N_META + tm, :] = pv
    for g, w in enumerate(POOL_WINDOWS):
        cols = slice(g * POOL_GROUP, (g + 1) * POOL_GROUP)
        acc = pv[:, cols]
        for k in range(1, w):
            acc = acc + ext_ref[N_META - k:N_META - k + tm, cols]
        d = acc * (1.0 / w) - pv[:, cols]
        y = _dot(d.astype(BF16), wpool_ref[g]) * pscale_ref[:, cols]
        yp_ref[:, cols] = y.astype(BF16)
    ext_ref[0:N_META, :] = ext_ref[tm:tm + N_META, :]


def _attn_kernel(ik_ref, ikm_ref, c_ref, cm_ref, ct_ref, cmt_ref, iqt_ref, iwt_ref, qt_ref,
                 wuk_ref, wuvt_ref, o_ref,
                 s_scr, qat_scr, lg_scr, p_scr, m_scr, a_scr, acc_scr, ot_scr, *, k_top):
    n_real = c_ref.shape[0]
    qb = iqt_ref.shape[1]
    j = pl.program_id(1)
    nkc = j + 1
    qchunk = (j * qb + lax.broadcasted_iota(jnp.int32, (1, qb), 1)) // CHUNK
    meta_rows = pl.ds(n_real, N_META)

    def rows(kc):
        return pl.ds(pl.multiple_of(kc * KC, KC), KC)

    def fold(x, op):
        groups = x.shape[0] // SUBLANES
        chains = 4 if groups % 4 == 0 else 1
        x = x.reshape(groups // chains, chains, SUBLANES, qb)
        return op(op(x, axis=0), axis=0)

    iqt_heads = [iqt_ref[h * IDX_DIM:(h + 1) * IDX_DIM, :] for h in range(IDX_HEADS)]

    def scores(ik_rows):
        acc = None
        for h in range(IDX_HEADS):
            t = jnp.maximum(_dot(ik_rows, iqt_heads[h]), 0.0) * iwt_ref[h:h + 1, :]
            acc = t if acc is None else acc + t
        return acc

    sm = scores(ikm_ref[...])
    s_scr[meta_rows, :] = sm
    mn0 = jnp.min(sm, axis=0, keepdims=True)
    mx0 = jnp.max(sm, axis=0, keepdims=True)

    def score_body(kc, carry):
        mn, mx = carry
        sc = scores(ik_ref[rows(kc), :])
        s_scr[rows(kc), :] = sc
        return jnp.minimum(mn, fold(sc, jnp.min)), jnp.maximum(mx, fold(sc, jnp.max))

    mn8, mx8 = lax.fori_loop(0, j, score_body, (jnp.full((SUBLANES, qb), POS, F32),
                                                jnp.full((SUBLANES, qb), NEG, F32)))
    sc = scores(ik_ref[rows(j), :])
    adm = (j * KC + lax.broadcasted_iota(jnp.int32, (KC, 1), 0)) // CHUNK <= qchunk
    s_scr[rows(j), :] = jnp.where(adm, sc, NEG)
    mn8 = jnp.minimum(mn8, fold(jnp.where(adm, sc, POS), jnp.min))
    mx8 = jnp.maximum(mx8, fold(jnp.where(adm, sc, NEG), jnp.max))
    mn = jnp.minimum(mn0, jnp.min(mn8, axis=0, keepdims=True))
    mx = jnp.maximum(mx0, jnp.max(mx8, axis=0, keepdims=True))

    def count_ge(th):
        def body(kc, cnt):
            return cnt + fold(jnp.where(s_scr[rows(kc), :] >= th, 1.0, 0.0), jnp.sum)
        cnt = lax.fori_loop(0, nkc, body,
                            fold(jnp.where(s_scr[meta_rows, :] >= th, 1.0, 0.0), jnp.sum))
        return jnp.sum(cnt, axis=0, keepdims=True)

    def band_extent(lo, hi):
        def ext(sv):
            return (fold(jnp.where(sv >= lo, sv, POS), jnp.min),
                    fold(jnp.where(sv < hi, sv, NEG), jnp.max))

        def body(kc, carry):
            bmin, bmax = ext(s_scr[rows(kc), :])
            return jnp.minimum(carry[0], bmin), jnp.maximum(carry[1], bmax)
        bmin, bmax = lax.fori_loop(0, nkc, body, ext(s_scr[meta_rows, :]))
        return jnp.min(bmin, axis=0, keepdims=True), jnp.max(bmax, axis=0, keepdims=True)

    def any_lane(flags):
        return jnp.max(flags)

    def bisect_cond(carry):
        it, _, _, _, pending = carry
        return jnp.logical_and(it < MAX_BISECT, pending > 0)

    def bisect_body(carry):
        it, lo, hi, cnt_lo, _ = carry
        for _ in range(BISECT_PER_CHECK):
            mid = lo + (hi - lo) * 0.5
            cnt = count_ge(mid)
            ge = cnt >= k_top
            lo = jnp.where(ge, mid, lo)
            hi = jnp.where(ge, hi, mid)
            cnt_lo = jnp.where(ge, cnt, cnt_lo)
        it = it + BISECT_PER_CHECK
        over = jnp.where(cnt_lo > k_top, 1, 0)
        pending = any_lane(over)

        def tied_check():
            bmin, bmax = band_extent(lo, hi)
            return any_lane(jnp.where(bmin < bmax, over, 0))

        pending = lax.cond(jnp.logical_and(pending > 0, it >= FIRST_TIE_CHECK),
                           tied_check, lambda: pending)
        return it, lo, hi, cnt_lo, pending

    n_adm = (N_META + CHUNK * (qchunk + 1)).astype(F32)
    hi0 = mx + (mx - mn) + (jnp.abs(mx) * (2.0 ** -10) + 1e-30)
    _, lo, hi, cnt_lo, _ = lax.while_loop(
        bisect_cond, bisect_body,
        (jnp.int32(0), mn, hi0, n_adm, any_lane(jnp.where(n_adm > k_top, 1, 0))))
    pending = any_lane(jnp.where(cnt_lo > k_top, 1, 0))

    @pl.when(pending == 0)
    def _():
        def body(kc, _):
            s_scr[rows(kc), :] = jnp.where(s_scr[rows(kc), :] >= lo, 0.0, NEG)
            return 0
        lax.fori_loop(0, nkc, body, 0)
        s_scr[meta_rows, :] = jnp.where(s_scr[meta_rows, :] >= lo, 0.0, NEG)

    @pl.when(pending > 0)
    def _():
        need = k_top - count_ge(hi)

        def pick(sv, tri, before):
            band = jnp.where(sv >= lo, jnp.where(sv < hi, 1.0, 0.0), 0.0)
            rank = _dot(tri, band.astype(BF16)) + before
            take = jnp.where(rank <= need, band, 0.0)
            bias = jnp.where(sv >= hi, 0.0, jnp.where(take > 0.5, 0.0, NEG))
            return bias, before + jnp.sum(fold(band, jnp.sum), axis=0, keepdims=True)

        def tri(n):
            return jnp.where(lax.broadcasted_iota(jnp.int32, (n, n), 0)
                             >= lax.broadcasted_iota(jnp.int32, (n, n), 1), 1.0, 0.0).astype(BF16)

        bias_m, before = pick(s_scr[meta_rows, :], tri(N_META), jnp.zeros((1, qb), F32))
        s_scr[meta_rows, :] = bias_m
        tri_kc = tri(KC)

        def body(kc, before):
            bias, before = pick(s_scr[rows(kc), :], tri_kc, before)
            s_scr[rows(kc), :] = bias
            return before
        lax.fori_loop(0, nkc, body, before)

    def head(h):
        return slice(h * qb, (h + 1) * qb)

    for h in range(N_HEADS):
        qat_scr[:, head(h)] = (_dot(wuk_ref[h], qt_ref[h * HEAD_DIM:(h + 1) * HEAD_DIM, :])
                               * (ATTN_SCALE * LOG2E)).astype(BF16)
    m_scr[...] = jnp.full(m_scr.shape, 0.5 * NEG, F32)
    acc_scr[...] = jnp.zeros(acc_scr.shape, F32)

    def attend(c_rows, ct_cols, bias):
        r = c_rows.shape[0]
        lg_scr[0:r, :] = _dot(c_rows, qat_scr[...])
        for h in range(N_HEADS):
            lg = lg_scr[0:r, head(h)] + bias
            m_old = m_scr[:, head(h)]
            m_new = jnp.maximum(m_old, jnp.max(lg, axis=0, keepdims=True))
            p_scr[0:r, head(h)] = jnp.exp2(lg - m_new).astype(BF16)
            a_scr[:, head(h)] = jnp.exp2(m_old - m_new)
            m_scr[:, head(h)] = m_new
        acc_scr[...] = acc_scr[...] * a_scr[...] + _dot(ct_cols, p_scr[0:r, :])

    attend(cm_ref[...], cmt_ref[...], s_scr[meta_rows, :])

    def attend_body(kc, _):
        attend(c_ref[rows(kc), :], ct_ref[kc], s_scr[rows(kc), :])
        return 0
    lax.fori_loop(0, nkc, attend_body, 0)

    for h in range(N_HEADS):
        olat = acc_scr[0:KV_LATENT, head(h)] / acc_scr[KV_LATENT:KV_LATENT + 1, head(h)]
        ot_scr[h * HEAD_DIM:(h + 1) * HEAD_DIM, :] = _dot(wuvt_ref[h], olat.astype(BF16))
    o_ref[...] = ot_scr[...].T.astype(BF16)


def _mix_kernel(x_ref, g1_ref, wgate_ref, attn_ref, yp_ref, wba_ref, wbp_ref, wout_ref,
                g2_ref, wr_ref, wrhi_ref, br_ref, h1_ref, u2_ref, route_ref, merged_scr):
    x = x_ref[...]
    u = _rms(x, g1_ref[...]).astype(BF16)
    attn = attn_ref[...]
    yp = yp_ref[...]
    for cb in range(D_MODEL // MIX_COLS):
        ca = slice(cb * MIX_COLS, (cb + 1) * MIX_COLS)
        cp = slice(D_MODEL + cb * MIX_COLS, D_MODEL + (cb + 1) * MIX_COLS)
        g_attn = 1.0 / (1.0 + jnp.exp(-_dot(u, wgate_ref[:, ca])))
        g_pool = 1.0 / (1.0 + jnp.exp(-_dot(u, wgate_ref[:, cp])))
        merged_scr[:, ca] = (g_attn * _dot(attn, wba_ref[:, ca])
                             + g_pool * _dot(yp, wbp_ref[:, ca])).astype(BF16)
    h1 = x + _dot(merged_scr[...], wout_ref[...])
    h1_ref[...] = h1
    u2 = _rms(h1, g2_ref[...])
    u2_hi = u2.astype(BF16)
    u2_ref[...] = u2_hi

    u2_lo = (u2 - u2_hi.astype(F32)).astype(BF16)
    hi_both = _dot(u2_hi, wr_ref[...])
    lg = hi_both[:, :LANES] + hi_both[:, LANES:] + _dot(u2_lo, wrhi_ref[...]) + br_ref[...]
    lane = lax.broadcasted_iota(jnp.int32, lg.shape, 1)
    is_g = lane < N_GROUPS
    gl = jnp.where(is_g, lg, NEG)
    gmax = jnp.max(gl, axis=1, keepdims=True)
    gidx = jnp.min(jnp.where(gl == gmax, lane, LANES), axis=1, keepdims=True)
    p_g = 1.0 / jnp.sum(jnp.where(is_g, jnp.exp(gl - gmax), 0.0), axis=1, keepdims=True)
    e_lane = lane - N_GROUPS
    lane_grp = jnp.where(e_lane >= 0,
                         jnp.where(e_lane < N_EXPERTS, e_lane // EXPERTS_PER_GROUP, -1), -1)
    in_grp = lane_grp == gidx
    el = jnp.where(in_grp, lg, NEG)
    t1 = jnp.max(el, axis=1, keepdims=True)
    i1 = jnp.min(jnp.where(el == t1, lane, LANES), axis=1, keepdims=True)
    el2 = jnp.where(lane == i1, NEG, el)
    t2 = jnp.max(el2, axis=1, keepdims=True)
    i2 = jnp.min(jnp.where(el2 == t2, lane, LANES), axis=1, keepdims=True)
    r = jnp.exp(t2 - t1)
    p1 = 1.0 / (1.0 + r)
    p2 = r * p1
    e1 = (i1 - N_GROUPS).astype(F32)
    e2 = (i2 - N_GROUPS).astype(F32)
    route_ref[...] = jnp.where(lane == 0, e1, jnp.where(lane == 1, e2, jnp.where(
        lane == 2, p1 * p_g, jnp.where(lane == 3, p2 * p_g, 0.0))))


def _moe_kernel(start_ref, cnt_ref, tok_ref, slot_ref, wt_ref, u2_ref, wg_ref, wu_ref, wd_ref,
                y_ref, x2_scr, r2_scr, g2_scr, rt_scr):
    ts = u2_ref.shape[0]
    tmx = g2_scr.shape[0] // SUBLANES
    st = pl.program_id(0)
    e = pl.program_id(1)
    n_e = pl.num_programs(1)
    nsub = D_MODEL // LANES

    def vreg_rows(i):
        return pl.ds(pl.multiple_of(i * SUBLANES, SUBLANES), SUBLANES)

    @pl.when(e == 0)
    def _():
        for s in range(nsub):
            x2_scr[pl.ds(s, ts, stride=nsub), :] = (
                u2_ref[:, s * LANES:(s + 1) * LANES].astype(F32))

    start = start_ref[st * n_e + e]
    cnt = cnt_ref[st * n_e + e]

    def tile_body(i, _):
        base = start + i * tmx

        def gather_body(r8, _):
            for u in range(SUBLANES):
                r = r8 * SUBLANES + u
                g2_scr[vreg_rows(r), :] = x2_scr[vreg_rows(tok_ref[base + r]), :]
            return 0
        lax.fori_loop(0, tmx // SUBLANES, gather_body, 0)

        xg = jnp.concatenate([g2_scr[pl.ds(s, tmx, stride=nsub), :] for s in range(nsub)],
                             axis=1).astype(BF16)
        a = _dot(xg, wg_ref[...])
        b = _dot(xg, wu_ref[...])
        hg = (a * (1.0 / (1.0 + jnp.exp(-a))) * b).astype(BF16)
        yr = _dot(hg, wd_ref[...])
        for s in range(nsub):
            rt_scr[pl.ds(s, tmx, stride=nsub), :] = yr[:, s * LANES:(s + 1) * LANES]
        r2_scr[pl.ds(pl.multiple_of(base * SUBLANES, SUBLANES), tmx * SUBLANES), :] = rt_scr[...]
        return 0

    lax.fori_loop(0, (cnt + tmx - 1) // tmx, tile_body, 0)

    @pl.when(e == n_e - 1)
    def _():
        def combine_body(t8, _):
            for u in range(SUBLANES):
                t = t8 * SUBLANES + u
                x2_scr[vreg_rows(t), :] = (
                    wt_ref[2 * t] * r2_scr[vreg_rows(slot_ref[2 * t]), :]
                    + wt_ref[2 * t + 1] * r2_scr[vreg_rows(slot_ref[2 * t + 1]), :])
            return 0
        lax.fori_loop(0, ts // SUBLANES, combine_body, 0)
        for s in range(nsub):
            y_ref[:, s * LANES:(s + 1) * LANES] = x2_scr[pl.ds(s, ts, stride=nsub), :]


def _final_kernel(h1_ref, y_ref, gf_ref, o_ref):
    o_ref[...] = _rms(h1_ref[...] + y_ref[...], gf_ref[...])


def _const_spec(shape):
    nd = len(shape)
    return pl.BlockSpec(shape, lambda *_: (0,) * nd)


def _params(n_axes):
    return pltpu.CompilerParams(dimension_semantics=("arbitrary",) * n_axes,
                                vmem_limit_bytes=VMEM_LIMIT)


def kernel(x, meta_tokens, norm1_g, w_in, kv_norm_g, w_uk, w_uv, w_pool, pool_scale,
           w_branch_attn, w_branch_pool, w_out, norm2_g, w_group_router, b_group_router,
           w_expert_router, b_expert_router, w_expert_gate, w_expert_up, w_expert_down,
           final_norm_g):
    B, S, D = x.shape
    assert D == D_MODEL and S % QB == 0 and S % TM_PROJ == 0 and w_in.shape[0] == 1
    assert QB == KC and TM_PROJ % KC == 0 and QB % CHUNK == 0
    N = B * S
    k_top = min(TOPK_MAX, S // 4)
    xr = x.reshape(N, D)

    wi = w_in[0]
    w1 = jnp.concatenate(
        [wi[:, 0:640], wi[:, 640:896], wi[:, 936:1448], wi[:, 896:936],
         jnp.zeros((D, W1_WIDTH - 1448), F32)], axis=1).astype(BF16)
    wgate = wi[:, 1448:].astype(BF16)
    g1 = norm1_g[0].reshape(1, D)
    kvg = kv_norm_g[0].reshape(1, KV_LATENT)
    wpool = w_pool[0].astype(BF16)
    pscale = pool_scale[0].reshape(1, POOL_WIDTH)
    wuk = jnp.transpose(w_uk[0], (1, 0, 2)).astype(BF16)
    wuvt = jnp.transpose(w_uv[0], (1, 2, 0)).astype(BF16)
    wr = jnp.concatenate(
        [w_group_router[0], w_expert_router[0].reshape(D, N_EXPERTS),
         jnp.zeros((D, LANES - N_GROUPS - N_EXPERTS), F32)], axis=1)
    br = jnp.concatenate(
        [b_group_router[0], b_expert_router[0].reshape(N_EXPERTS),
         jnp.zeros((LANES - N_GROUPS - N_EXPERTS,), F32)]).reshape(1, LANES)

    c_m, tail_m, pv_m = pl.pallas_call(
        _meta_kernel,
        out_shape=(jax.ShapeDtypeStruct((N_META, KV_LATENT), F32),
                   jax.ShapeDtypeStruct((N_META, LANES), F32),
                   jax.ShapeDtypeStruct((N_META, POOL_WIDTH), F32)),
        name="meta",
    )(meta_tokens, g1, w1, kvg)
    cm = c_m.astype(BF16)
    cmt = jnp.concatenate([cm.T, jnp.ones((1, N_META), BF16),
                           jnp.zeros((CT_ROWS - KV_LATENT - 1, N_META), BF16)], axis=0)
    ikm = tail_m[:, :IDX_DIM].astype(BF16)

    tpb = S // TM_PROJ
    tok = lambda w: pl.BlockSpec((TM_PROJ, w), lambda b, i: (b * tpb + i, 0))
    tok_t = lambda w: pl.BlockSpec((w, TM_PROJ), lambda b, i: (0, b * tpb + i))
    qt, c, ct, iqt, ik, iwt, yp = pl.pallas_call(
        _proj_kernel,
        grid=(B, tpb),
        in_specs=[tok(D), _const_spec((1, D)), _const_spec((D, W1_WIDTH)),
                  _const_spec((1, KV_LATENT)), _const_spec((N_META, POOL_WIDTH)),
                  _const_spec((len(POOL_WINDOWS), POOL_GROUP, POOL_GROUP)),
                  _const_spec((1, POOL_WIDTH))],
        out_specs=[tok_t(ATTN_WIDTH), tok(KV_LATENT),
                   pl.BlockSpec((None, TM_PROJ // KC, CT_ROWS, KC), lambda b, i: (b, i, 0, 0)),
                   tok_t(IDX_HEADS * IDX_DIM), tok(IDX_DIM), tok_t(IDX_HEADS), tok(POOL_WIDTH)],
        out_shape=(jax.ShapeDtypeStruct((ATTN_WIDTH, N), BF16),
                   jax.ShapeDtypeStruct((N, KV_LATENT), BF16),
                   jax.ShapeDtypeStruct((B, S // KC, CT_ROWS, KC), BF16),
                   jax.ShapeDtypeStruct((IDX_HEADS * IDX_DIM, N), BF16),
                   jax.ShapeDtypeStruct((N, IDX_DIM), BF16),
                   jax.ShapeDtypeStruct((IDX_HEADS, N), F32),
                   jax.ShapeDtypeStruct((N, POOL_WIDTH), BF16)),
        scratch_shapes=[pltpu.VMEM((TM_PROJ + N_META, POOL_WIDTH), F32)],
        compiler_params=_params(2),
        name="proj",
    )(xr, g1, w1, kvg, pv_m, wpool, pscale)

    nqb = S // QB
    qcol = lambda w: pl.BlockSpec((w, QB), lambda b, j: (0, b * nqb + j))
    attn = pl.pallas_call(
        functools.partial(_attn_kernel, k_top=float(k_top)),
        grid=(B, nqb),
        in_specs=[pl.BlockSpec((S, IDX_DIM), lambda b, j: (b, 0)),
                  _const_spec((N_META, IDX_DIM)),
                  pl.BlockSpec((S, KV_LATENT), lambda b, j: (b, 0)),
                  _const_spec((N_META, KV_LATENT)),
                  pl.BlockSpec((None, S // KC, CT_ROWS, KC), lambda b, j: (b, 0, 0, 0)),
                  _const_spec((CT_ROWS, N_META)),
                  qcol(IDX_HEADS * IDX_DIM), qcol(IDX_HEADS), qcol(ATTN_WIDTH),
                  _const_spec((N_HEADS, KV_LATENT, HEAD_DIM)),
                  _const_spec((N_HEADS, HEAD_DIM, KV_LATENT))],
        out_specs=pl.BlockSpec((QB, ATTN_WIDTH), lambda b, j: (b * nqb + j, 0)),
        out_shape=jax.ShapeDtypeStruct((N, ATTN_WIDTH), BF16),
        scratch_shapes=[pltpu.VMEM((S + N_META, QB), F32),
                        pltpu.VMEM((KV_LATENT, N_HEADS * QB), BF16),
                        pltpu.VMEM((KC, N_HEADS * QB), F32),
                        pltpu.VMEM((KC, N_HEADS * QB), BF16),
                        pltpu.VMEM((1, N_HEADS * QB), F32), pltpu.VMEM((1, N_HEADS * QB), F32),
                        pltpu.VMEM((CT_ROWS, N_HEADS * QB), F32),
                        pltpu.VMEM((ATTN_WIDTH, QB), F32)],
        compiler_params=_params(2),
        name="attn",
    )(ik, ikm, c, cm, ct, cmt, iqt, iwt, qt, wuk, wuvt)

    wr_hi = wr.astype(BF16)
    wr_lo = (wr - wr_hi.astype(F32)).astype(BF16)
    tok1 = lambda w: pl.BlockSpec((TM_MIX, w), lambda i: (i, 0))
    h1, u2, route = pl.pallas_call(
        _mix_kernel,
        grid=(N // TM_MIX,),
        in_specs=[tok1(D), _const_spec((1, D)), _const_spec((D, 2 * D)), tok1(ATTN_WIDTH),
                  tok1(POOL_WIDTH), _const_spec((ATTN_WIDTH, D)), _const_spec((POOL_WIDTH, D)),
                  _const_spec((D, D)), _const_spec((1, D)), _const_spec((D, 2 * LANES)),
                  _const_spec((D, LANES)), _const_spec((1, LANES))],
        out_specs=[tok1(D), tok1(D), tok1(LANES)],
        out_shape=(jax.ShapeDtypeStruct((N, D), F32), jax.ShapeDtypeStruct((N, D), BF16),
                   jax.ShapeDtypeStruct((N, LANES), F32)),
        scratch_shapes=[pltpu.VMEM((TM_MIX, D), BF16)],
        compiler_params=_params(1),
        name="mix",
    )(xr, g1, wgate, attn, yp, w_branch_attn[0].astype(BF16), w_branch_pool[0].astype(BF16),
      w_out[0].astype(BF16), norm2_g[0].reshape(1, D),
      jnp.concatenate([wr_hi, wr_lo], axis=1), wr_hi, br)

    nst = N // TS_MOE
    n_asg = 2 * TS_MOE
    eid = route[:, 0:2].astype(jnp.int32).reshape(nst, n_asg)
    wts = route[:, 2:4].reshape(nst * n_asg)
    order = jnp.argsort(eid, axis=1, stable=True).astype(jnp.int32)
    slot = jnp.argsort(order, axis=1).astype(jnp.int32).reshape(nst * n_asg)
    tok_sorted = jnp.pad(order // 2, ((0, 0), (0, LIST_PAD))).reshape(nst * (n_asg + LIST_PAD))
    counts = jnp.sum(eid[:, :, None] == jnp.arange(N_EXPERTS, dtype=jnp.int32)[None, None, :],
                     axis=1, dtype=jnp.int32)
    starts = (jnp.cumsum(counts, axis=1) - counts).reshape(nst * N_EXPERTS)
    counts = counts.reshape(nst * N_EXPERTS)

    weg = w_expert_gate[0].reshape(N_EXPERTS, D, EXPERT_HIDDEN).astype(BF16)
    weu = w_expert_up[0].reshape(N_EXPERTS, D, EXPERT_HIDDEN).astype(BF16)
    wed = w_expert_down[0].reshape(N_EXPERTS, EXPERT_HIDDEN, D).astype(BF16)
    smem = lambda n: pl.BlockSpec((n,), lambda s, e, *_: (s,), memory_space=pltpu.SMEM)
    y = pl.pallas_call(
        _moe_kernel,
        grid_spec=pltpu.PrefetchScalarGridSpec(
            num_scalar_prefetch=2,
            grid=(nst, N_EXPERTS),
            in_specs=[smem(n_asg + LIST_PAD), smem(n_asg), smem(n_asg),
                      pl.BlockSpec((TS_MOE, D), lambda s, e, *_: (s, 0)),
                      pl.BlockSpec((None, D, EXPERT_HIDDEN), lambda s, e, *_: (e, 0, 0)),
                      pl.BlockSpec((None, D, EXPERT_HIDDEN), lambda s, e, *_: (e, 0, 0)),
                      pl.BlockSpec((None, EXPERT_HIDDEN, D), lambda s, e, *_: (e, 0, 0))],
            out_specs=pl.BlockSpec((TS_MOE, D), lambda s, e, *_: (s, 0)),
            scratch_shapes=[pltpu.VMEM((TS_MOE * SUBLANES, LANES), F32),
                            pltpu.VMEM(((n_asg + TMX_MOE) * SUBLANES, LANES), F32),
                            pltpu.VMEM((TMX_MOE * SUBLANES, LANES), F32),
                            pltpu.VMEM((TMX_MOE * SUBLANES, LANES), F32)]),
        out_shape=jax.ShapeDtypeStruct((N, D), F32),
        compiler_params=_params(2),
        name="moe",
    )(starts, counts, tok_sorted, slot, wts, u2, weg, weu, wed)

    ftok = pl.BlockSpec((TM_FINAL, D), lambda i: (i, 0))
    out = pl.pallas_call(
        _final_kernel,
        grid=(N // TM_FINAL,),
        in_specs=[ftok, ftok, _const_spec((1, D))],
        out_specs=ftok,
        out_shape=jax.ShapeDtypeStruct((N, D), F32),
        compiler_params=_params(1),
        name="final",
    )(h1, y, final_norm_g.reshape(1, D))
    return out.reshape(B, S, D)
```

```python
import functools

import jax
import jax.numpy as jnp
from jax import lax
from jax.experimental import pallas as pl
from jax.experimental.pallas import tpu as pltpu

F32 = jnp.float32
BF16 = jnp.bfloat16

D_MODEL = 1024
CHUNK = 64
N_META = 16
N_HEADS = 8
HEAD_DIM = 64
ATTN_WIDTH = N_HEADS * HEAD_DIM
KV_LATENT = 128
IDX_HEADS = 8
IDX_DIM = 32
TOPK_MAX = 256
ATTN_SCALE = HEAD_DIM ** -0.5
IDX_SCALE = (IDX_HEADS ** -0.5) * (IDX_DIM ** -0.5)
POOL_WINDOWS = (2, 4, 8, 16)
POOL_WIDTH = 512
POOL_GROUP = 128
N_GROUPS = 4
EXPERTS_PER_GROUP = 8
N_EXPERTS = N_GROUPS * EXPERTS_PER_GROUP
EXPERT_HIDDEN = 256
EPS = 1e-6

LANES = 128
SUBLANES = 8
W1_WIDTH = 1536
NEG = -1e30
POS = 1e30
VMEM_LIMIT = 56 * 1024 * 1024

TM_PROJ = 512
TM_MIX = 256
MIX_COLS = 256
QB = 256
KC = 256
MAX_BISECT = 40
BISECT_PER_CHECK = 4
FIRST_TIE_CHECK = 16
CT_ROWS = KV_LATENT + 16
LOG2E = 1.4426950408889634
TS_MOE = 2048
TMX_MOE = 160
LIST_PAD = 1024
TM_FINAL = 1024


def _rms(x, g):
    return x * lax.rsqrt(jnp.mean(x * x, axis=-1, keepdims=True) + EPS) * g


def _dot(a, b):
    return jnp.dot(a, b, preferred_element_type=F32)


def _meta_kernel(m_ref, g1_ref, w1_ref, kvg_ref, c_ref, tail_ref, pv_ref):
    u = _rms(m_ref[...], g1_ref[...]).astype(BF16)
    p = _dot(u, w1_ref[...])
    c_ref[...] = _rms(p[:, 512:640], kvg_ref[...])
    pv_ref[...] = p[:, 896:1408]
    tail_ref[...] = p[:, 1408:1536]


def _proj_kernel(x_ref, g1_ref, w1_ref, kvg_ref, pvmeta_ref, wpool_ref, pscale_ref,
                 qt_ref, c_ref, ct_ref, iqt_ref, ik_ref, iwt_ref, yp_ref, ext_ref):
    tm = x_ref.shape[0]
    u = _rms(x_ref[...], g1_ref[...]).astype(BF16)
    p = _dot(u, w1_ref[...])
    qt_ref[...] = p[:, 0:512].T.astype(BF16)
    c = _rms(p[:, 512:640], kvg_ref[...])
    c_ref[...] = c.astype(BF16)
    ct = jnp.concatenate([c.T, jnp.ones((1, tm), F32),
                          jnp.zeros((CT_ROWS - KV_LATENT - 1, tm), F32)], axis=0).astype(BF16)
    for k in range(tm // KC):
        ct_ref[k] = ct[:, k * KC:(k + 1) * KC]
    iqt_ref[...] = p[:, 640:896].T.astype(BF16)
    tail = p[:, 1408:1536]
    ik_ref[...] = tail[:, 0:IDX_DIM].astype(BF16)
    iwt_ref[...] = tail.T[IDX_DIM:IDX_DIM + IDX_HEADS, :] * IDX_SCALE
    pv = p[:, 896:1408]

    @pl.when(pl.program_id(1) == 0)
    def _():
        ext_ref[0:N_META, :] = pvmeta_ref[...]

    ext_ref[N_META:N_META + tm, :] = pv
    for g, w in enumerate(POOL_WINDOWS):
        cols = slice(g * POOL_GROUP, (g + 1) * POOL_GROUP)
        acc = pv[:, cols]
        for k in range(1, w):
            acc = acc + ext_ref[N_META - k:N_META - k + tm, cols]
        d = acc * (1.0 / w) - pv[:, cols]
        y = _dot(d.astype(BF16), wpool_ref[g]) * pscale_ref[:, cols]
        yp_ref[:, cols] = y.astype(BF16)
    ext_ref[0:N_META, :] = ext_ref[tm:tm + N_META, :]


def _attn_kernel(ik_ref, ikm_ref, c_ref, cm_ref, ct_ref, cmt_ref, iqt_ref, iwt_ref, qt_ref,
                 wuk_ref, wuvt_ref, o_ref,
                 s_scr, qat_scr, lg_scr, m_scr, acc_scr, ot_scr, *, k_top):
    n_real = c_ref.shape[0]
    qb = iqt_ref.shape[1]
    j = pl.program_id(1)
    nkc = j + 1
    qchunk = (j * qb + lax.broadcasted_iota(jnp.int32, (1, qb), 1)) // CHUNK
    meta_rows = pl.ds(n_real, N_META)

    def rows(kc):
        return pl.ds(pl.multiple_of(kc * KC, KC), KC)

    def fold(x, op):
        groups = x.shape[0] // SUBLANES
        chains = 4 if groups % 4 == 0 else 1
        x = x.reshape(groups // chains, chains, SUBLANES, qb)
        return op(op(x, axis=0), axis=0)

    iqt_heads = [iqt_ref[h * IDX_DIM:(h + 1) * IDX_DIM, :] for h in range(IDX_HEADS)]

    def scores(ik_rows):
        acc = None
        for h in range(IDX_HEADS):
            t = jnp.maximum(_dot(ik_rows, iqt_heads[h]), 0.0) * iwt_ref[h:h + 1, :]
            acc = t if acc is None else acc + t
        return acc

    sm = scores(ikm_ref[...])
    s_scr[meta_rows, :] = sm
    mn0 = jnp.min(sm, axis=0, keepdims=True)
    mx0 = jnp.max(sm, axis=0, keepdims=True)

    def score_body(kc, carry):
        mn, mx = carry
        sc = scores(ik_ref[rows(kc), :])
        s_scr[rows(kc), :] = sc
        return jnp.minimum(mn, fold(sc, jnp.min)), jnp.maximum(mx, fold(sc, jnp.max))

    mn8, mx8 = lax.fori_loop(0, j, score_body, (jnp.full((SUBLANES, qb), POS, F32),
                                                jnp.full((SUBLANES, qb), NEG, F32)))
    sc = scores(ik_ref[rows(j), :])
    adm = (j * KC + lax.broadcasted_iota(jnp.int32, (KC, 1), 0)) // CHUNK <= qchunk
    s_scr[rows(j), :] = jnp.where(adm, sc, NEG)
    mn8 = jnp.minimum(mn8, fold(jnp.where(adm, sc, POS), jnp.min))
    mx8 = jnp.maximum(mx8, fold(jnp.where(adm, sc, NEG), jnp.max))
    mn = jnp.minimum(mn0, jnp.min(mn8, axis=0, keepdims=True))
    mx = jnp.maximum(mx0, jnp.max(mx8, axis=0, keepdims=True))

    def count_ge(th):
        def body(kc, cnt):
            return cnt + fold(jnp.where(s_scr[rows(kc), :] >= th, 1.0, 0.0), jnp.sum)
        cnt = lax.fori_loop(0, nkc, body,
                            fold(jnp.where(s_scr[meta_rows, :] >= th, 1.0, 0.0), jnp.sum))
        return jnp.sum(cnt, axis=0, keepdims=True)

    def band_extent(lo, hi):
        def ext(sv):
            return (fold(jnp.where(sv >= lo, sv, POS), jnp.min),
                    fold(jnp.where(sv < hi, sv, NEG), jnp.max))

        def body(kc, carry):
            bmin, bmax = ext(s_scr[rows(kc), :])
            return jnp.minimum(carry[0], bmin), jnp.maximum(carry[1], bmax)
        bmin, bmax = lax.fori_loop(0, nkc, body, ext(s_scr[meta_rows, :]))
        return jnp.min(bmin, axis=0, keepdims=True), jnp.max(bmax, axis=0, keepdims=True)

    def any_lane(flags):
        return jnp.max(flags)

    def bisect_cond(carry):
        it, _, _, _, pending = carry
        return jnp.logical_and(it < MAX_BISECT, pending > 0)

    def bisect_body(carry):
        it, lo, hi, cnt_lo, _ = carry
        for _ in range(BISECT_PER_CHECK):
            mid = lo + (hi - lo) * 0.5
            cnt = count_ge(mid)
            ge = cnt >= k_top
            lo = jnp.where(ge, mid, lo)
            hi = jnp.where(ge, hi, mid)
            cnt_lo = jnp.where(ge, cnt, cnt_lo)
        it = it + BISECT_PER_CHECK
        over = jnp.where(cnt_lo > k_top, 1, 0)
        pending = any_lane(over)

        def tied_check():
            bmin, bmax = band_extent(lo, hi)
            return any_lane(jnp.where(bmin < bmax, over, 0))

        pending = lax.cond(jnp.logical_and(pending > 0, it >= FIRST_TIE_CHECK),
                           tied_check, lambda: pending)
        return it, lo, hi, cnt_lo, pending

    n_adm = (N_META + CHUNK * (qchunk + 1)).astype(F32)
    hi0 = mx + (mx - mn) + (jnp.abs(mx) * (2.0 ** -10) + 1e-30)
    _, lo, hi, cnt_lo, _ = lax.while_loop(
        bisect_cond, bisect_body,
        (jnp.int32(0), mn, hi0, n_adm, any_lane(jnp.where(n_adm > k_top, 1, 0))))
    pending = any_lane(jnp.where(cnt_lo > k_top, 1, 0))

    @pl.when(pending == 0)
    def _():
        def body(kc, _):
            s_scr[rows(kc), :] = jnp.where(s_scr[rows(kc), :] >= lo, 0.0, NEG)
            return 0
        lax.fori_loop(0, nkc, body, 0)
        s_scr[meta_rows, :] = jnp.where(s_scr[meta_rows, :] >= lo, 0.0, NEG)

    @pl.when(pending > 0)
    def _():
        need = k_top - count_ge(hi)

        def pick(sv, tri, before):
            band = jnp.where(sv >= lo, jnp.where(sv < hi, 1.0, 0.0), 0.0)
            rank = _dot(tri, band.astype(BF16)) + before
            take = jnp.where(rank <= need, band, 0.0)
            bias = jnp.where(sv >= hi, 0.0, jnp.where(take > 0.5, 0.0, NEG))
            return bias, before + jnp.sum(fold(band, jnp.sum), axis=0, keepdims=True)

        def tri(n):
            return jnp.where(lax.broadcasted_iota(jnp.int32, (n, n), 0)
                             >= lax.broadcasted_iota(jnp.int32, (n, n), 1), 1.0, 0.0).astype(BF16)

        bias_m, before = pick(s_scr[meta_rows, :], tri(N_META), jnp.zeros((1, qb), F32))
        s_scr[meta_rows, :] = bias_m
        tri_kc = tri(KC)

        def body(kc, before):
            bias, before = pick(s_scr[rows(kc), :], tri_kc, before)
            s_scr[rows(kc), :] = bias
            return before
        lax.fori_loop(0, nkc, body, before)

    def head(h):
        return slice(h * qb, (h + 1) * qb)

    for h in range(N_HEADS):
        qat_scr[:, head(h)] = (_dot(wuk_ref[h], qt_ref[h * HEAD_DIM:(h + 1) * HEAD_DIM, :])
                               * (ATTN_SCALE * LOG2E)).astype(BF16)
    m_scr[...] = jnp.full(m_scr.shape, 0.5 * NEG, F32)
    acc_scr[...] = jnp.zeros(acc_scr.shape, F32)

    def attend(c_rows, ct_cols, bias):
        r = c_rows.shape[0]
        lg_scr[0:r, :] = _dot(c_rows, qat_scr[...])
        for h in range(N_HEADS):
            lg = lg_scr[0:r, head(h)] + bias
            m_old = m_scr[:, head(h)]
            m_new = jnp.maximum(m_old, jnp.max(lg, axis=0, keepdims=True))
            p = jnp.exp2(lg - m_new).astype(BF16)
            m_scr[:, head(h)] = m_new
            acc_scr[:, head(h)] = (acc_scr[:, head(h)] * jnp.exp2(m_old - m_new)
                                   + _dot(ct_cols, p))

    attend(cm_ref[...], cmt_ref[...], s_scr[meta_rows, :])

    def attend_body(kc, _):
        attend(c_ref[rows(kc), :], ct_ref[kc], s_scr[rows(kc), :])
        return 0
    lax.fori_loop(0, nkc, attend_body, 0)

    for h in range(N_HEADS):
        olat = acc_scr[0:KV_LATENT, head(h)] / acc_scr[KV_LATENT:KV_LATENT + 1, head(h)]
        ot_scr[h * HEAD_DIM:(h + 1) * HEAD_DIM, :] = _dot(wuvt_ref[h], olat.astype(BF16))
    o_ref[...] = ot_scr[...].T.astype(BF16)


def _mix_kernel(x_ref, g1_ref, wgate_ref, attn_ref, yp_ref, wba_ref, wbp_ref, wout_ref,
                g2_ref, wr_ref, wrhi_ref, br_ref, h1_ref, u2_ref, route_ref, merged_scr):
    x = x_ref[...]
    u = _rms(x, g1_ref[...]).astype(BF16)
    attn = attn_ref[...]
    yp = yp_ref[...]
    for cb in range(D_MODEL // MIX_COLS):
        ca = slice(cb * MIX_COLS, (cb + 1) * MIX_COLS)
        cp = slice(D_MODEL + cb * MIX_COLS, D_MODEL + (cb + 1) * MIX_COLS)
        g_attn = 1.0 / (1.0 + jnp.exp(-_dot(u, wgate_ref[:, ca])))
        g_pool = 1.0 / (1.0 + jnp.exp(-_dot(u, wgate_ref[:, cp])))
        merged_scr[:, ca] = (g_attn * _dot(attn, wba_ref[:, ca])
                             + g_pool * _dot(yp, wbp_ref[:, ca])).astype(BF16)
    h1 = x + _dot(merged_scr[...], wout_ref[...])
    h1_ref[...] = h1
    u2 = _rms(h1, g2_ref[...])
    u2_ref[...] = u2
    u2_hi = u2.astype(BF16)

    u2_lo = (u2 - u2_hi.astype(F32)).astype(BF16)
    hi_both = _dot(u2_hi, wr_ref[...])
    lg = hi_both[:, :LANES] + hi_both[:, LANES:] + _dot(u2_lo, wrhi_ref[...]) + br_ref[...]
    lane = lax.broadcasted_iota(jnp.int32, lg.shape, 1)
    is_g = lane < N_GROUPS
    gl = jnp.where(is_g, lg, NEG)
    gmax = jnp.max(gl, axis=1, keepdims=True)
    gidx = jnp.min(jnp.where(gl == gmax, lane, LANES), axis=1, keepdims=True)
    p_g = 1.0 / jnp.sum(jnp.where(is_g, jnp.exp(gl - gmax), 0.0), axis=1, keepdims=True)
    e_lane = lane - N_GROUPS
    lane_grp = jnp.where(e_lane >= 0,
                         jnp.where(e_lane < N_EXPERTS, e_lane // EXPERTS_PER_GROUP, -1), -1)
    in_grp = lane_grp == gidx
    el = jnp.where(in_grp, lg, NEG)
    t1 = jnp.max(el, axis=1, keepdims=True)
    i1 = jnp.min(jnp.where(el == t1, lane, LANES), axis=1, keepdims=True)
    el2 = jnp.where(lane == i1, NEG, el)
    t2 = jnp.max(el2, axis=1, keepdims=True)
    i2 = jnp.min(jnp.where(el2 == t2, lane, LANES), axis=1, keepdims=True)
    r = jnp.exp(t2 - t1)
    p1 = 1.0 / (1.0 + r)
    p2 = r * p1
    e1 = (i1 - N_GROUPS).astype(F32)
    e2 = (i2 - N_GROUPS).astype(F32)
    route_ref[...] = jnp.where(lane == 0, e1, jnp.where(lane == 1, e2, jnp.where(
        lane == 2, p1 * p_g, jnp.where(lane == 3, p2 * p_g, 0.0))))


def _moe_kernel(start_ref, cnt_ref, tok_ref, slot_ref, wt_ref, x2_ref, wg_ref, wu_ref, wd_ref,
                y2_ref, r2_scr, g2_scr, rt_scr):
    ts = x2_ref.shape[0] // SUBLANES
    tmx = g2_scr.shape[0] // SUBLANES
    st = pl.program_id(0)
    e = pl.program_id(1)
    n_e = pl.num_programs(1)
    nsub = D_MODEL // LANES

    def vreg_rows(i):
        return pl.ds(pl.multiple_of(i * SUBLANES, SUBLANES), SUBLANES)

    start = start_ref[st * n_e + e]
    cnt = cnt_ref[st * n_e + e]

    def tile_body(i, _):
        base = start + i * tmx

        def gather_body(r8, _):
            for u in range(SUBLANES):
                r = r8 * SUBLANES + u
                g2_scr[vreg_rows(r), :] = x2_ref[vreg_rows(tok_ref[base + r]), :]
            return 0
        lax.fori_loop(0, tmx // SUBLANES, gather_body, 0)

        xg = jnp.concatenate([g2_scr[pl.ds(s, tmx, stride=nsub), :] for s in range(nsub)],
                             axis=1).astype(BF16)
        a = _dot(xg, wg_ref[...])
        b = _dot(xg, wu_ref[...])
        hg = (a * (1.0 / (1.0 + jnp.exp(-a))) * b).astype(BF16)
        yr = _dot(hg, wd_ref[...])
        for s in range(nsub):
            rt_scr[pl.ds(s, tmx, stride=nsub), :] = yr[:, s * LANES:(s + 1) * LANES]
        r2_scr[pl.ds(pl.multiple_of(base * SUBLANES, SUBLANES), tmx * SUBLANES), :] = rt_scr[...]
        return 0

    lax.fori_loop(0, (cnt + tmx - 1) // tmx, tile_body, 0)

    @pl.when(e == n_e - 1)
    def _():
        def combine_body(t8, _):
            for u in range(SUBLANES):
                t = t8 * SUBLANES + u
                y2_ref[vreg_rows(t), :] = (
                    wt_ref[2 * t] * r2_scr[vreg_rows(slot_ref[2 * t]), :]
                    + wt_ref[2 * t + 1] * r2_scr[vreg_rows(slot_ref[2 * t + 1]), :])
            return 0
        lax.fori_loop(0, ts // SUBLANES, combine_body, 0)


def _final_kernel(h1_ref, y_ref, gf_ref, o_ref):
    o_ref[...] = _rms(h1_ref[...] + y_ref[...], gf_ref[...])


def _const_spec(shape):
    nd = len(shape)
    return pl.BlockSpec(shape, lambda *_: (0,) * nd)


def _params(n_axes):
    return pltpu.CompilerParams(dimension_semantics=("arbitrary",) * n_axes,
                                vmem_limit_bytes=VMEM_LIMIT)


def kernel(x, meta_tokens, norm1_g, w_in, kv_norm_g, w_uk, w_uv, w_pool, pool_scale,
           w_branch_attn, w_branch_pool, w_out, norm2_g, w_group_router, b_group_router,
           w_expert_router, b_expert_router, w_expert_gate, w_expert_up, w_expert_down,
           final_norm_g):
    B, S, D = x.shape
    assert D == D_MODEL and S % QB == 0 and S % TM_PROJ == 0 and w_in.shape[0] == 1
    assert QB == KC and TM_PROJ % KC == 0 and QB % CHUNK == 0
    N = B * S
    k_top = min(TOPK_MAX, S // 4)
    xr = x.reshape(N, D)

    wi = w_in[0]
    w1 = jnp.concatenate(
        [wi[:, 0:640], wi[:, 640:896], wi[:, 936:1448], wi[:, 896:936],
         jnp.zeros((D, W1_WIDTH - 1448), F32)], axis=1).astype(BF16)
    wgate = wi[:, 1448:].astype(BF16)
    g1 = norm1_g[0].reshape(1, D)
    kvg = kv_norm_g[0].reshape(1, KV_LATENT)
    wpool = w_pool[0].astype(BF16)
    pscale = pool_scale[0].reshape(1, POOL_WIDTH)
    wuk = jnp.transpose(w_uk[0], (1, 0, 2)).astype(BF16)
    wuvt = jnp.transpose(w_uv[0], (1, 2, 0)).astype(BF16)
    wr = jnp.concatenate(
        [w_group_router[0], w_expert_router[0].reshape(D, N_EXPERTS),
         jnp.zeros((D, LANES - N_GROUPS - N_EXPERTS), F32)], axis=1)
    br = jnp.concatenate(
        [b_group_router[0], b_expert_router[0].reshape(N_EXPERTS),
         jnp.zeros((LANES - N_GROUPS - N_EXPERTS,), F32)]).reshape(1, LANES)

    c_m, tail_m, pv_m = pl.pallas_call(
        _meta_kernel,
        out_shape=(jax.ShapeDtypeStruct((N_META, KV_LATENT), F32),
                   jax.ShapeDtypeStruct((N_META, LANES), F32),
                   jax.ShapeDtypeStruct((N_META, POOL_WIDTH), F32)),
        name="meta",
    )(meta_tokens, g1, w1, kvg)
    cm = c_m.astype(BF16)
    cmt = jnp.concatenate([cm.T, jnp.ones((1, N_META), BF16),
                           jnp.zeros((CT_ROWS - KV_LATENT - 1, N_META), BF16)], axis=0)
    ikm = tail_m[:, :IDX_DIM].astype(BF16)

    tpb = S // TM_PROJ
    tok = lambda w: pl.BlockSpec((TM_PROJ, w), lambda b, i: (b * tpb + i, 0))
    tok_t = lambda w: pl.BlockSpec((w, TM_PROJ), lambda b, i: (0, b * tpb + i))
    qt, c, ct, iqt, ik, iwt, yp = pl.pallas_call(
        _proj_kernel,
        grid=(B, tpb),
        in_specs=[tok(D), _const_spec((1, D)), _const_spec((D, W1_WIDTH)),
                  _const_spec((1, KV_LATENT)), _const_spec((N_META, POOL_WIDTH)),
                  _const_spec((len(POOL_WINDOWS), POOL_GROUP, POOL_GROUP)),
                  _const_spec((1, POOL_WIDTH))],
        out_specs=[tok_t(ATTN_WIDTH), tok(KV_LATENT),
                   pl.BlockSpec((None, TM_PROJ // KC, CT_ROWS, KC), lambda b, i: (b, i, 0, 0)),
                   tok_t(IDX_HEADS * IDX_DIM), tok(IDX_DIM), tok_t(IDX_HEADS), tok(POOL_WIDTH)],
        out_shape=(jax.ShapeDtypeStruct((ATTN_WIDTH, N), BF16),
                   jax.ShapeDtypeStruct((N, KV_LATENT), BF16),
                   jax.ShapeDtypeStruct((B, S // KC, CT_ROWS, KC), BF16),
                   jax.ShapeDtypeStruct((IDX_HEADS * IDX_DIM, N), BF16),
                   jax.ShapeDtypeStruct((N, IDX_DIM), BF16),
                   jax.ShapeDtypeStruct((IDX_HEADS, N), F32),
                   jax.ShapeDtypeStruct((N, POOL_WIDTH), BF16)),
        scratch_shapes=[pltpu.VMEM((TM_PROJ + N_META, POOL_WIDTH), F32)],
        compiler_params=_params(2),
        name="proj",
    )(xr, g1, w1, kvg, pv_m, wpool, pscale)

    nqb = S // QB
    qcol = lambda w: pl.BlockSpec((w, QB), lambda b, j: (0, b * nqb + j))
    attn = pl.pallas_call(
        functools.partial(_attn_kernel, k_top=float(k_top)),
        grid=(B, nqb),
        in_specs=[pl.BlockSpec((S, IDX_DIM), lambda b, j: (b, 0)),
                  _const_spec((N_META, IDX_DIM)),
                  pl.BlockSpec((S, KV_LATENT), lambda b, j: (b, 0)),
                  _const_spec((N_META, KV_LATENT)),
                  pl.BlockSpec((None, S // KC, CT_ROWS, KC), lambda b, j: (b, 0, 0, 0)),
                  _const_spec((CT_ROWS, N_META)),
                  qcol(IDX_HEADS * IDX_DIM), qcol(IDX_HEADS), qcol(ATTN_WIDTH),
                  _const_spec((N_HEADS, KV_LATENT, HEAD_DIM)),
                  _const_spec((N_HEADS, HEAD_DIM, KV_LATENT))],
        out_specs=pl.BlockSpec((QB, ATTN_WIDTH), lambda b, j: (b * nqb + j, 0)),
        out_shape=jax.ShapeDtypeStruct((N, ATTN_WIDTH), BF16),
        scratch_shapes=[pltpu.VMEM((S + N_META, QB), F32),
                        pltpu.VMEM((KV_LATENT, N_HEADS * QB), BF16),
                        pltpu.VMEM((KC, N_HEADS * QB), F32),
                        pltpu.VMEM((1, N_HEADS * QB), F32),
                        pltpu.VMEM((CT_ROWS, N_HEADS * QB), F32),
                        pltpu.VMEM((ATTN_WIDTH, QB), F32)],
        compiler_params=_params(2),
        name="attn",
    )(ik, ikm, c, cm, ct, cmt, iqt, iwt, qt, wuk, wuvt)

    wr_hi = wr.astype(BF16)
    wr_lo = (wr - wr_hi.astype(F32)).astype(BF16)
    tok1 = lambda w: pl.BlockSpec((TM_MIX, w), lambda i: (i, 0))
    h1, u2, route = pl.pallas_call(
        _mix_kernel,
        grid=(N // TM_MIX,),
        in_specs=[tok1(D), _const_spec((1, D)), _const_spec((D, 2 * D)), tok1(ATTN_WIDTH),
                  tok1(POOL_WIDTH), _const_spec((ATTN_WIDTH, D)), _const_spec((POOL_WIDTH, D)),
                  _const_spec((D, D)), _const_spec((1, D)), _const_spec((D, 2 * LANES)),
                  _const_spec((D, LANES)), _const_spec((1, LANES))],
        out_specs=[tok1(D), tok1(D), tok1(LANES)],
        out_shape=(jax.ShapeDtypeStruct((N, D), F32), jax.ShapeDtypeStruct((N, D), F32),
                   jax.ShapeDtypeStruct((N, LANES), F32)),
        scratch_shapes=[pltpu.VMEM((TM_MIX, D), BF16)],
        compiler_params=_params(1),
        name="mix",
    )(xr, g1, wgate, attn, yp, w_branch_attn[0].astype(BF16), w_branch_pool[0].astype(BF16),
      w_out[0].astype(BF16), norm2_g[0].reshape(1, D),
      jnp.concatenate([wr_hi, wr_lo], axis=1), wr_hi, br)

    nst = N // TS_MOE
    n_asg = 2 * TS_MOE
    eid = route[:, 0:2].astype(jnp.int32).reshape(nst, n_asg)
    wts = route[:, 2:4].reshape(nst * n_asg)
    order = jnp.argsort(eid, axis=1, stable=True).astype(jnp.int32)
    slot = jnp.argsort(order, axis=1).astype(jnp.int32).reshape(nst * n_asg)
    tok_sorted = jnp.pad(order // 2, ((0, 0), (0, LIST_PAD))).reshape(nst * (n_asg + LIST_PAD))
    counts = jnp.sum(eid[:, :, None] == jnp.arange(N_EXPERTS, dtype=jnp.int32)[None, None, :],
                     axis=1, dtype=jnp.int32)
    starts = (jnp.cumsum(counts, axis=1) - counts).reshape(nst * N_EXPERTS)
    counts = counts.reshape(nst * N_EXPERTS)

    weg = w_expert_gate[0].reshape(N_EXPERTS, D, EXPERT_HIDDEN).astype(BF16)
    weu = w_expert_up[0].reshape(N_EXPERTS, D, EXPERT_HIDDEN).astype(BF16)
    wed = w_expert_down[0].reshape(N_EXPERTS, EXPERT_HIDDEN, D).astype(BF16)
    smem = lambda n: pl.BlockSpec((n,), lambda s, e, *_: (s,), memory_space=pltpu.SMEM)
    y = pl.pallas_call(
        _moe_kernel,
        grid_spec=pltpu.PrefetchScalarGridSpec(
            num_scalar_prefetch=2,
            grid=(nst, N_EXPERTS),
            in_specs=[smem(n_asg + LIST_PAD), smem(n_asg), smem(n_asg),
                      pl.BlockSpec((TS_MOE * SUBLANES, LANES), lambda s, e, *_: (s, 0)),
                      pl.BlockSpec((None, D, EXPERT_HIDDEN), lambda s, e, *_: (e, 0, 0)),
                      pl.BlockSpec((None, D, EXPERT_HIDDEN), lambda s, e, *_: (e, 0, 0)),
                      pl.BlockSpec((None, EXPERT_HIDDEN, D), lambda s, e, *_: (e, 0, 0))],
            out_specs=pl.BlockSpec((TS_MOE * SUBLANES, LANES), lambda s, e, *_: (s, 0)),
            scratch_shapes=[pltpu.VMEM(((n_asg + TMX_MOE) * SUBLANES, LANES), F32),
                            pltpu.VMEM((TMX_MOE * SUBLANES, LANES), F32),
                            pltpu.VMEM((TMX_MOE * SUBLANES, LANES), F32)]),
        out_shape=jax.ShapeDtypeStruct((N * SUBLANES, LANES), F32),
        compiler_params=_params(2),
        name="moe",
    )(starts, counts, tok_sorted, slot, wts, u2.reshape(N * SUBLANES, LANES), weg, weu, wed)
    y = y.reshape(N, D)

    ftok = pl.BlockSpec((TM_FINAL, D), lambda i: (i, 0))
    out = pl.pallas_call(
        _final_kernel,
        grid=(N // TM_FINAL,),
        in_specs=[ftok, ftok, _const_spec((1, D))],
        out_specs=ftok,
        out_shape=jax.ShapeDtypeStruct((N, D), F32),
        compiler_params=_params(1),
        name="final",
    )(h1, y, final_norm_g.reshape(1, D))
    return out.reshape(B, S, D)
```

```python
import functools

import jax
import jax.numpy as jnp
from jax import lax
from jax.experimental import pallas as pl
from jax.experimental.pallas import tpu as pltpu

F32 = jnp.float32
BF16 = jnp.bfloat16

D_MODEL = 1024
CHUNK = 64
N_META = 16
N_HEADS = 8
HEAD_DIM = 64
ATTN_WIDTH = N_HEADS * HEAD_DIM
KV_LATENT = 128
IDX_HEADS = 8
IDX_DIM = 32
TOPK_MAX = 256
ATTN_SCALE = HEAD_DIM ** -0.5
IDX_SCALE = (IDX_HEADS ** -0.5) * (IDX_DIM ** -0.5)
POOL_WINDOWS = (2, 4, 8, 16)
POOL_WIDTH = 512
POOL_GROUP = 128
N_GROUPS = 4
EXPERTS_PER_GROUP = 8
N_EXPERTS = N_GROUPS * EXPERTS_PER_GROUP
EXPERT_HIDDEN = 256
EPS = 1e-6

LANES = 128
SUBLANES = 8
W1_WIDTH = 1536
NEG = -1e30
POS = 1e30
VMEM_LIMIT = 56 * 1024 * 1024

TM_PROJ = 512
TM_MIX = 256
MIX_COLS = 256
QB = 256
KC = 256
MAX_BISECT = 40
BISECT_PER_CHECK = 4
FIRST_TIE_CHECK = 16
CT_ROWS = KV_LATENT + 16
LOG2E = 1.4426950408889634
TS_MOE = 2048
TMX_MOE = 160
LIST_PAD = 1024
TM_FINAL = 1024


def _rms(x, g):
    return x * lax.rsqrt(jnp.mean(x * x, axis=-1, keepdims=True) + EPS) * g


def _dot(a, b):
    return jnp.dot(a, b, preferred_element_type=F32)


def _meta_kernel(m_ref, g1_ref, w1_ref, kvg_ref, c_ref, tail_ref, pv_ref):
    u = _rms(m_ref[...], g1_ref[...]).astype(BF16)
    p = _dot(u, w1_ref[...])
    c_ref[...] = _rms(p[:, 512:640], kvg_ref[...])
    pv_ref[...] = p[:, 896:1408]
    tail_ref[...] = p[:, 1408:1536]


def _proj_kernel(x_ref, g1_ref, w1_ref, kvg_ref, pvmeta_ref, wpool_ref, pscale_ref,
                 qt_ref, c_ref, ct_ref, iqt_ref, ik_ref, iwt_ref, yp_ref, ext_ref):
    tm = x_ref.shape[0]
    u = _rms(x_ref[...], g1_ref[...]).astype(BF16)
    p = _dot(u, w1_ref[...])
    qt_ref[...] = p[:, 0:512].T.astype(BF16)
    c = _rms(p[:, 512:640], kvg_ref[...])
    c_ref[...] = c.astype(BF16)
    ct = jnp.concatenate([c.T, jnp.ones((1, tm), F32),
                          jnp.zeros((CT_ROWS - KV_LATENT - 1, tm), F32)], axis=0).astype(BF16)
    for k in range(tm // KC):
        ct_ref[k] = ct[:, k * KC:(k + 1) * KC]
    iqt_ref[...] = p[:, 640:896].T.astype(BF16)
    tail = p[:, 1408:1536]
    ik_ref[...] = tail[:, 0:IDX_DIM].astype(BF16)
    iwt_ref[...] = tail.T[IDX_DIM:IDX_DIM + IDX_HEADS, :] * IDX_SCALE
    pv = p[:, 896:1408]

    @pl.when(pl.program_id(1) == 0)
    def _():
        ext_ref[0:N_META, :] = pvmeta_ref[...]

    ext_ref[N_META:N_META + tm, :] = pv
    for g, w in enumerate(POOL_WINDOWS):
        cols = slice(g * POOL_GROUP, (g + 1) * POOL_GROUP)
        acc = pv[:, cols]
        for k in range(1, w):
            acc = acc + ext_ref[N_META - k:N_META - k + tm, cols]
        d = acc * (1.0 / w) - pv[:, cols]
        y = _dot(d.astype(BF16), wpool_ref[g]) * pscale_ref[:, cols]
        yp_ref[:, cols] = y.astype(BF16)
    ext_ref[0:N_META, :] = ext_ref[tm:tm + N_META, :]


def _attn_kernel(ik_ref, ikm_ref, c_ref, cm_ref, ct_ref, cmt_ref, iqt_ref, iwt_ref, qt_ref,
                 wuk_ref, wuvt_ref, o_ref,
                 s_scr, qat_scr, lg_scr, m_scr, acc_scr, ot_scr, *, k_top):
    n_real = c_ref.shape[0]
    qb = iqt_ref.shape[1]
    j = pl.program_id(1)
    nkc = j + 1
    qchunk = (j * qb + lax.broadcasted_iota(jnp.int32, (1, qb), 1)) // CHUNK
    meta_rows = pl.ds(n_real, N_META)

    def rows(kc):
        return pl.ds(pl.multiple_of(kc * KC, KC), KC)

    def fold(x, op):
        groups = x.shape[0] // SUBLANES
        chains = 4 if groups % 4 == 0 else 1
        x = x.reshape(groups // chains, chains, SUBLANES, qb)
        return op(op(x, axis=0), axis=0)

    iqt_heads = [iqt_ref[h * IDX_DIM:(h + 1) * IDX_DIM, :] for h in range(IDX_HEADS)]

    def scores(ik_rows):
        acc = None
        for h in range(IDX_HEADS):
            t = jnp.maximum(_dot(ik_rows, iqt_heads[h]), 0.0) * iwt_ref[h:h + 1, :]
            acc = t if acc is None else acc + t
        return acc

    sm = scores(ikm_ref[...])
    s_scr[meta_rows, :] = sm
    mn0 = jnp.min(sm, axis=0, keepdims=True)
    mx0 = jnp.max(sm, axis=0, keepdims=True)

    def score_body(kc, carry):
        mn, mx = carry
        sc = scores(ik_ref[rows(kc), :])
        s_scr[rows(kc), :] = sc
        return jnp.minimum(mn, fold(sc, jnp.min)), jnp.maximum(mx, fold(sc, jnp.max))

    mn8, mx8 = lax.fori_loop(0, j, score_body, (jnp.full((SUBLANES, qb), POS, F32),
                                                jnp.full((SUBLANES, qb), NEG, F32)))
    sc = scores(ik_ref[rows(j), :])
    adm = (j * KC + lax.broadcasted_iota(jnp.int32, (KC, 1), 0)) // CHUNK <= qchunk
    s_scr[rows(j), :] = jnp.where(adm, sc, NEG)
    mn8 = jnp.minimum(mn8, fold(jnp.where(adm, sc, POS), jnp.min))
    mx8 = jnp.maximum(mx8, fold(jnp.where(adm, sc, NEG), jnp.max))
    mn = jnp.minimum(mn0, jnp.min(mn8, axis=0, keepdims=True))
    mx = jnp.maximum(mx0, jnp.max(mx8, axis=0, keepdims=True))

    def count_ge(th):
        def body(kc, cnt):
            return cnt + fold(jnp.where(s_scr[rows(kc), :] >= th, 1.0, 0.0), jnp.sum)
        cnt = lax.fori_loop(0, nkc, body,
                            fold(jnp.where(s_scr[meta_rows, :] >= th, 1.0, 0.0), jnp.sum))
        return jnp.sum(cnt, axis=0, keepdims=True)

    def band_extent(lo, hi):
        def ext(sv):
            return (fold(jnp.where(sv >= lo, sv, POS), jnp.min),
                    fold(jnp.where(sv < hi, sv, NEG), jnp.max))

        def body(kc, carry):
            bmin, bmax = ext(s_scr[rows(kc), :])
            return jnp.minimum(carry[0], bmin), jnp.maximum(carry[1], bmax)
        bmin, bmax = lax.fori_loop(0, nkc, body, ext(s_scr[meta_rows, :]))
        return jnp.min(bmin, axis=0, keepdims=True), jnp.max(bmax, axis=0, keepdims=True)

    def any_lane(flags):
        return jnp.max(flags)

    def bisect_cond(carry):
        it, _, _, _, pending = carry
        return jnp.logical_and(it < MAX_BISECT, pending > 0)

    def bisect_body(carry):
        it, lo, hi, cnt_lo, _ = carry
        for _ in range(BISECT_PER_CHECK):
            mid = lo + (hi - lo) * 0.5
            cnt = count_ge(mid)
            ge = cnt >= k_top
            lo = jnp.where(ge, mid, lo)
            hi = jnp.where(ge, hi, mid)
            cnt_lo = jnp.where(ge, cnt, cnt_lo)
        it = it + BISECT_PER_CHECK
        over = jnp.where(cnt_lo > k_top, 1, 0)
        pending = any_lane(over)

        def tied_check():
            bmin, bmax = band_extent(lo, hi)
            return any_lane(jnp.where(bmin < bmax, over, 0))

        pending = lax.cond(jnp.logical_and(pending > 0, it >= FIRST_TIE_CHECK),
                           tied_check, lambda: pending)
        return it, lo, hi, cnt_lo, pending

    n_adm = (N_META + CHUNK * (qchunk + 1)).astype(F32)
    hi0 = mx + (mx - mn) + (jnp.abs(mx) * (2.0 ** -10) + 1e-30)
    _, lo, hi, cnt_lo, _ = lax.while_loop(
        bisect_cond, bisect_body,
        (jnp.int32(0), mn, hi0, n_adm, any_lane(jnp.where(n_adm > k_top, 1, 0))))
    pending = any_lane(jnp.where(cnt_lo > k_top, 1, 0))

    @pl.when(pending == 0)
    def _():
        def body(kc, _):
            s_scr[rows(kc), :] = jnp.where(s_scr[rows(kc), :] >= lo, 0.0, NEG)
            return 0
        lax.fori_loop(0, nkc, body, 0)
        s_scr[meta_rows, :] = jnp.where(s_scr[meta_rows, :] >= lo, 0.0, NEG)

    @pl.when(pending > 0)
    def _():
        need = k_top - count_ge(hi)

        def pick(sv, tri, before):
            band = jnp.where(sv >= lo, jnp.where(sv < hi, 1.0, 0.0), 0.0)
            rank = _dot(tri, band.astype(BF16)) + before
            take = jnp.where(rank <= need, band, 0.0)
            bias = jnp.where(sv >= hi, 0.0, jnp.where(take > 0.5, 0.0, NEG))
            return bias, before + jnp.sum(fold(band, jnp.sum), axis=0, keepdims=True)

        def tri(n):
            return jnp.where(lax.broadcasted_iota(jnp.int32, (n, n), 0)
                             >= lax.broadcasted_iota(jnp.int32, (n, n), 1), 1.0, 0.0).astype(BF16)

        bias_m, before = pick(s_scr[meta_rows, :], tri(N_META), jnp.zeros((1, qb), F32))
        s_scr[meta_rows, :] = bias_m
        tri_kc = tri(KC)

        def body(kc, before):
            bias, before = pick(s_scr[rows(kc), :], tri_kc, before)
            s_scr[rows(kc), :] = bias
            return before
        lax.fori_loop(0, nkc, body, before)

    def head(h):
        return slice(h * qb, (h + 1) * qb)

    for h in range(N_HEADS):
        qat_scr[:, head(h)] = (_dot(wuk_ref[h], qt_ref[h * HEAD_DIM:(h + 1) * HEAD_DIM, :])
                               * (ATTN_SCALE * LOG2E)).astype(BF16)
    m_scr[...] = jnp.full(m_scr.shape, 0.5 * NEG, F32)
    acc_scr[...] = jnp.zeros(acc_scr.shape, F32)

    def attend(c_rows, ct_cols, bias):
        r = c_rows.shape[0]
        lg_scr[0:r, :] = _dot(c_rows, qat_scr[...])
        for h in range(N_HEADS):
            lg = lg_scr[0:r, head(h)] + bias
            m_old = m_scr[:, head(h)]
            m_new = jnp.maximum(m_old, jnp.max(lg, axis=0, keepdims=True))
            p = jnp.exp2(lg - m_new).astype(BF16)
            m_scr[:, head(h)] = m_new
            acc_scr[:, head(h)] = (acc_scr[:, head(h)] * jnp.exp2(m_old - m_new)
                                   + _dot(ct_cols, p))

    attend(cm_ref[...], cmt_ref[...], s_scr[meta_rows, :])

    def attend_body(kc, _):
        attend(c_ref[rows(kc), :], ct_ref[kc], s_scr[rows(kc), :])
        return 0
    lax.fori_loop(0, nkc, attend_body, 0)

    for h in range(N_HEADS):
        olat = acc_scr[0:KV_LATENT, head(h)] / acc_scr[KV_LATENT:KV_LATENT + 1, head(h)]
        ot_scr[h * HEAD_DIM:(h + 1) * HEAD_DIM, :] = _dot(wuvt_ref[h], olat.astype(BF16))
    o_ref[...] = ot_scr[...].T.astype(BF16)


def _mix_kernel(x_ref, g1_ref, wgate_ref, attn_ref, yp_ref, wba_ref, wbp_ref, wout_ref,
                g2_ref, wr_ref, wrhi_ref, br_ref, h1_ref, u2_ref, route_ref, count_ref,
                merged_scr):
    x = x_ref[...]
    u = _rms(x, g1_ref[...]).astype(BF16)
    attn = attn_ref[...]
    yp = yp_ref[...]
    for cb in range(D_MODEL // MIX_COLS):
        ca = slice(cb * MIX_COLS, (cb + 1) * MIX_COLS)
        cp = slice(D_MODEL + cb * MIX_COLS, D_MODEL + (cb + 1) * MIX_COLS)
        g_attn = 1.0 / (1.0 + jnp.exp(-_dot(u, wgate_ref[:, ca])))
        g_pool = 1.0 / (1.0 + jnp.exp(-_dot(u, wgate_ref[:, cp])))
        merged_scr[:, ca] = (g_attn * _dot(attn, wba_ref[:, ca])
                             + g_pool * _dot(yp, wbp_ref[:, ca])).astype(BF16)
    h1 = x + _dot(merged_scr[...], wout_ref[...])
    h1_ref[...] = h1
    u2 = _rms(h1, g2_ref[...])
    nsub = D_MODEL // LANES
    for s in range(nsub):
        u2_ref[pl.ds(s, x.shape[0], stride=nsub), :] = u2[:, s * LANES:(s + 1) * LANES]
    u2_hi = u2.astype(BF16)

    u2_lo = (u2 - u2_hi.astype(F32)).astype(BF16)
    hi_both = _dot(u2_hi, wr_ref[...])
    lg = hi_both[:, :LANES] + hi_both[:, LANES:] + _dot(u2_lo, wrhi_ref[...]) + br_ref[...]
    lane = lax.broadcasted_iota(jnp.int32, lg.shape, 1)
    is_g = lane < N_GROUPS
    gl = jnp.where(is_g, lg, NEG)
    gmax = jnp.max(gl, axis=1, keepdims=True)
    gidx = jnp.min(jnp.where(gl == gmax, lane, LANES), axis=1, keepdims=True)
    p_g = 1.0 / jnp.sum(jnp.where(is_g, jnp.exp(gl - gmax), 0.0), axis=1, keepdims=True)
    e_lane = lane - N_GROUPS
    lane_grp = jnp.where(e_lane >= 0,
                         jnp.where(e_lane < N_EXPERTS, e_lane // EXPERTS_PER_GROUP, -1), -1)
    in_grp = lane_grp == gidx
    el = jnp.where(in_grp, lg, NEG)
    t1 = jnp.max(el, axis=1, keepdims=True)
    i1 = jnp.min(jnp.where(el == t1, lane, LANES), axis=1, keepdims=True)
    el2 = jnp.where(lane == i1, NEG, el)
    t2 = jnp.max(el2, axis=1, keepdims=True)
    i2 = jnp.min(jnp.where(el2 == t2, lane, LANES), axis=1, keepdims=True)
    r = jnp.exp(t2 - t1)
    p1 = 1.0 / (1.0 + r)
    p2 = r * p1
    e1 = (i1 - N_GROUPS).astype(F32)
    e2 = (i2 - N_GROUPS).astype(F32)
    route_ref[...] = jnp.where(lane == 0, e1, jnp.where(lane == 1, e2, jnp.where(
        lane == 2, p1 * p_g, jnp.where(lane == 3, p2 * p_g, 0.0))))
    chosen = jnp.where(lane == i1, 1.0, jnp.where(lane == i2, 1.0, 0.0))
    count_ref[...] = jnp.broadcast_to(jnp.sum(chosen, axis=0, keepdims=True), count_ref.shape)


def _moe_kernel(start_ref, cnt_ref, tok_ref, slot_ref, wt_ref, x2_ref, wg_ref, wu_ref, wd_ref,
                y2_ref, r2_scr, g2_scr, rt_scr):
    ts = x2_ref.shape[0] // SUBLANES
    tmx = g2_scr.shape[0] // SUBLANES
    st = pl.program_id(0)
    e = pl.program_id(1)
    n_e = pl.num_programs(1)
    nsub = D_MODEL // LANES

    def vreg_rows(i):
        return pl.ds(pl.multiple_of(i * SUBLANES, SUBLANES), SUBLANES)

    start = start_ref[st * n_e + e]
    cnt = cnt_ref[st * n_e + e]

    def tile_body(i, _):
        base = start + i * tmx

        def gather_body(r8, _):
            for u in range(SUBLANES):
                r = r8 * SUBLANES + u
                g2_scr[vreg_rows(r), :] = x2_ref[vreg_rows(tok_ref[base + r]), :]
            return 0
        lax.fori_loop(0, tmx // SUBLANES, gather_body, 0)

        xg = jnp.concatenate([g2_scr[pl.ds(s, tmx, stride=nsub), :] for s in range(nsub)],
                             axis=1).astype(BF16)
        a = _dot(xg, wg_ref[...])
        b = _dot(xg, wu_ref[...])
        hg = (a * (1.0 / (1.0 + jnp.exp(-a))) * b).astype(BF16)
        yr = _dot(hg, wd_ref[...])
        for s in range(nsub):
            rt_scr[pl.ds(s, tmx, stride=nsub), :] = yr[:, s * LANES:(s + 1) * LANES]
        r2_scr[pl.ds(pl.multiple_of(base * SUBLANES, SUBLANES), tmx * SUBLANES), :] = rt_scr[...]
        return 0

    lax.fori_loop(0, (cnt + tmx - 1) // tmx, tile_body, 0)

    @pl.when(e == n_e - 1)
    def _():
        def combine_body(t8, _):
            for u in range(SUBLANES):
                t = t8 * SUBLANES + u
                y2_ref[vreg_rows(t), :] = (
                    wt_ref[2 * t] * r2_scr[vreg_rows(slot_ref[2 * t]), :]
                    + wt_ref[2 * t + 1] * r2_scr[vreg_rows(slot_ref[2 * t + 1]), :])
            return 0
        lax.fori_loop(0, ts // SUBLANES, combine_body, 0)


def _final_kernel(h1_ref, y2_ref, gf_ref, o_ref):
    nsub = D_MODEL // LANES
    y = jnp.concatenate([y2_ref[pl.ds(s, h1_ref.shape[0], stride=nsub), :] for s in range(nsub)],
                        axis=1)
    o_ref[...] = _rms(h1_ref[...] + y, gf_ref[...])


def _const_spec(shape):
    nd = len(shape)
    return pl.BlockSpec(shape, lambda *_: (0,) * nd)


def _params(n_axes):
    return pltpu.CompilerParams(dimension_semantics=("arbitrary",) * n_axes,
                                vmem_limit_bytes=VMEM_LIMIT)


def kernel(x, meta_tokens, norm1_g, w_in, kv_norm_g, w_uk, w_uv, w_pool, pool_scale,
           w_branch_attn, w_branch_pool, w_out, norm2_g, w_group_router, b_group_router,
           w_expert_router, b_expert_router, w_expert_gate, w_expert_up, w_expert_down,
           final_norm_g):
    B, S, D = x.shape
    assert D == D_MODEL and S % QB == 0 and S % TM_PROJ == 0 and w_in.shape[0] == 1
    assert QB == KC and TM_PROJ % KC == 0 and QB % CHUNK == 0
    N = B * S
    k_top = min(TOPK_MAX, S // 4)
    xr = x.reshape(N, D)

    wi = w_in[0]
    w1 = jnp.concatenate(
        [wi[:, 0:640], wi[:, 640:896], wi[:, 936:1448], wi[:, 896:936],
         jnp.zeros((D, W1_WIDTH - 1448), F32)], axis=1).astype(BF16)
    wgate = wi[:, 1448:].astype(BF16)
    g1 = norm1_g[0].reshape(1, D)
    kvg = kv_norm_g[0].reshape(1, KV_LATENT)
    wpool = w_pool[0].astype(BF16)
    pscale = pool_scale[0].reshape(1, POOL_WIDTH)
    wuk = jnp.transpose(w_uk[0], (1, 0, 2)).astype(BF16)
    wuvt = jnp.transpose(w_uv[0], (1, 2, 0)).astype(BF16)
    wr = jnp.concatenate(
        [w_group_router[0], w_expert_router[0].reshape(D, N_EXPERTS),
         jnp.zeros((D, LANES - N_GROUPS - N_EXPERTS), F32)], axis=1)
    br = jnp.concatenate(
        [b_group_router[0], b_expert_router[0].reshape(N_EXPERTS),
         jnp.zeros((LANES - N_GROUPS - N_EXPERTS,), F32)]).reshape(1, LANES)

    c_m, tail_m, pv_m = pl.pallas_call(
        _meta_kernel,
        out_shape=(jax.ShapeDtypeStruct((N_META, KV_LATENT), F32),
                   jax.ShapeDtypeStruct((N_META, LANES), F32),
                   jax.ShapeDtypeStruct((N_META, POOL_WIDTH), F32)),
        name="meta",
    )(meta_tokens, g1, w1, kvg)
    cm = c_m.astype(BF16)
    cmt = jnp.concatenate([cm.T, jnp.ones((1, N_META), BF16),
                           jnp.zeros((CT_ROWS - KV_LATENT - 1, N_META), BF16)], axis=0)
    ikm = tail_m[:, :IDX_DIM].astype(BF16)

    tpb = S // TM_PROJ
    tok = lambda w: pl.BlockSpec((TM_PROJ, w), lambda b, i: (b * tpb + i, 0))
    tok_t = lambda w: pl.BlockSpec((w, TM_PROJ), lambda b, i: (0, b * tpb + i))
    qt, c, ct, iqt, ik, iwt, yp = pl.pallas_call(
        _proj_kernel,
        grid=(B, tpb),
        in_specs=[tok(D), _const_spec((1, D)), _const_spec((D, W1_WIDTH)),
                  _const_spec((1, KV_LATENT)), _const_spec((N_META, POOL_WIDTH)),
                  _const_spec((len(POOL_WINDOWS), POOL_GROUP, POOL_GROUP)),
                  _const_spec((1, POOL_WIDTH))],
        out_specs=[tok_t(ATTN_WIDTH), tok(KV_LATENT),
                   pl.BlockSpec((None, TM_PROJ // KC, CT_ROWS, KC), lambda b, i: (b, i, 0, 0)),
                   tok_t(IDX_HEADS * IDX_DIM), tok(IDX_DIM), tok_t(IDX_HEADS), tok(POOL_WIDTH)],
        out_shape=(jax.ShapeDtypeStruct((ATTN_WIDTH, N), BF16),
                   jax.ShapeDtypeStruct((N, KV_LATENT), BF16),
                   jax.ShapeDtypeStruct((B, S // KC, CT_ROWS, KC), BF16),
                   jax.ShapeDtypeStruct((IDX_HEADS * IDX_DIM, N), BF16),
                   jax.ShapeDtypeStruct((N, IDX_DIM), BF16),
                   jax.ShapeDtypeStruct((IDX_HEADS, N), F32),
                   jax.ShapeDtypeStruct((N, POOL_WIDTH), BF16)),
        scratch_shapes=[pltpu.VMEM((TM_PROJ + N_META, POOL_WIDTH), F32)],
        compiler_params=_params(2),
        name="proj",
    )(xr, g1, w1, kvg, pv_m, wpool, pscale)

    nqb = S // QB
    qcol = lambda w: pl.BlockSpec((w, QB), lambda b, j: (0, b * nqb + j))
    attn = pl.pallas_call(
        functools.partial(_attn_kernel, k_top=float(k_top)),
        grid=(B, nqb),
        in_specs=[pl.BlockSpec((S, IDX_DIM), lambda b, j: (b, 0)),
                  _const_spec((N_META, IDX_DIM)),
                  pl.BlockSpec((S, KV_LATENT), lambda b, j: (b, 0)),
                  _const_spec((N_META, KV_LATENT)),
                  pl.BlockSpec((None, S // KC, CT_ROWS, KC), lambda b, j: (b, 0, 0, 0)),
                  _const_spec((CT_ROWS, N_META)),
                  qcol(IDX_HEADS * IDX_DIM), qcol(IDX_HEADS), qcol(ATTN_WIDTH),
                  _const_spec((N_HEADS, KV_LATENT, HEAD_DIM)),
                  _const_spec((N_HEADS, HEAD_DIM, KV_LATENT))],
        out_specs=pl.BlockSpec((QB, ATTN_WIDTH), lambda b, j: (b * nqb + j, 0)),
        out_shape=jax.ShapeDtypeStruct((N, ATTN_WIDTH), BF16),
        scratch_shapes=[pltpu.VMEM((S + N_META, QB), F32),
                        pltpu.VMEM((KV_LATENT, N_HEADS * QB), BF16),
                        pltpu.VMEM((KC, N_HEADS * QB), F32),
                        pltpu.VMEM((1, N_HEADS * QB), F32),
                        pltpu.VMEM((CT_ROWS, N_HEADS * QB), F32),
                        pltpu.VMEM((ATTN_WIDTH, QB), F32)],
        compiler_params=_params(2),
        name="attn",
    )(ik, ikm, c, cm, ct, cmt, iqt, iwt, qt, wuk, wuvt)

    wr_hi = wr.astype(BF16)
    wr_lo = (wr - wr_hi.astype(F32)).astype(BF16)
    tok1 = lambda w: pl.BlockSpec((TM_MIX, w), lambda i: (i, 0))
    h1, u2, route, tile_counts = pl.pallas_call(
        _mix_kernel,
        grid=(N // TM_MIX,),
        in_specs=[tok1(D), _const_spec((1, D)), _const_spec((D, 2 * D)), tok1(ATTN_WIDTH),
                  tok1(POOL_WIDTH), _const_spec((ATTN_WIDTH, D)), _const_spec((POOL_WIDTH, D)),
                  _const_spec((D, D)), _const_spec((1, D)), _const_spec((D, 2 * LANES)),
                  _const_spec((D, LANES)), _const_spec((1, LANES))],
        out_specs=[tok1(D), pl.BlockSpec((TM_MIX * SUBLANES, LANES), lambda i: (i, 0)),
                   tok1(LANES), pl.BlockSpec((SUBLANES, LANES), lambda i: (i, 0))],
        out_shape=(jax.ShapeDtypeStruct((N, D), F32),
                   jax.ShapeDtypeStruct((N * SUBLANES, LANES), F32),
                   jax.ShapeDtypeStruct((N, LANES), F32),
                   jax.ShapeDtypeStruct((N // TM_MIX * SUBLANES, LANES), F32)),
        scratch_shapes=[pltpu.VMEM((TM_MIX, D), BF16)],
        compiler_params=_params(1),
        name="mix",
    )(xr, g1, wgate, attn, yp, w_branch_attn[0].astype(BF16), w_branch_pool[0].astype(BF16),
      w_out[0].astype(BF16), norm2_g[0].reshape(1, D),
      jnp.concatenate([wr_hi, wr_lo], axis=1), wr_hi, br)

    nst = N // TS_MOE
    n_asg = 2 * TS_MOE
    eid = route[:, 0:2].astype(jnp.int32).reshape(nst, n_asg)
    wts = route[:, 2:4].reshape(nst * n_asg)
    order = jnp.argsort(eid, axis=1, stable=True).astype(jnp.int32)
    slot = jnp.argsort(order, axis=1).astype(jnp.int32).reshape(nst * n_asg)
    tok_sorted = jnp.pad(order // 2, ((0, 0), (0, LIST_PAD))).reshape(nst * (n_asg + LIST_PAD))
    counts = tile_counts.reshape(nst, TS_MOE // TM_MIX, SUBLANES, LANES)[
        :, :, 0, N_GROUPS:N_GROUPS + N_EXPERTS].sum(axis=1).astype(jnp.int32)
    starts = (jnp.cumsum(counts, axis=1) - counts).reshape(nst * N_EXPERTS)
    counts = counts.reshape(nst * N_EXPERTS)

    weg = w_expert_gate[0].reshape(N_EXPERTS, D, EXPERT_HIDDEN).astype(BF16)
    weu = w_expert_up[0].reshape(N_EXPERTS, D, EXPERT_HIDDEN).astype(BF16)
    wed = w_expert_down[0].reshape(N_EXPERTS, EXPERT_HIDDEN, D).astype(BF16)
    smem = lambda n: pl.BlockSpec((n,), lambda s, e, *_: (s,), memory_space=pltpu.SMEM)
    y = pl.pallas_call(
        _moe_kernel,
        grid_spec=pltpu.PrefetchScalarGridSpec(
            num_scalar_prefetch=2,
            grid=(nst, N_EXPERTS),
            in_specs=[smem(n_asg + LIST_PAD), smem(n_asg), smem(n_asg),
                      pl.BlockSpec((TS_MOE * SUBLANES, LANES), lambda s, e, *_: (s, 0)),
                      pl.BlockSpec((None, D, EXPERT_HIDDEN), lambda s, e, *_: (e, 0, 0)),
                      pl.BlockSpec((None, D, EXPERT_HIDDEN), lambda s, e, *_: (e, 0, 0)),
                      pl.BlockSpec((None, EXPERT_HIDDEN, D), lambda s, e, *_: (e, 0, 0))],
            out_specs=pl.BlockSpec((TS_MOE * SUBLANES, LANES), lambda s, e, *_: (s, 0)),
            scratch_shapes=[pltpu.VMEM(((n_asg + TMX_MOE) * SUBLANES, LANES), F32),
                            pltpu.VMEM((TMX_MOE * SUBLANES, LANES), F32),
                            pltpu.VMEM((TMX_MOE * SUBLANES, LANES), F32)]),
        out_shape=jax.ShapeDtypeStruct((N * SUBLANES, LANES), F32),
        compiler_params=_params(2),
        name="moe",
    )(starts, counts, tok_sorted, slot, wts, u2, weg, weu, wed)

    ftok = pl.BlockSpec((TM_FINAL, D), lambda i: (i, 0))
    out = pl.pallas_call(
        _final_kernel,
        grid=(N // TM_FINAL,),
        in_specs=[ftok, pl.BlockSpec((TM_FINAL * SUBLANES, LANES), lambda i: (i, 0)),
                  _const_spec((1, D))],
        out_specs=ftok,
        out_shape=jax.ShapeDtypeStruct((N, D), F32),
        compiler_params=_params(1),
        name="final",
    )(h1, y, final_norm_g.reshape(1, D))
    return out.reshape(B, S, D)
```

```python
import functools

import jax
import jax.numpy as jnp
from jax import lax
from jax.experimental import pallas as pl
from jax.experimental.pallas import tpu as pltpu

F32 = jnp.float32
BF16 = jnp.bfloat16

D_MODEL = 1024
CHUNK = 64
N_META = 16
N_HEADS = 8
HEAD_DIM = 64
ATTN_WIDTH = N_HEADS * HEAD_DIM
KV_LATENT = 128
IDX_HEADS = 8
IDX_DIM = 32
TOPK_MAX = 256
ATTN_SCALE = HEAD_DIM ** -0.5
IDX_SCALE = (IDX_HEADS ** -0.5) * (IDX_DIM ** -0.5)
POOL_WINDOWS = (2, 4, 8, 16)
POOL_WIDTH = 512
POOL_GROUP = 128
N_GROUPS = 4
EXPERTS_PER_GROUP = 8
N_EXPERTS = N_GROUPS * EXPERTS_PER_GROUP
EXPERT_HIDDEN = 256
EPS = 1e-6

LANES = 128
SUBLANES = 8
W1_WIDTH = 1536
NEG = -1e30
POS = 1e30
VMEM_LIMIT = 56 * 1024 * 1024

TM_PROJ = 1024
TM_MIX = 1024
MIX_ROWS = 256
MIX_COLS = 256
QB = 256
KC = 256
MAX_BISECT = 40
BISECT_PER_CHECK = 4
FIRST_TIE_CHECK = 16
CT_ROWS = KV_LATENT + 16
LOG2E = 1.4426950408889634
TS_MOE = 2048
TMX_MOE = 160
LIST_PAD = 1024
TM_FINAL = 1024


def _rms(x, g):
    return x * lax.rsqrt(jnp.mean(x * x, axis=-1, keepdims=True) + EPS) * g


def _dot(a, b):
    return jnp.dot(a, b, preferred_element_type=F32)


def _meta_kernel(m_ref, g1_ref, w1_ref, kvg_ref, c_ref, tail_ref, pv_ref):
    u = _rms(m_ref[...], g1_ref[...]).astype(BF16)
    p = _dot(u, w1_ref[...])
    c_ref[...] = _rms(p[:, 512:640], kvg_ref[...])
    pv_ref[...] = p[:, 896:1408]
    tail_ref[...] = p[:, 1408:1536]


def _proj_kernel(x_ref, g1_ref, w1_ref, kvg_ref, pvmeta_ref, wpool_ref, pscale_ref,
                 qt_ref, c_ref, ct_ref, iqt_ref, ik_ref, iwt_ref, yp_ref, ext_ref):
    tm = x_ref.shape[0]

    @pl.when(pl.program_id(1) == 0)
    def _():
        ext_ref[0:N_META, :] = pvmeta_ref[...]

    for r0 in range(0, tm, KC):
        rs = slice(r0, r0 + KC)
        u = _rms(x_ref[rs, :], g1_ref[...]).astype(BF16)
        p = _dot(u, w1_ref[...])
        qt_ref[:, rs] = p[:, 0:512].T.astype(BF16)
        c = _rms(p[:, 512:640], kvg_ref[...])
        c_ref[rs, :] = c.astype(BF16)
        ct_ref[r0 // KC] = jnp.concatenate(
            [c.T, jnp.ones((1, KC), F32), jnp.zeros((CT_ROWS - KV_LATENT - 1, KC), F32)],
            axis=0).astype(BF16)
        iqt_ref[:, rs] = p[:, 640:896].T.astype(BF16)
        tail = p[:, 1408:1536]
        ik_ref[rs, :] = tail[:, 0:IDX_DIM].astype(BF16)
        iwt_ref[:, rs] = tail.T[IDX_DIM:IDX_DIM + IDX_HEADS, :] * IDX_SCALE
        pv = p[:, 896:1408]
        ext_ref[N_META + r0:N_META + r0 + KC, :] = pv
        for g, w in enumerate(POOL_WINDOWS):
            cols = slice(g * POOL_GROUP, (g + 1) * POOL_GROUP)
            acc = pv[:, cols]
            for k in range(1, w):
                acc = acc + ext_ref[N_META + r0 - k:N_META + r0 - k + KC, cols]
            d = acc * (1.0 / w) - pv[:, cols]
            y = _dot(d.astype(BF16), wpool_ref[g]) * pscale_ref[:, cols]
            yp_ref[rs, cols] = y.astype(BF16)
    ext_ref[0:N_META, :] = ext_ref[tm:tm + N_META, :]


def _attn_kernel(ik_ref, ikm_ref, c_ref, cm_ref, ct_ref, cmt_ref, iqt_ref, iwt_ref, qt_ref,
                 wuk_ref, wuvt_ref, o_ref,
                 s_scr, qat_scr, lg_scr, m_scr, acc_scr, ot_scr, *, k_top):
    n_real = c_ref.shape[0]
    qb = iqt_ref.shape[1]
    j = pl.program_id(1)
    nkc = j + 1
    qchunk = (j * qb + lax.broadcasted_iota(jnp.int32, (1, qb), 1)) // CHUNK
    meta_rows = pl.ds(n_real, N_META)

    def rows(kc):
        return pl.ds(pl.multiple_of(kc * KC, KC), KC)

    def fold(x, op):
        groups = x.shape[0] // SUBLANES
        chains = 4 if groups % 4 == 0 else 1
        x = x.reshape(groups // chains, chains, SUBLANES, qb)
        return op(op(x, axis=0), axis=0)

    iqt_heads = [iqt_ref[h * IDX_DIM:(h + 1) * IDX_DIM, :] for h in range(IDX_HEADS)]

    def scores(ik_rows):
        acc = None
        for h in range(IDX_HEADS):
            t = jnp.maximum(_dot(ik_rows, iqt_heads[h]), 0.0) * iwt_ref[h:h + 1, :]
            acc = t if acc is None else acc + t
        return acc

    sm = scores(ikm_ref[...])
    s_scr[meta_rows, :] = sm
    mn0 = jnp.min(sm, axis=0, keepdims=True)
    mx0 = jnp.max(sm, axis=0, keepdims=True)

    def score_body(kc, carry):
        mn, mx = carry
        sc = scores(ik_ref[rows(kc), :])
        s_scr[rows(kc), :] = sc
        return jnp.minimum(mn, fold(sc, jnp.min)), jnp.maximum(mx, fold(sc, jnp.max))

    mn8, mx8 = lax.fori_loop(0, j, score_body, (jnp.full((SUBLANES, qb), POS, F32),
                                                jnp.full((SUBLANES, qb), NEG, F32)))
    sc = scores(ik_ref[rows(j), :])
    adm = (j * KC + lax.broadcasted_iota(jnp.int32, (KC, 1), 0)) // CHUNK <= qchunk
    s_scr[rows(j), :] = jnp.where(adm, sc, NEG)
    mn8 = jnp.minimum(mn8, fold(jnp.where(adm, sc, POS), jnp.min))
    mx8 = jnp.maximum(mx8, fold(jnp.where(adm, sc, NEG), jnp.max))
    mn = jnp.minimum(mn0, jnp.min(mn8, axis=0, keepdims=True))
    mx = jnp.maximum(mx0, jnp.max(mx8, axis=0, keepdims=True))

    def count_ge(th):
        def body(kc, cnt):
            return cnt + fold(jnp.where(s_scr[rows(kc), :] >= th, 1.0, 0.0), jnp.sum)
        cnt = lax.fori_loop(0, nkc, body,
                            fold(jnp.where(s_scr[meta_rows, :] >= th, 1.0, 0.0), jnp.sum))
        return jnp.sum(cnt, axis=0, keepdims=True)

    def band_extent(lo, hi):
        def ext(sv):
            return (fold(jnp.where(sv >= lo, sv, POS), jnp.min),
                    fold(jnp.where(sv < hi, sv, NEG), jnp.max))

        def body(kc, carry):
            bmin, bmax = ext(s_scr[rows(kc), :])
            return jnp.minimum(carry[0], bmin), jnp.maximum(carry[1], bmax)
        bmin, bmax = lax.fori_loop(0, nkc, body, ext(s_scr[meta_rows, :]))
        return jnp.min(bmin, axis=0, keepdims=True), jnp.max(bmax, axis=0, keepdims=True)

    def any_lane(flags):
        return jnp.max(flags)

    def bisect_cond(carry):
        it, _, _, _, pending = carry
        return jnp.logical_and(it < MAX_BISECT, pending > 0)

    def bisect_body(carry):
        it, lo, hi, cnt_lo, _ = carry
        for _ in range(BISECT_PER_CHECK):
            mid = lo + (hi - lo) * 0.5
            cnt = count_ge(mid)
            ge = cnt >= k_top
            lo = jnp.where(ge, mid, lo)
            hi = jnp.where(ge, hi, mid)
            cnt_lo = jnp.where(ge, cnt, cnt_lo)
        it = it + BISECT_PER_CHECK
        over = jnp.where(cnt_lo > k_top, 1, 0)
        pending = any_lane(over)

        def tied_check():
            bmin, bmax = band_extent(lo, hi)
            return any_lane(jnp.where(bmin < bmax, over, 0))

        pending = lax.cond(jnp.logical_and(pending > 0, it >= FIRST_TIE_CHECK),
                           tied_check, lambda: pending)
        return it, lo, hi, cnt_lo, pending

    n_adm = (N_META + CHUNK * (qchunk + 1)).astype(F32)
    hi0 = mx + (mx - mn) + (jnp.abs(mx) * (2.0 ** -10) + 1e-30)
    _, lo, hi, cnt_lo, _ = lax.while_loop(
        bisect_cond, bisect_body,
        (jnp.int32(0), mn, hi0, n_adm, any_lane(jnp.where(n_adm > k_top, 1, 0))))
    pending = any_lane(jnp.where(cnt_lo > k_top, 1, 0))

    @pl.when(pending == 0)
    def _():
        def body(kc, _):
            s_scr[rows(kc), :] = jnp.where(s_scr[rows(kc), :] >= lo, 0.0, NEG)
            return 0
        lax.fori_loop(0, nkc, body, 0)
        s_scr[meta_rows, :] = jnp.where(s_scr[meta_rows, :] >= lo, 0.0, NEG)

    @pl.when(pending > 0)
    def _():
        need = k_top - count_ge(hi)

        def pick(sv, tri, before):
            band = jnp.where(sv >= lo, jnp.where(sv < hi, 1.0, 0.0), 0.0)
            rank = _dot(tri, band.astype(BF16)) + before
            take = jnp.where(rank <= need, band, 0.0)
            bias = jnp.where(sv >= hi, 0.0, jnp.where(take > 0.5, 0.0, NEG))
            return bias, before + jnp.sum(fold(band, jnp.sum), axis=0, keepdims=True)

        def tri(n):
            return jnp.where(lax.broadcasted_iota(jnp.int32, (n, n), 0)
                             >= lax.broadcasted_iota(jnp.int32, (n, n), 1), 1.0, 0.0).astype(BF16)

        bias_m, before = pick(s_scr[meta_rows, :], tri(N_META), jnp.zeros((1, qb), F32))
        s_scr[meta_rows, :] = bias_m
        tri_kc = tri(KC)

        def body(kc, before):
            bias, before = pick(s_scr[rows(kc), :], tri_kc, before)
            s_scr[rows(kc), :] = bias
            return before
        lax.fori_loop(0, nkc, body, before)

    def head(h):
        return slice(h * qb, (h + 1) * qb)

    for h in range(N_HEADS):
        qat_scr[:, head(h)] = (_dot(wuk_ref[h], qt_ref[h * HEAD_DIM:(h + 1) * HEAD_DIM, :])
                               * (ATTN_SCALE * LOG2E)).astype(BF16)
    m_scr[...] = jnp.full(m_scr.shape, 0.5 * NEG, F32)
    acc_scr[...] = jnp.zeros(acc_scr.shape, F32)

    def attend(c_rows, ct_cols, bias):
        r = c_rows.shape[0]
        lg_scr[0:r, :] = _dot(c_rows, qat_scr[...])
        for h in range(N_HEADS):
            lg = lg_scr[0:r, head(h)] + bias
            m_old = m_scr[:, head(h)]
            m_new = jnp.maximum(m_old, jnp.max(lg, axis=0, keepdims=True))
            p = jnp.exp2(lg - m_new).astype(BF16)
            m_scr[:, head(h)] = m_new
            acc_scr[:, head(h)] = (acc_scr[:, head(h)] * jnp.exp2(m_old - m_new)
                                   + _dot(ct_cols, p))

    attend(cm_ref[...], cmt_ref[...], s_scr[meta_rows, :])

    def attend_chunk(kc):
        attend(c_ref[rows(kc), :], ct_ref[kc], s_scr[rows(kc), :])

    def attend_pair(i, _):
        attend_chunk(2 * i)
        attend_chunk(2 * i + 1)
        return 0
    lax.fori_loop(0, nkc // 2, attend_pair, 0)

    @pl.when(nkc % 2 == 1)
    def _():
        attend_chunk(j)

    for h in range(N_HEADS):
        olat = acc_scr[0:KV_LATENT, head(h)] / acc_scr[KV_LATENT:KV_LATENT + 1, head(h)]
        ot_scr[h * HEAD_DIM:(h + 1) * HEAD_DIM, :] = _dot(wuvt_ref[h], olat.astype(BF16))
    o_ref[...] = ot_scr[...].T.astype(BF16)


def _mix_kernel(x_ref, g1_ref, wgate_ref, attn_ref, yp_ref, wba_ref, wbp_ref, wout_ref,
                g2_ref, wr_ref, wrhi_ref, br_ref, h1_ref, u2_ref, route_ref, count_ref,
                merged_scr):
    nsub = D_MODEL // LANES
    counts = None
    for r0 in range(0, x_ref.shape[0], MIX_ROWS):
        rows = pl.ds(r0, MIX_ROWS)
        c = _mix_group(x_ref.at[rows], g1_ref, wgate_ref, attn_ref.at[rows], yp_ref.at[rows],
                       wba_ref, wbp_ref, wout_ref, g2_ref, wr_ref, wrhi_ref, br_ref,
                       h1_ref.at[rows], u2_ref.at[pl.ds(r0 * nsub, MIX_ROWS * nsub)],
                       route_ref.at[rows], merged_scr.at[rows])
        counts = c if counts is None else counts + c
    count_ref[...] = jnp.broadcast_to(counts, count_ref.shape)


def _mix_group(x_ref, g1_ref, wgate_ref, attn_ref, yp_ref, wba_ref, wbp_ref, wout_ref,
               g2_ref, wr_ref, wrhi_ref, br_ref, h1_ref, u2_ref, route_ref, merged_scr):
    x = x_ref[...]
    u = _rms(x, g1_ref[...]).astype(BF16)
    attn = attn_ref[...]
    yp = yp_ref[...]
    for cb in range(D_MODEL // MIX_COLS):
        ca = slice(cb * MIX_COLS, (cb + 1) * MIX_COLS)
        cp = slice(D_MODEL + cb * MIX_COLS, D_MODEL + (cb + 1) * MIX_COLS)
        g_attn = 1.0 / (1.0 + jnp.exp(-_dot(u, wgate_ref[:, ca])))
        g_pool = 1.0 / (1.0 + jnp.exp(-_dot(u, wgate_ref[:, cp])))
        merged_scr[:, ca] = (g_attn * _dot(attn, wba_ref[:, ca])
                             + g_pool * _dot(yp, wbp_ref[:, ca])).astype(BF16)
    h1 = x + _dot(merged_scr[...], wout_ref[...])
    h1_ref[...] = h1
    u2 = _rms(h1, g2_ref[...])
    nsub = D_MODEL // LANES
    for s in range(nsub):
        u2_ref[pl.ds(s, x.shape[0], stride=nsub), :] = u2[:, s * LANES:(s + 1) * LANES]
    u2_hi = u2.astype(BF16)

    u2_lo = (u2 - u2_hi.astype(F32)).astype(BF16)
    hi_both = _dot(u2_hi, wr_ref[...])
    lg = hi_both[:, :LANES] + hi_both[:, LANES:] + _dot(u2_lo, wrhi_ref[...]) + br_ref[...]
    lane = lax.broadcasted_iota(jnp.int32, lg.shape, 1)
    is_g = lane < N_GROUPS
    gl = jnp.where(is_g, lg, NEG)
    gmax = jnp.max(gl, axis=1, keepdims=True)
    gidx = jnp.min(jnp.where(gl == gmax, lane, LANES), axis=1, keepdims=True)
    p_g = 1.0 / jnp.sum(jnp.where(is_g, jnp.exp(gl - gmax), 0.0), axis=1, keepdims=True)
    e_lane = lane - N_GROUPS
    lane_grp = jnp.where(e_lane >= 0,
                         jnp.where(e_lane < N_EXPERTS, e_lane // EXPERTS_PER_GROUP, -1), -1)
    in_grp = lane_grp == gidx
    el = jnp.where(in_grp, lg, NEG)
    t1 = jnp.max(el, axis=1, keepdims=True)
    i1 = jnp.min(jnp.where(el == t1, lane, LANES), axis=1, keepdims=True)
    el2 = jnp.where(lane == i1, NEG, el)
    t2 = jnp.max(el2, axis=1, keepdims=True)
    i2 = jnp.min(jnp.where(el2 == t2, lane, LANES), axis=1, keepdims=True)
    r = jnp.exp(t2 - t1)
    p1 = 1.0 / (1.0 + r)
    p2 = r * p1
    e1 = (i1 - N_GROUPS).astype(F32)
    e2 = (i2 - N_GROUPS).astype(F32)
    route_ref[...] = jnp.where(lane == 0, e1, jnp.where(lane == 1, e2, jnp.where(
        lane == 2, p1 * p_g, jnp.where(lane == 3, p2 * p_g, 0.0))))
    chosen = jnp.where(lane == i1, 1.0, jnp.where(lane == i2, 1.0, 0.0))
    return jnp.sum(chosen, axis=0, keepdims=True)


def _moe_kernel(start_ref, cnt_ref, tok_ref, slot_ref, wt_ref, x2_ref, wg_ref, wu_ref, wd_ref,
                y2_ref, r2_scr, g2_scr, rt_scr):
    ts = x2_ref.shape[0] // SUBLANES
    tmx = g2_scr.shape[0] // SUBLANES
    st = pl.program_id(0)
    e = pl.program_id(1)
    n_e = pl.num_programs(1)
    nsub = D_MODEL // LANES

    def vreg_rows(i):
        return pl.ds(pl.multiple_of(i * SUBLANES, SUBLANES), SUBLANES)

    start = start_ref[st * n_e + e]
    cnt = cnt_ref[st * n_e + e]

    def tile_body(i, _):
        base = start + i * tmx

        def gather_body(r8, _):
            for u in range(SUBLANES):
                r = r8 * SUBLANES + u
                g2_scr[vreg_rows(r), :] = x2_ref[vreg_rows(tok_ref[base + r]), :]
            return 0
        lax.fori_loop(0, tmx // SUBLANES, gather_body, 0)

        xg = jnp.concatenate([g2_scr[pl.ds(s, tmx, stride=nsub), :] for s in range(nsub)],
                             axis=1).astype(BF16)
        a = _dot(xg, wg_ref[...])
        b = _dot(xg, wu_ref[...])
        hg = (a * (1.0 / (1.0 + jnp.exp(-a))) * b).astype(BF16)
        yr = _dot(hg, wd_ref[...])
        for s in range(nsub):
            rt_scr[pl.ds(s, tmx, stride=nsub), :] = yr[:, s * LANES:(s + 1) * LANES]
        r2_scr[pl.ds(pl.multiple_of(base * SUBLANES, SUBLANES), tmx * SUBLANES), :] = rt_scr[...]
        return 0

    lax.fori_loop(0, (cnt + tmx - 1) // tmx, tile_body, 0)

    @pl.when(e == n_e - 1)
    def _():
        def combine_body(t8, _):
            for u in range(SUBLANES):
                t = t8 * SUBLANES + u
                y2_ref[vreg_rows(t), :] = (
                    wt_ref[2 * t] * r2_scr[vreg_rows(slot_ref[2 * t]), :]
                    + wt_ref[2 * t + 1] * r2_scr[vreg_rows(slot_ref[2 * t + 1]), :])
            return 0
        lax.fori_loop(0, ts // SUBLANES, combine_body, 0)


def _final_kernel(h1_ref, y2_ref, gf_ref, o_ref):
    nsub = D_MODEL // LANES
    y = jnp.concatenate([y2_ref[pl.ds(s, h1_ref.shape[0], stride=nsub), :] for s in range(nsub)],
                        axis=1)
    o_ref[...] = _rms(h1_ref[...] + y, gf_ref[...])


def _const_spec(shape):
    nd = len(shape)
    return pl.BlockSpec(shape, lambda *_: (0,) * nd, pipeline_mode=pl.Buffered(1))


def _params(n_axes):
    return pltpu.CompilerParams(dimension_semantics=("arbitrary",) * n_axes,
                                vmem_limit_bytes=VMEM_LIMIT)


def kernel(x, meta_tokens, norm1_g, w_in, kv_norm_g, w_uk, w_uv, w_pool, pool_scale,
           w_branch_attn, w_branch_pool, w_out, norm2_g, w_group_router, b_group_router,
           w_expert_router, b_expert_router, w_expert_gate, w_expert_up, w_expert_down,
           final_norm_g):
    B, S, D = x.shape
    assert D == D_MODEL and S % QB == 0 and S % TM_PROJ == 0 and w_in.shape[0] == 1
    assert QB == KC and TM_PROJ % KC == 0 and QB % CHUNK == 0
    N = B * S
    k_top = min(TOPK_MAX, S // 4)
    xr = x.reshape(N, D)

    wi = w_in[0]
    w1 = jnp.concatenate(
        [wi[:, 0:640], wi[:, 640:896], wi[:, 936:1448], wi[:, 896:936],
         jnp.zeros((D, W1_WIDTH - 1448), F32)], axis=1).astype(BF16)
    wgate = wi[:, 1448:].astype(BF16)
    g1 = norm1_g[0].reshape(1, D)
    kvg = kv_norm_g[0].reshape(1, KV_LATENT)
    wpool = w_pool[0].astype(BF16)
    pscale = pool_scale[0].reshape(1, POOL_WIDTH)
    wuk = jnp.transpose(w_uk[0], (1, 0, 2)).astype(BF16)
    wuvt = jnp.transpose(w_uv[0], (1, 2, 0)).astype(BF16)
    wr = jnp.concatenate(
        [w_group_router[0], w_expert_router[0].reshape(D, N_EXPERTS),
         jnp.zeros((D, LANES - N_GROUPS - N_EXPERTS), F32)], axis=1)
    br = jnp.concatenate(
        [b_group_router[0], b_expert_router[0].reshape(N_EXPERTS),
         jnp.zeros((LANES - N_GROUPS - N_EXPERTS,), F32)]).reshape(1, LANES)

    c_m, tail_m, pv_m = pl.pallas_call(
        _meta_kernel,
        out_shape=(jax.ShapeDtypeStruct((N_META, KV_LATENT), F32),
                   jax.ShapeDtypeStruct((N_META, LANES), F32),
                   jax.ShapeDtypeStruct((N_META, POOL_WIDTH), F32)),
        name="meta",
    )(meta_tokens, g1, w1, kvg)
    cm = c_m.astype(BF16)
    cmt = jnp.concatenate([cm.T, jnp.ones((1, N_META), BF16),
                           jnp.zeros((CT_ROWS - KV_LATENT - 1, N_META), BF16)], axis=0)
    ikm = tail_m[:, :IDX_DIM].astype(BF16)

    tpb = S // TM_PROJ
    tok = lambda w: pl.BlockSpec((TM_PROJ, w), lambda b, i: (b * tpb + i, 0))
    tok_t = lambda w: pl.BlockSpec((w, TM_PROJ), lambda b, i: (0, b * tpb + i))
    qt, c, ct, iqt, ik, iwt, yp = pl.pallas_call(
        _proj_kernel,
        grid=(B, tpb),
        in_specs=[tok(D), _const_spec((1, D)), _const_spec((D, W1_WIDTH)),
                  _const_spec((1, KV_LATENT)), _const_spec((N_META, POOL_WIDTH)),
                  _const_spec((len(POOL_WINDOWS), POOL_GROUP, POOL_GROUP)),
                  _const_spec((1, POOL_WIDTH))],
        out_specs=[tok_t(ATTN_WIDTH), tok(KV_LATENT),
                   pl.BlockSpec((None, TM_PROJ // KC, CT_ROWS, KC), lambda b, i: (b, i, 0, 0)),
                   tok_t(IDX_HEADS * IDX_DIM), tok(IDX_DIM), tok_t(IDX_HEADS), tok(POOL_WIDTH)],
        out_shape=(jax.ShapeDtypeStruct((ATTN_WIDTH, N), BF16),
                   jax.ShapeDtypeStruct((N, KV_LATENT), BF16),
                   jax.ShapeDtypeStruct((B, S // KC, CT_ROWS, KC), BF16),
                   jax.ShapeDtypeStruct((IDX_HEADS * IDX_DIM, N), BF16),
                   jax.ShapeDtypeStruct((N, IDX_DIM), BF16),
                   jax.ShapeDtypeStruct((IDX_HEADS, N), F32),
                   jax.ShapeDtypeStruct((N, POOL_WIDTH), BF16)),
        scratch_shapes=[pltpu.VMEM((TM_PROJ + N_META, POOL_WIDTH), F32)],
        compiler_params=_params(2),
        name="proj",
    )(xr, g1, w1, kvg, pv_m, wpool, pscale)

    nqb = S // QB
    qcol = lambda w: pl.BlockSpec((w, QB), lambda b, j: (0, b * nqb + j))
    attn = pl.pallas_call(
        functools.partial(_attn_kernel, k_top=float(k_top)),
        grid=(B, nqb),
        in_specs=[pl.BlockSpec((S, IDX_DIM), lambda b, j: (b, 0)),
                  _const_spec((N_META, IDX_DIM)),
                  pl.BlockSpec((S, KV_LATENT), lambda b, j: (b, 0)),
                  _const_spec((N_META, KV_LATENT)),
                  pl.BlockSpec((None, S // KC, CT_ROWS, KC), lambda b, j: (b, 0, 0, 0)),
                  _const_spec((CT_ROWS, N_META)),
                  qcol(IDX_HEADS * IDX_DIM), qcol(IDX_HEADS), qcol(ATTN_WIDTH),
                  _const_spec((N_HEADS, KV_LATENT, HEAD_DIM)),
                  _const_spec((N_HEADS, HEAD_DIM, KV_LATENT))],
        out_specs=pl.BlockSpec((QB, ATTN_WIDTH), lambda b, j: (b * nqb + j, 0)),
        out_shape=jax.ShapeDtypeStruct((N, ATTN_WIDTH), BF16),
        scratch_shapes=[pltpu.VMEM((S + N_META, QB), F32),
                        pltpu.VMEM((KV_LATENT, N_HEADS * QB), BF16),
                        pltpu.VMEM((KC, N_HEADS * QB), F32),
                        pltpu.VMEM((1, N_HEADS * QB), F32),
                        pltpu.VMEM((CT_ROWS, N_HEADS * QB), F32),
                        pltpu.VMEM((ATTN_WIDTH, QB), F32)],
        compiler_params=_params(2),
        name="attn",
    )(ik, ikm, c, cm, ct, cmt, iqt, iwt, qt, wuk, wuvt)

    wr_hi = wr.astype(BF16)
    wr_lo = (wr - wr_hi.astype(F32)).astype(BF16)
    tok1 = lambda w: pl.BlockSpec((TM_MIX, w), lambda i: (i, 0))
    h1, u2, route, tile_counts = pl.pallas_call(
        _mix_kernel,
        grid=(N // TM_MIX,),
        in_specs=[tok1(D), _const_spec((1, D)), _const_spec((D, 2 * D)), tok1(ATTN_WIDTH),
                  tok1(POOL_WIDTH), _const_spec((ATTN_WIDTH, D)), _const_spec((POOL_WIDTH, D)),
                  _const_spec((D, D)), _const_spec((1, D)), _const_spec((D, 2 * LANES)),
                  _const_spec((D, LANES)), _const_spec((1, LANES))],
        out_specs=[tok1(D), pl.BlockSpec((TM_MIX * SUBLANES, LANES), lambda i: (i, 0)),
                   tok1(LANES), pl.BlockSpec((SUBLANES, LANES), lambda i: (i, 0))],
        out_shape=(jax.ShapeDtypeStruct((N, D), F32),
                   jax.ShapeDtypeStruct((N * SUBLANES, LANES), F32),
                   jax.ShapeDtypeStruct((N, LANES), F32),
                   jax.ShapeDtypeStruct((N // TM_MIX * SUBLANES, LANES), F32)),
        scratch_shapes=[pltpu.VMEM((TM_MIX, D), BF16)],
        compiler_params=_params(1),
        name="mix",
    )(xr, g1, wgate, attn, yp, w_branch_attn[0].astype(BF16), w_branch_pool[0].astype(BF16),
      w_out[0].astype(BF16), norm2_g[0].reshape(1, D),
      jnp.concatenate([wr_hi, wr_lo], axis=1), wr_hi, br)

    nst = N // TS_MOE
    n_asg = 2 * TS_MOE
    eid = route[:, 0:2].astype(jnp.int32).reshape(nst, n_asg)
    wts = route[:, 2:4].reshape(nst * n_asg)
    order = jnp.argsort(eid, axis=1, stable=True).astype(jnp.int32)
    slot = jnp.argsort(order, axis=1).astype(jnp.int32).reshape(nst * n_asg)
    tok_sorted = jnp.pad(order // 2, ((0, 0), (0, LIST_PAD))).reshape(nst * (n_asg + LIST_PAD))
    counts = tile_counts.reshape(nst, TS_MOE // TM_MIX, SUBLANES, LANES)[
        :, :, 0, N_GROUPS:N_GROUPS + N_EXPERTS].sum(axis=1).astype(jnp.int32)
    starts = (jnp.cumsum(counts, axis=1) - counts).reshape(nst * N_EXPERTS)
    counts = counts.reshape(nst * N_EXPERTS)

    weg = w_expert_gate[0].reshape(N_EXPERTS, D, EXPERT_HIDDEN).astype(BF16)
    weu = w_expert_up[0].reshape(N_EXPERTS, D, EXPERT_HIDDEN).astype(BF16)
    wed = w_expert_down[0].reshape(N_EXPERTS, EXPERT_HIDDEN, D).astype(BF16)
    smem = lambda n: pl.BlockSpec((n,), lambda s, e, *_: (s,), memory_space=pltpu.SMEM)
    y = pl.pallas_call(
        _moe_kernel,
        grid_spec=pltpu.PrefetchScalarGridSpec(
            num_scalar_prefetch=2,
            grid=(nst, N_EXPERTS),
            in_specs=[smem(n_asg + LIST_PAD), smem(n_asg), smem(n_asg),
                      pl.BlockSpec((TS_MOE * SUBLANES, LANES), lambda s, e, *_: (s, 0)),
                      pl.BlockSpec((None, D, EXPERT_HIDDEN), lambda s, e, *_: (e, 0, 0)),
                      pl.BlockSpec((None, D, EXPERT_HIDDEN), lambda s, e, *_: (e, 0, 0)),
                      pl.BlockSpec((None, EXPERT_HIDDEN, D), lambda s, e, *_: (e, 0, 0))],
            out_specs=pl.BlockSpec((TS_MOE * SUBLANES, LANES), lambda s, e, *_: (s, 0)),
            scratch_shapes=[pltpu.VMEM(((n_asg + TMX_MOE) * SUBLANES, LANES), F32),
                            pltpu.VMEM((TMX_MOE * SUBLANES, LANES), F32),
                            pltpu.VMEM((TMX_MOE * SUBLANES, LANES), F32)]),
        out_shape=jax.ShapeDtypeStruct((N * SUBLANES, LANES), F32),
        compiler_params=_params(2),
        name="moe",
    )(starts, counts, tok_sorted, slot, wts, u2, weg, weu, wed)

    ftok = pl.BlockSpec((TM_FINAL, D), lambda i: (i, 0))
    out = pl.pallas_call(
        _final_kernel,
        grid=(N // TM_FINAL,),
        in_specs=[ftok, pl.BlockSpec((TM_FINAL * SUBLANES, LANES), lambda i: (i, 0)),
                  _const_spec((1, D))],
        out_specs=ftok,
        out_shape=jax.ShapeDtypeStruct((N, D), F32),
        compiler_params=_params(1),
        name="final",
    )(h1, y, final_norm_g.reshape(1, D))
    return out.reshape(B, S, D)
```

```python
import functools

import jax
import jax.numpy as jnp
from jax import lax
from jax.experimental import pallas as pl
from jax.experimental.pallas import tpu as pltpu

F32 = jnp.float32
BF16 = jnp.bfloat16

D_MODEL = 1024
CHUNK = 64
N_META = 16
N_HEADS = 8
HEAD_DIM = 64
ATTN_WIDTH = N_HEADS * HEAD_DIM
KV_LATENT = 128
IDX_HEADS = 8
IDX_DIM = 32
TOPK_MAX = 256
ATTN_SCALE = HEAD_DIM ** -0.5
IDX_SCALE = (IDX_HEADS ** -0.5) * (IDX_DIM ** -0.5)
POOL_WINDOWS = (2, 4, 8, 16)
POOL_WIDTH = 512
POOL_GROUP = 128
N_GROUPS = 4
EXPERTS_PER_GROUP = 8
N_EXPERTS = N_GROUPS * EXPERTS_PER_GROUP
EXPERT_HIDDEN = 256
EPS = 1e-6

LANES = 128
SUBLANES = 8
W1_WIDTH = 1536
NEG = -1e30
POS = 1e30
VMEM_LIMIT = 56 * 1024 * 1024

TM_PROJ = 1024
TM_MIX = 1024
MIX_ROWS = 256
MIX_COLS = 256
QB = 256
KC = 256
MAX_BISECT = 40
BISECT_PER_CHECK = 4
FIRST_TIE_CHECK = 16
CT_ROWS = KV_LATENT + 16
LOG2E = 1.4426950408889634
TS_MOE = 2048
TMX_MOE = 160
LIST_PAD = 1024
TM_FINAL = 1024


def _rms(x, g):
    return x * lax.rsqrt(jnp.mean(x * x, axis=-1, keepdims=True) + EPS) * g


def _dot(a, b):
    return jnp.dot(a, b, preferred_element_type=F32)


def _meta_kernel(m_ref, g1_ref, w1_ref, kvg_ref, c_ref, tail_ref, pv_ref):
    u = _rms(m_ref[...], g1_ref[...]).astype(BF16)
    p = _dot(u, w1_ref[...])
    c_ref[...] = _rms(p[:, 512:640], kvg_ref[...])
    pv_ref[...] = p[:, 896:1408]
    tail_ref[...] = p[:, 1408:1536]


def _proj_kernel(x_ref, g1_ref, w1_ref, kvg_ref, pvmeta_ref, wpool_ref, pscale_ref,
                 qt_ref, c_ref, ct_ref, iqt_ref, ik_ref, iwt_ref, yp_ref, ext_ref):
    tm = x_ref.shape[0]

    @pl.when(pl.program_id(1) == 0)
    def _():
        ext_ref[0:N_META, :] = pvmeta_ref[...]

    for r0 in range(0, tm, KC):
        rs = slice(r0, r0 + KC)
        u = _rms(x_ref[rs, :], g1_ref[...]).astype(BF16)
        p = _dot(u, w1_ref[...])
        qt_ref[:, rs] = p[:, 0:512].T.astype(BF16)
        c = _rms(p[:, 512:640], kvg_ref[...])
        c_ref[rs, :] = c.astype(BF16)
        ct_ref[r0 // KC] = jnp.concatenate(
            [c.T, jnp.ones((1, KC), F32), jnp.zeros((CT_ROWS - KV_LATENT - 1, KC), F32)],
            axis=0).astype(BF16)
        iqt_ref[:, rs] = p[:, 640:896].T.astype(BF16)
        tail = p[:, 1408:1536]
        ik_ref[rs, :] = tail[:, 0:IDX_DIM].astype(BF16)
        iwt_ref[:, rs] = tail.T[IDX_DIM:IDX_DIM + IDX_HEADS, :] * IDX_SCALE
        pv = p[:, 896:1408]
        ext_ref[N_META + r0:N_META + r0 + KC, :] = pv
        for g, w in enumerate(POOL_WINDOWS):
            cols = slice(g * POOL_GROUP, (g + 1) * POOL_GROUP)
            acc = pv[:, cols]
            for k in range(1, w):
                acc = acc + ext_ref[N_META + r0 - k:N_META + r0 - k + KC, cols]
            d = acc * (1.0 / w) - pv[:, cols]
            y = _dot(d.astype(BF16), wpool_ref[g]) * pscale_ref[:, cols]
            yp_ref[rs, cols] = y.astype(BF16)
    ext_ref[0:N_META, :] = ext_ref[tm:tm + N_META, :]


def _attn_kernel(ik_ref, ikm_ref, c_ref, cm_ref, ct_ref, cmt_ref, iqt_ref, iwt_ref, qt_ref,
                 wuk_ref, wuvt_ref, o_ref,
                 s_scr, qat_scr, lg_scr, m_scr, acc_scr, ot_scr, *, k_top):
    n_real = c_ref.shape[0]
    qb = iqt_ref.shape[1]
    j = pl.program_id(1)
    nkc = j + 1
    qchunk = (j * qb + lax.broadcasted_iota(jnp.int32, (1, qb), 1)) // CHUNK
    meta_rows = pl.ds(n_real, N_META)

    def rows(kc):
        return pl.ds(pl.multiple_of(kc * KC, KC), KC)

    def fold(x, op):
        groups = x.shape[0] // SUBLANES
        chains = 4 if groups % 4 == 0 else 1
        x = x.reshape(groups // chains, chains, SUBLANES, qb)
        return op(op(x, axis=0), axis=0)

    iqt_heads = [iqt_ref[h * IDX_DIM:(h + 1) * IDX_DIM, :] for h in range(IDX_HEADS)]

    def scores(ik_rows):
        acc = None
        for h in range(IDX_HEADS):
            t = jnp.maximum(_dot(ik_rows, iqt_heads[h]), 0.0) * iwt_ref[h:h + 1, :]
            acc = t if acc is None else acc + t
        return acc

    sm = scores(ikm_ref[...])
    s_scr[meta_rows, :] = sm
    mn0 = jnp.min(sm, axis=0, keepdims=True)
    mx0 = jnp.max(sm, axis=0, keepdims=True)

    def score_body(kc, carry):
        mn, mx = carry
        sc = scores(ik_ref[rows(kc), :])
        s_scr[rows(kc), :] = sc
        return jnp.minimum(mn, fold(sc, jnp.min)), jnp.maximum(mx, fold(sc, jnp.max))

    mn8, mx8 = lax.fori_loop(0, j, score_body, (jnp.full((SUBLANES, qb), POS, F32),
                                                jnp.full((SUBLANES, qb), NEG, F32)))
    sc = scores(ik_ref[rows(j), :])
    adm = (j * KC + lax.broadcasted_iota(jnp.int32, (KC, 1), 0)) // CHUNK <= qchunk
    s_scr[rows(j), :] = jnp.where(adm, sc, NEG)
    mn8 = jnp.minimum(mn8, fold(jnp.where(adm, sc, POS), jnp.min))
    mx8 = jnp.maximum(mx8, fold(jnp.where(adm, sc, NEG), jnp.max))
    mn = jnp.minimum(mn0, jnp.min(mn8, axis=0, keepdims=True))
    mx = jnp.maximum(mx0, jnp.max(mx8, axis=0, keepdims=True))

    def count_ge(th):
        def body(kc, cnt):
            return cnt + fold(jnp.where(s_scr[rows(kc), :] >= th, 1.0, 0.0), jnp.sum)
        cnt = lax.fori_loop(0, nkc, body,
                            fold(jnp.where(s_scr[meta_rows, :] >= th, 1.0, 0.0), jnp.sum))
        return jnp.sum(cnt, axis=0, keepdims=True)

    def band_extent(lo, hi):
        def ext(sv):
            return (fold(jnp.where(sv >= lo, sv, POS), jnp.min),
                    fold(jnp.where(sv < hi, sv, NEG), jnp.max))

        def body(kc, carry):
            bmin, bmax = ext(s_scr[rows(kc), :])
            return jnp.minimum(carry[0], bmin), jnp.maximum(carry[1], bmax)
        bmin, bmax = lax.fori_loop(0, nkc, body, ext(s_scr[meta_rows, :]))
        return jnp.min(bmin, axis=0, keepdims=True), jnp.max(bmax, axis=0, keepdims=True)

    def any_lane(flags):
        return jnp.max(flags)

    def bisect_cond(carry):
        it, _, _, _, pending = carry
        return jnp.logical_and(it < MAX_BISECT, pending > 0)

    def bisect_body(carry):
        it, lo, hi, cnt_lo, _ = carry
        for _ in range(BISECT_PER_CHECK):
            mid = lo + (hi - lo) * 0.5
            cnt = count_ge(mid)
            ge = cnt >= k_top
            lo = jnp.where(ge, mid, lo)
            hi = jnp.where(ge, hi, mid)
            cnt_lo = jnp.where(ge, cnt, cnt_lo)
        it = it + BISECT_PER_CHECK
        over = jnp.where(cnt_lo > k_top, 1, 0)
        pending = any_lane(over)

        def tied_check():
            bmin, bmax = band_extent(lo, hi)
            return any_lane(jnp.where(bmin < bmax, over, 0))

        pending = lax.cond(jnp.logical_and(pending > 0, it >= FIRST_TIE_CHECK),
                           tied_check, lambda: pending)
        return it, lo, hi, cnt_lo, pending

    n_adm = (N_META + CHUNK * (qchunk + 1)).astype(F32)
    hi0 = mx + (mx - mn) + (jnp.abs(mx) * (2.0 ** -10) + 1e-30)
    _, lo, hi, cnt_lo, _ = lax.while_loop(
        bisect_cond, bisect_body,
        (jnp.int32(0), mn, hi0, n_adm, any_lane(jnp.where(n_adm > k_top, 1, 0))))
    pending = any_lane(jnp.where(cnt_lo > k_top, 1, 0))

    @pl.when(pending == 0)
    def _():
        def body(kc, _):
            s_scr[rows(kc), :] = jnp.where(s_scr[rows(kc), :] >= lo, 0.0, NEG)
            return 0
        lax.fori_loop(0, nkc, body, 0)
        s_scr[meta_rows, :] = jnp.where(s_scr[meta_rows, :] >= lo, 0.0, NEG)

    @pl.when(pending > 0)
    def _():
        need = k_top - count_ge(hi)

        def pick(sv, tri, before):
            band = jnp.where(sv >= lo, jnp.where(sv < hi, 1.0, 0.0), 0.0)
            rank = _dot(tri, band.astype(BF16)) + before
            take = jnp.where(rank <= need, band, 0.0)
            bias = jnp.where(sv >= hi, 0.0, jnp.where(take > 0.5, 0.0, NEG))
            return bias, before + jnp.sum(fold(band, jnp.sum), axis=0, keepdims=True)

        def tri(n):
            return jnp.where(lax.broadcasted_iota(jnp.int32, (n, n), 0)
                             >= lax.broadcasted_iota(jnp.int32, (n, n), 1), 1.0, 0.0).astype(BF16)

        bias_m, before = pick(s_scr[meta_rows, :], tri(N_META), jnp.zeros((1, qb), F32))
        s_scr[meta_rows, :] = bias_m
        tri_kc = tri(KC)

        def body(kc, before):
            bias, before = pick(s_scr[rows(kc), :], tri_kc, before)
            s_scr[rows(kc), :] = bias
            return before
        lax.fori_loop(0, nkc, body, before)

    def head(h):
        return slice(h * qb, (h + 1) * qb)

    for h in range(N_HEADS):
        qat_scr[:, head(h)] = (_dot(wuk_ref[h], qt_ref[h * HEAD_DIM:(h + 1) * HEAD_DIM, :])
                               * (ATTN_SCALE * LOG2E)).astype(BF16)
    m_scr[...] = jnp.full(m_scr.shape, 0.5 * NEG, F32)
    acc_scr[...] = jnp.zeros(acc_scr.shape, F32)

    def attend(c_rows, ct_cols, bias):
        r = c_rows.shape[0]
        lg_scr[0:r, :] = _dot(c_rows, qat_scr[...])
        for h in range(N_HEADS):
            lg = lg_scr[0:r, head(h)] + bias
            m_old = m_scr[:, head(h)]
            m_new = jnp.maximum(m_old, jnp.max(lg, axis=0, keepdims=True))
            p = jnp.exp2(lg - m_new).astype(BF16)
            m_scr[:, head(h)] = m_new
            acc_scr[:, head(h)] = (acc_scr[:, head(h)] * jnp.exp2(m_old - m_new)
                                   + _dot(ct_cols, p))

    attend(cm_ref[...], cmt_ref[...], s_scr[meta_rows, :])

    def attend_chunk(kc):
        attend(c_ref[rows(kc), :], ct_ref[kc], s_scr[rows(kc), :])

    def attend_pair(i, _):
        attend_chunk(2 * i)
        attend_chunk(2 * i + 1)
        return 0
    lax.fori_loop(0, nkc // 2, attend_pair, 0)

    @pl.when(nkc % 2 == 1)
    def _():
        attend_chunk(j)

    for h in range(N_HEADS):
        olat = acc_scr[0:KV_LATENT, head(h)] / acc_scr[KV_LATENT:KV_LATENT + 1, head(h)]
        ot_scr[h * HEAD_DIM:(h + 1) * HEAD_DIM, :] = _dot(wuvt_ref[h], olat.astype(BF16))
    o_ref[...] = ot_scr[...].T.astype(BF16)


def _mix_kernel(x_ref, g1_ref, wgate_ref, attn_ref, yp_ref, wba_ref, wbp_ref, wout_ref,
                g2_ref, wr_ref, wrhi_ref, br_ref, h1_ref, u2_ref, route_ref, count_ref,
                merged_scr):
    nsub = D_MODEL // LANES
    counts = None
    for r0 in range(0, x_ref.shape[0], MIX_ROWS):
        rows = pl.ds(r0, MIX_ROWS)
        c = _mix_group(x_ref.at[rows], g1_ref, wgate_ref, attn_ref.at[rows], yp_ref.at[rows],
                       wba_ref, wbp_ref, wout_ref, g2_ref, wr_ref, wrhi_ref, br_ref,
                       h1_ref.at[rows], u2_ref.at[pl.ds(r0 * nsub, MIX_ROWS * nsub)],
                       route_ref.at[:, rows], merged_scr.at[rows])
        counts = c if counts is None else counts + c
    count_ref[...] = jnp.broadcast_to(counts, count_ref.shape)


def _mix_group(x_ref, g1_ref, wgate_ref, attn_ref, yp_ref, wba_ref, wbp_ref, wout_ref,
               g2_ref, wr_ref, wrhi_ref, br_ref, h1_ref, u2_ref, route_ref, merged_scr):
    x = x_ref[...]
    u = _rms(x, g1_ref[...]).astype(BF16)
    attn = attn_ref[...]
    yp = yp_ref[...]
    for cb in range(D_MODEL // MIX_COLS):
        ca = slice(cb * MIX_COLS, (cb + 1) * MIX_COLS)
        cp = slice(D_MODEL + cb * MIX_COLS, D_MODEL + (cb + 1) * MIX_COLS)
        g_attn = 1.0 / (1.0 + jnp.exp(-_dot(u, wgate_ref[:, ca])))
        g_pool = 1.0 / (1.0 + jnp.exp(-_dot(u, wgate_ref[:, cp])))
        merged_scr[:, ca] = (g_attn * _dot(attn, wba_ref[:, ca])
                             + g_pool * _dot(yp, wbp_ref[:, ca])).astype(BF16)
    h1 = x + _dot(merged_scr[...], wout_ref[...])
    h1_ref[...] = h1
    u2 = _rms(h1, g2_ref[...])
    nsub = D_MODEL // LANES
    for s in range(nsub):
        u2_ref[pl.ds(s, x.shape[0], stride=nsub), :] = u2[:, s * LANES:(s + 1) * LANES]
    u2_hi = u2.astype(BF16)

    u2_lo = (u2 - u2_hi.astype(F32)).astype(BF16)
    hi_both = _dot(u2_hi, wr_ref[...])
    lg = hi_both[:, :LANES] + hi_both[:, LANES:] + _dot(u2_lo, wrhi_ref[...]) + br_ref[...]
    lane = lax.broadcasted_iota(jnp.int32, lg.shape, 1)
    is_g = lane < N_GROUPS
    gl = jnp.where(is_g, lg, NEG)
    gmax = jnp.max(gl, axis=1, keepdims=True)
    gidx = jnp.min(jnp.where(gl == gmax, lane, LANES), axis=1, keepdims=True)
    p_g = 1.0 / jnp.sum(jnp.where(is_g, jnp.exp(gl - gmax), 0.0), axis=1, keepdims=True)
    e_lane = lane - N_GROUPS
    lane_grp = jnp.where(e_lane >= 0,
                         jnp.where(e_lane < N_EXPERTS, e_lane // EXPERTS_PER_GROUP, -1), -1)
    in_grp = lane_grp == gidx
    el = jnp.where(in_grp, lg, NEG)
    t1 = jnp.max(el, axis=1, keepdims=True)
    i1 = jnp.min(jnp.where(el == t1, lane, LANES), axis=1, keepdims=True)
    el2 = jnp.where(lane == i1, NEG, el)
    t2 = jnp.max(el2, axis=1, keepdims=True)
    i2 = jnp.min(jnp.where(el2 == t2, lane, LANES), axis=1, keepdims=True)
    r = jnp.exp(t2 - t1)
    p1 = 1.0 / (1.0 + r)
    p2 = r * p1
    e1 = (i1 - N_GROUPS).astype(F32)
    e2 = (i2 - N_GROUPS).astype(F32)
    record = jnp.where(lane == 0, e1, jnp.where(lane == 1, e2, jnp.where(
        lane == 2, p1 * p_g, jnp.where(lane == 3, p2 * p_g, 0.0))))
    route_ref[...] = record.T[0:SUBLANES, :]
    chosen = jnp.where(lane == i1, 1.0, jnp.where(lane == i2, 1.0, 0.0))
    return jnp.sum(chosen, axis=0, keepdims=True)


def _moe_kernel(start_ref, cnt_ref, tok_ref, slot_ref, wt_ref, x2_ref, wg_ref, wu_ref, wd_ref,
                y2_ref, r2_scr, ga_scr, gb_scr, rt_scr):
    ts = x2_ref.shape[0] // SUBLANES
    tmx = ga_scr.shape[0] // SUBLANES
    st = pl.program_id(0)
    e = pl.program_id(1)
    n_e = pl.num_programs(1)
    nsub = D_MODEL // LANES
    step = st * n_e + e

    def vreg_rows(i):
        return pl.ds(pl.multiple_of(i * SUBLANES, SUBLANES), SUBLANES)

    def gather(g_scr, base):
        def gather_body(r8, _):
            for u in range(SUBLANES):
                r = r8 * SUBLANES + u
                g_scr[vreg_rows(r), :] = x2_ref[vreg_rows(tok_ref[base + r]), :]
            return 0
        lax.fori_loop(0, tmx // SUBLANES, gather_body, 0)

    def expert_ffn(g_scr, base):
        xg = jnp.concatenate([g_scr[pl.ds(s, tmx, stride=nsub), :] for s in range(nsub)],
                             axis=1).astype(BF16)
        a = _dot(xg, wg_ref[...])
        b = _dot(xg, wu_ref[...])
        hg = (a * (1.0 / (1.0 + jnp.exp(-a))) * b).astype(BF16)
        yr = _dot(hg, wd_ref[...])
        for s in range(nsub):
            rt_scr[pl.ds(s, tmx, stride=nsub), :] = yr[:, s * LANES:(s + 1) * LANES]
        r2_scr[pl.ds(pl.multiple_of(base * SUBLANES, SUBLANES), tmx * SUBLANES), :] = rt_scr[...]

    start = start_ref[step]
    cnt = cnt_ref[step]

    @pl.when(e == 0)
    def _():
        gather(ga_scr, start)

    def run(cur_scr, nxt_scr):
        expert_ffn(cur_scr, start)
        if nxt_scr is not None:
            nxt = start_ref[step + 1]
            for r in range(tmx):
                nxt_scr[r * SUBLANES:(r + 1) * SUBLANES, :] = (
                    x2_ref[vreg_rows(tok_ref[nxt + r]), :])

        def extra_tile(i, _):
            gather(cur_scr, start + i * tmx)
            expert_ffn(cur_scr, start + i * tmx)
            return 0
        lax.fori_loop(1, (cnt + tmx - 1) // tmx, extra_tile, 0)

    last = e == n_e - 1
    even = e % 2 == 0

    @pl.when(jnp.logical_and(even, jnp.logical_not(last)))
    def _():
        run(ga_scr, gb_scr)

    @pl.when(jnp.logical_and(jnp.logical_not(even), jnp.logical_not(last)))
    def _():
        run(gb_scr, ga_scr)

    @pl.when(last)
    def _():
        run(ga_scr if (N_EXPERTS - 1) % 2 == 0 else gb_scr, None)

        def combine_body(t8, _):
            for u in range(SUBLANES):
                t = t8 * SUBLANES + u
                y2_ref[vreg_rows(t), :] = (
                    wt_ref[t] * r2_scr[vreg_rows(slot_ref[t]), :]
                    + wt_ref[ts + t] * r2_scr[vreg_rows(slot_ref[ts + t]), :])
            return 0
        lax.fori_loop(0, ts // SUBLANES, combine_body, 0)


def _final_kernel(h1_ref, y2_ref, gf_ref, o_ref):
    nsub = D_MODEL // LANES
    y = jnp.concatenate([y2_ref[pl.ds(s, h1_ref.shape[0], stride=nsub), :] for s in range(nsub)],
                        axis=1)
    o_ref[...] = _rms(h1_ref[...] + y, gf_ref[...])


def _const_spec(shape):
    nd = len(shape)
    return pl.BlockSpec(shape, lambda *_: (0,) * nd, pipeline_mode=pl.Buffered(1))


def _params(n_axes):
    return pltpu.CompilerParams(dimension_semantics=("arbitrary",) * n_axes,
                                vmem_limit_bytes=VMEM_LIMIT)


def kernel(x, meta_tokens, norm1_g, w_in, kv_norm_g, w_uk, w_uv, w_pool, pool_scale,
           w_branch_attn, w_branch_pool, w_out, norm2_g, w_group_router, b_group_router,
           w_expert_router, b_expert_router, w_expert_gate, w_expert_up, w_expert_down,
           final_norm_g):
    B, S, D = x.shape
    assert D == D_MODEL and S % QB == 0 and S % TM_PROJ == 0 and w_in.shape[0] == 1
    assert QB == KC and TM_PROJ % KC == 0 and QB % CHUNK == 0
    N = B * S
    k_top = min(TOPK_MAX, S // 4)
    xr = x.reshape(N, D)

    wi = w_in[0]
    w1 = jnp.concatenate(
        [wi[:, 0:640], wi[:, 640:896], wi[:, 936:1448], wi[:, 896:936],
         jnp.zeros((D, W1_WIDTH - 1448), F32)], axis=1).astype(BF16)
    wgate = wi[:, 1448:].astype(BF16)
    g1 = norm1_g[0].reshape(1, D)
    kvg = kv_norm_g[0].reshape(1, KV_LATENT)
    wpool = w_pool[0].astype(BF16)
    pscale = pool_scale[0].reshape(1, POOL_WIDTH)
    wuk = jnp.transpose(w_uk[0], (1, 0, 2)).astype(BF16)
    wuvt = jnp.transpose(w_uv[0], (1, 2, 0)).astype(BF16)
    wr = jnp.concatenate(
        [w_group_router[0], w_expert_router[0].reshape(D, N_EXPERTS),
         jnp.zeros((D, LANES - N_GROUPS - N_EXPERTS), F32)], axis=1)
    br = jnp.concatenate(
        [b_group_router[0], b_expert_router[0].reshape(N_EXPERTS),
         jnp.zeros((LANES - N_GROUPS - N_EXPERTS,), F32)]).reshape(1, LANES)

    c_m, tail_m, pv_m = pl.pallas_call(
        _meta_kernel,
        out_shape=(jax.ShapeDtypeStruct((N_META, KV_LATENT), F32),
                   jax.ShapeDtypeStruct((N_META, LANES), F32),
                   jax.ShapeDtypeStruct((N_META, POOL_WIDTH), F32)),
        name="meta",
    )(meta_tokens, g1, w1, kvg)
    cm = c_m.astype(BF16)
    cmt = jnp.concatenate([cm.T, jnp.ones((1, N_META), BF16),
                           jnp.zeros((CT_ROWS - KV_LATENT - 1, N_META), BF16)], axis=0)
    ikm = tail_m[:, :IDX_DIM].astype(BF16)

    tpb = S // TM_PROJ
    tok = lambda w: pl.BlockSpec((TM_PROJ, w), lambda b, i: (b * tpb + i, 0))
    tok_t = lambda w: pl.BlockSpec((w, TM_PROJ), lambda b, i: (0, b * tpb + i))
    qt, c, ct, iqt, ik, iwt, yp = pl.pallas_call(
        _proj_kernel,
        grid=(B, tpb),
        in_specs=[tok(D), _const_spec((1, D)), _const_spec((D, W1_WIDTH)),
                  _const_spec((1, KV_LATENT)), _const_spec((N_META, POOL_WIDTH)),
                  _const_spec((len(POOL_WINDOWS), POOL_GROUP, POOL_GROUP)),
                  _const_spec((1, POOL_WIDTH))],
        out_specs=[tok_t(ATTN_WIDTH), tok(KV_LATENT),
                   pl.BlockSpec((None, TM_PROJ // KC, CT_ROWS, KC), lambda b, i: (b, i, 0, 0)),
                   tok_t(IDX_HEADS * IDX_DIM), tok(IDX_DIM), tok_t(IDX_HEADS), tok(POOL_WIDTH)],
        out_shape=(jax.ShapeDtypeStruct((ATTN_WIDTH, N), BF16),
                   jax.ShapeDtypeStruct((N, KV_LATENT), BF16),
                   jax.ShapeDtypeStruct((B, S // KC, CT_ROWS, KC), BF16),
                   jax.ShapeDtypeStruct((IDX_HEADS * IDX_DIM, N), BF16),
                   jax.ShapeDtypeStruct((N, IDX_DIM), BF16),
                   jax.ShapeDtypeStruct((IDX_HEADS, N), F32),
                   jax.ShapeDtypeStruct((N, POOL_WIDTH), BF16)),
        scratch_shapes=[pltpu.VMEM((TM_PROJ + N_META, POOL_WIDTH), F32)],
        compiler_params=_params(2),
        name="proj",
    )(xr, g1, w1, kvg, pv_m, wpool, pscale)

    nqb = S // QB
    qcol = lambda w: pl.BlockSpec((w, QB), lambda b, j: (0, b * nqb + j))
    attn = pl.pallas_call(
        functools.partial(_attn_kernel, k_top=float(k_top)),
        grid=(B, nqb),
        in_specs=[pl.BlockSpec((S, IDX_DIM), lambda b, j: (b, 0)),
                  _const_spec((N_META, IDX_DIM)),
                  pl.BlockSpec((S, KV_LATENT), lambda b, j: (b, 0)),
                  _const_spec((N_META, KV_LATENT)),
                  pl.BlockSpec((None, S // KC, CT_ROWS, KC), lambda b, j: (b, 0, 0, 0)),
                  _const_spec((CT_ROWS, N_META)),
                  qcol(IDX_HEADS * IDX_DIM), qcol(IDX_HEADS), qcol(ATTN_WIDTH),
                  _const_spec((N_HEADS, KV_LATENT, HEAD_DIM)),
                  _const_spec((N_HEADS, HEAD_DIM, KV_LATENT))],
        out_specs=pl.BlockSpec((QB, ATTN_WIDTH), lambda b, j: (b * nqb + j, 0)),
        out_shape=jax.ShapeDtypeStruct((N, ATTN_WIDTH), BF16),
        scratch_shapes=[pltpu.VMEM((S + N_META, QB), F32),
                        pltpu.VMEM((KV_LATENT, N_HEADS * QB), BF16),
                        pltpu.VMEM((KC, N_HEADS * QB), F32),
                        pltpu.VMEM((1, N_HEADS * QB), F32),
                        pltpu.VMEM((CT_ROWS, N_HEADS * QB), F32),
                        pltpu.VMEM((ATTN_WIDTH, QB), F32)],
        compiler_params=_params(2),
        name="attn",
    )(ik, ikm, c, cm, ct, cmt, iqt, iwt, qt, wuk, wuvt)

    wr_hi = wr.astype(BF16)
    wr_lo = (wr - wr_hi.astype(F32)).astype(BF16)
    tok1 = lambda w: pl.BlockSpec((TM_MIX, w), lambda i: (i, 0))
    h1, u2, route, tile_counts = pl.pallas_call(
        _mix_kernel,
        grid=(N // TM_MIX,),
        in_specs=[tok1(D), _const_spec((1, D)), _const_spec((D, 2 * D)), tok1(ATTN_WIDTH),
                  tok1(POOL_WIDTH), _const_spec((ATTN_WIDTH, D)), _const_spec((POOL_WIDTH, D)),
                  _const_spec((D, D)), _const_spec((1, D)), _const_spec((D, 2 * LANES)),
                  _const_spec((D, LANES)), _const_spec((1, LANES))],
        out_specs=[tok1(D), pl.BlockSpec((TM_MIX * SUBLANES, LANES), lambda i: (i, 0)),
                   pl.BlockSpec((SUBLANES, TM_MIX), lambda i: (0, i)),
                   pl.BlockSpec((SUBLANES, LANES), lambda i: (i, 0))],
        out_shape=(jax.ShapeDtypeStruct((N, D), F32),
                   jax.ShapeDtypeStruct((N * SUBLANES, LANES), F32),
                   jax.ShapeDtypeStruct((SUBLANES, N), F32),
                   jax.ShapeDtypeStruct((N // TM_MIX * SUBLANES, LANES), F32)),
        scratch_shapes=[pltpu.VMEM((TM_MIX, D), BF16)],
        compiler_params=_params(1),
        name="mix",
    )(xr, g1, wgate, attn, yp, w_branch_attn[0].astype(BF16), w_branch_pool[0].astype(BF16),
      w_out[0].astype(BF16), norm2_g[0].reshape(1, D),
      jnp.concatenate([wr_hi, wr_lo], axis=1), wr_hi, br)

    nst = N // TS_MOE
    n_asg = 2 * TS_MOE
    per_tile = lambda a: a.reshape(2, nst, TS_MOE).transpose(1, 0, 2).reshape(nst, n_asg)
    eid = per_tile(route[0:2].astype(jnp.int32))
    wts = per_tile(route[2:4]).reshape(nst * n_asg)
    order = jnp.argsort(eid, axis=1, stable=True).astype(jnp.int32)
    slot = jnp.argsort(order, axis=1).astype(jnp.int32).reshape(nst * n_asg)
    tok_sorted = jnp.pad(order % TS_MOE, ((0, 0), (0, LIST_PAD))).reshape(
        nst * (n_asg + LIST_PAD))
    counts = tile_counts.reshape(nst, TS_MOE // TM_MIX, SUBLANES, LANES)[
        :, :, 0, N_GROUPS:N_GROUPS + N_EXPERTS].sum(axis=1).astype(jnp.int32)
    starts = (jnp.cumsum(counts, axis=1) - counts).reshape(nst * N_EXPERTS)
    counts = counts.reshape(nst * N_EXPERTS)

    weg = w_expert_gate[0].reshape(N_EXPERTS, D, EXPERT_HIDDEN).astype(BF16)
    weu = w_expert_up[0].reshape(N_EXPERTS, D, EXPERT_HIDDEN).astype(BF16)
    wed = w_expert_down[0].reshape(N_EXPERTS, EXPERT_HIDDEN, D).astype(BF16)
    smem = lambda n: pl.BlockSpec((n,), lambda s, e, *_: (s,), memory_space=pltpu.SMEM)
    y = pl.pallas_call(
        _moe_kernel,
        grid_spec=pltpu.PrefetchScalarGridSpec(
            num_scalar_prefetch=2,
            grid=(nst, N_EXPERTS),
            in_specs=[smem(n_asg + LIST_PAD), smem(n_asg), smem(n_asg),
                      pl.BlockSpec((TS_MOE * SUBLANES, LANES), lambda s, e, *_: (s, 0)),
                      pl.BlockSpec((None, D, EXPERT_HIDDEN), lambda s, e, *_: (e, 0, 0)),
                      pl.BlockSpec((None, D, EXPERT_HIDDEN), lambda s, e, *_: (e, 0, 0)),
                      pl.BlockSpec((None, EXPERT_HIDDEN, D), lambda s, e, *_: (e, 0, 0))],
            out_specs=pl.BlockSpec((TS_MOE * SUBLANES, LANES), lambda s, e, *_: (s, 0)),
            scratch_shapes=[pltpu.VMEM(((n_asg + TMX_MOE) * SUBLANES, LANES), F32),
                            pltpu.VMEM((TMX_MOE * SUBLANES, LANES), F32),
                            pltpu.VMEM((TMX_MOE * SUBLANES, LANES), F32),
                            pltpu.VMEM((TMX_MOE * SUBLANES, LANES), F32)]),
        out_shape=jax.ShapeDtypeStruct((N * SUBLANES, LANES), F32),
        compiler_params=_params(2),
        name="moe",
    )(starts, counts, tok_sorted, slot, wts, u2, weg, weu, wed)

    ftok = pl.BlockSpec((TM_FINAL, D), lambda i: (i, 0))
    out = pl.pallas_call(
        _final_kernel,
        grid=(N // TM_FINAL,),
        in_specs=[ftok, pl.BlockSpec((TM_FINAL * SUBLANES, LANES), lambda i: (i, 0)),
                  _const_spec((1, D))],
        out_specs=ftok,
        out_shape=jax.ShapeDtypeStruct((N, D), F32),
        compiler_params=_params(1),
        name="final",
    )(h1, y, final_norm_g.reshape(1, D))
    return out.reshape(B, S, D)
```

```python
import functools

import jax
import jax.numpy as jnp
from jax import lax
from jax.experimental import pallas as pl
from jax.experimental.pallas import tpu as pltpu

F32 = jnp.float32
BF16 = jnp.bfloat16

D_MODEL = 1024
CHUNK = 64
N_META = 16
N_HEADS = 8
HEAD_DIM = 64
ATTN_WIDTH = N_HEADS * HEAD_DIM
KV_LATENT = 128
IDX_HEADS = 8
IDX_DIM = 32
TOPK_MAX = 256
ATTN_SCALE = HEAD_DIM ** -0.5
IDX_SCALE = (IDX_HEADS ** -0.5) * (IDX_DIM ** -0.5)
POOL_WINDOWS = (2, 4, 8, 16)
POOL_WIDTH = 512
POOL_GROUP = 128
N_GROUPS = 4
EXPERTS_PER_GROUP = 8
N_EXPERTS = N_GROUPS * EXPERTS_PER_GROUP
EXPERT_HIDDEN = 256
EPS = 1e-6

LANES = 128
SUBLANES = 8
W1_WIDTH = 1536
NEG = -1e30
POS = 1e30
VMEM_LIMIT = 56 * 1024 * 1024

TM_PROJ = 1024
TM_MIX = 1024
MIX_ROWS = 256
MIX_COLS = 256
QB = 256
KC = 256
MAX_BISECT = 40
BISECT_PER_CHECK = 4
FIRST_TIE_CHECK = 16
CT_ROWS = KV_LATENT + 16
LOG2E = 1.4426950408889634
TS_MOE = 2048
TMX_MOE = 160
MOE_WEIGHT_BUFFERS = 3
LIST_PAD = 1024
TM_FINAL = 1024


def _rms(x, g):
    return x * lax.rsqrt(jnp.mean(x * x, axis=-1, keepdims=True) + EPS) * g


def _dot(a, b):
    return jnp.dot(a, b, preferred_element_type=F32)


def _meta_kernel(m_ref, g1_ref, w1_ref, kvg_ref, c_ref, tail_ref, pv_ref):
    u = _rms(m_ref[...], g1_ref[...]).astype(BF16)
    p = _dot(u, w1_ref[...])
    c_ref[...] = _rms(p[:, 512:640], kvg_ref[...])
    pv_ref[...] = p[:, 896:1408]
    tail_ref[...] = p[:, 1408:1536]


def _proj_kernel(x_ref, g1_ref, w1_ref, kvg_ref, pvmeta_ref, wpool_ref, pscale_ref,
                 qt_ref, c_ref, ct_ref, iqt_ref, ik_ref, iwt_ref, yp_ref, ext_ref):
    tm = x_ref.shape[0]

    @pl.when(pl.program_id(1) == 0)
    def _():
        ext_ref[0:N_META, :] = pvmeta_ref[...]

    for r0 in range(0, tm, KC):
        rs = slice(r0, r0 + KC)
        u = _rms(x_ref[rs, :], g1_ref[...]).astype(BF16)
        p = _dot(u, w1_ref[...])
        qt_ref[:, rs] = p[:, 0:512].T.astype(BF16)
        c = _rms(p[:, 512:640], kvg_ref[...])
        c_ref[rs, :] = c.astype(BF16)
        ct_ref[r0 // KC] = jnp.concatenate(
            [c.T, jnp.ones((1, KC), F32), jnp.zeros((CT_ROWS - KV_LATENT - 1, KC), F32)],
            axis=0).astype(BF16)
        iqt_ref[:, rs] = p[:, 640:896].T.astype(BF16)
        tail = p[:, 1408:1536]
        ik_ref[rs, :] = tail[:, 0:IDX_DIM].astype(BF16)
        iwt_ref[:, rs] = tail.T[IDX_DIM:IDX_DIM + IDX_HEADS, :] * IDX_SCALE
        pv = p[:, 896:1408]
        ext_ref[N_META + r0:N_META + r0 + KC, :] = pv
        for g, w in enumerate(POOL_WINDOWS):
            cols = slice(g * POOL_GROUP, (g + 1) * POOL_GROUP)
            acc = pv[:, cols]
            for k in range(1, w):
                acc = acc + ext_ref[N_META + r0 - k:N_META + r0 - k + KC, cols]
            d = acc * (1.0 / w) - pv[:, cols]
            y = _dot(d.astype(BF16), wpool_ref[g]) * pscale_ref[:, cols]
            yp_ref[rs, cols] = y.astype(BF16)
    ext_ref[0:N_META, :] = ext_ref[tm:tm + N_META, :]


def _attn_kernel(ik_ref, ikm_ref, c_ref, cm_ref, ct_ref, cmt_ref, iqt_ref, iwt_ref, qt_ref,
                 wuk_ref, wuvt_ref, o_ref,
                 s_scr, qat_scr, lg_scr, m_scr, acc_scr, ot_scr, *, k_top):
    n_real = c_ref.shape[0]
    qb = iqt_ref.shape[1]
    j = pl.program_id(1)
    nkc = j + 1
    qchunk = (j * qb + lax.broadcasted_iota(jnp.int32, (1, qb), 1)) // CHUNK
    meta_rows = pl.ds(n_real, N_META)

    def rows(kc):
        return pl.ds(pl.multiple_of(kc * KC, KC), KC)

    def fold(x, op):
        groups = x.shape[0] // SUBLANES
        chains = 4 if groups % 4 == 0 else 1
        x = x.reshape(groups // chains, chains, SUBLANES, qb)
        return op(op(x, axis=0), axis=0)

    iqt_heads = [iqt_ref[h * IDX_DIM:(h + 1) * IDX_DIM, :] for h in range(IDX_HEADS)]

    def scores(ik_rows):
        acc = None
        for h in range(IDX_HEADS):
            t = jnp.maximum(_dot(ik_rows, iqt_heads[h]), 0.0) * iwt_ref[h:h + 1, :]
            acc = t if acc is None else acc + t
        return acc

    sm = scores(ikm_ref[...])
    s_scr[meta_rows, :] = sm
    mn0 = jnp.min(sm, axis=0, keepdims=True)
    mx0 = jnp.max(sm, axis=0, keepdims=True)

    def score_body(kc, carry):
        mn, mx = carry
        sc = scores(ik_ref[rows(kc), :])
        s_scr[rows(kc), :] = sc
        return jnp.minimum(mn, fold(sc, jnp.min)), jnp.maximum(mx, fold(sc, jnp.max))

    mn8, mx8 = lax.fori_loop(0, j, score_body, (jnp.full((SUBLANES, qb), POS, F32),
                                                jnp.full((SUBLANES, qb), NEG, F32)))
    sc = scores(ik_ref[rows(j), :])
    adm = (j * KC + lax.broadcasted_iota(jnp.int32, (KC, 1), 0)) // CHUNK <= qchunk
    s_scr[rows(j), :] = jnp.where(adm, sc, NEG)
    mn8 = jnp.minimum(mn8, fold(jnp.where(adm, sc, POS), jnp.min))
    mx8 = jnp.maximum(mx8, fold(jnp.where(adm, sc, NEG), jnp.max))
    mn = jnp.minimum(mn0, jnp.min(mn8, axis=0, keepdims=True))
    mx = jnp.maximum(mx0, jnp.max(mx8, axis=0, keepdims=True))

    def count_ge(th):
        def body(kc, cnt):
            return cnt + fold(jnp.where(s_scr[rows(kc), :] >= th, 1.0, 0.0), jnp.sum)
        cnt = lax.fori_loop(0, nkc, body,
                            fold(jnp.where(s_scr[meta_rows, :] >= th, 1.0, 0.0), jnp.sum))
        return jnp.sum(cnt, axis=0, keepdims=True)

    def band_extent(lo, hi):
        def ext(sv):
            return (fold(jnp.where(sv >= lo, sv, POS), jnp.min),
                    fold(jnp.where(sv < hi, sv, NEG), jnp.max))

        def body(kc, carry):
            bmin, bmax = ext(s_scr[rows(kc), :])
            return jnp.minimum(carry[0], bmin), jnp.maximum(carry[1], bmax)
        bmin, bmax = lax.fori_loop(0, nkc, body, ext(s_scr[meta_rows, :]))
        return jnp.min(bmin, axis=0, keepdims=True), jnp.max(bmax, axis=0, keepdims=True)

    def any_lane(flags):
        return jnp.max(flags)

    def bisect_cond(carry):
        it, _, _, _, pending = carry
        return jnp.logical_and(it < MAX_BISECT, pending > 0)

    def bisect_body(carry):
        it, lo, hi, cnt_lo, _ = carry
        for _ in range(BISECT_PER_CHECK):
            mid = lo + (hi - lo) * 0.5
            cnt = count_ge(mid)
            ge = cnt >= k_top
            lo = jnp.where(ge, mid, lo)
            hi = jnp.where(ge, hi, mid)
            cnt_lo = jnp.where(ge, cnt, cnt_lo)
        it = it + BISECT_PER_CHECK
        over = jnp.where(cnt_lo > k_top, 1, 0)
        pending = any_lane(over)

        def tied_check():
            bmin, bmax = band_extent(lo, hi)
            return any_lane(jnp.where(bmin < bmax, over, 0))

        pending = lax.cond(jnp.logical_and(pending > 0, it >= FIRST_TIE_CHECK),
                           tied_check, lambda: pending)
        return it, lo, hi, cnt_lo, pending

    n_adm = (N_META + CHUNK * (qchunk + 1)).astype(F32)
    hi0 = mx + (mx - mn) + (jnp.abs(mx) * (2.0 ** -10) + 1e-30)
    _, lo, hi, cnt_lo, _ = lax.while_loop(
        bisect_cond, bisect_body,
        (jnp.int32(0), mn, hi0, n_adm, any_lane(jnp.where(n_adm > k_top, 1, 0))))
    pending = any_lane(jnp.where(cnt_lo > k_top, 1, 0))

    @pl.when(pending == 0)
    def _():
        def body(kc, _):
            s_scr[rows(kc), :] = jnp.where(s_scr[rows(kc), :] >= lo, 0.0, NEG)
            return 0
        lax.fori_loop(0, nkc, body, 0)
        s_scr[meta_rows, :] = jnp.where(s_scr[meta_rows, :] >= lo, 0.0, NEG)

    @pl.when(pending > 0)
    def _():
        need = k_top - count_ge(hi)

        def pick(sv, tri, before):
            band = jnp.where(sv >= lo, jnp.where(sv < hi, 1.0, 0.0), 0.0)
            rank = _dot(tri, band.astype(BF16)) + before
            take = jnp.where(rank <= need, band, 0.0)
            bias = jnp.where(sv >= hi, 0.0, jnp.where(take > 0.5, 0.0, NEG))
            return bias, before + jnp.sum(fold(band, jnp.sum), axis=0, keepdims=True)

        def tri(n):
            return jnp.where(lax.broadcasted_iota(jnp.int32, (n, n), 0)
                             >= lax.broadcasted_iota(jnp.int32, (n, n), 1), 1.0, 0.0).astype(BF16)

        bias_m, before = pick(s_scr[meta_rows, :], tri(N_META), jnp.zeros((1, qb), F32))
        s_scr[meta_rows, :] = bias_m
        tri_kc = tri(KC)

        def body(kc, before):
            bias, before = pick(s_scr[rows(kc), :], tri_kc, before)
            s_scr[rows(kc), :] = bias
            return before
        lax.fori_loop(0, nkc, body, before)

    def head(h):
        return slice(h * qb, (h + 1) * qb)

    for h in range(N_HEADS):
        qat_scr[:, head(h)] = (_dot(wuk_ref[h], qt_ref[h * HEAD_DIM:(h + 1) * HEAD_DIM, :])
                               * (ATTN_SCALE * LOG2E)).astype(BF16)
    m_scr[...] = jnp.full(m_scr.shape, 0.5 * NEG, F32)
    acc_scr[...] = jnp.zeros(acc_scr.shape, F32)

    def attend(c_rows, ct_cols, bias):
        r = c_rows.shape[0]
        lg_scr[0:r, :] = _dot(c_rows, qat_scr[...])
        for h in range(N_HEADS):
            lg = lg_scr[0:r, head(h)] + bias
            m_old = m_scr[:, head(h)]
            m_new = jnp.maximum(m_old, jnp.max(lg, axis=0, keepdims=True))
            p = jnp.exp2(lg - m_new).astype(BF16)
            m_scr[:, head(h)] = m_new
            acc_scr[:, head(h)] = (acc_scr[:, head(h)] * jnp.exp2(m_old - m_new)
                                   + _dot(ct_cols, p))

    attend(cm_ref[...], cmt_ref[...], s_scr[meta_rows, :])

    def attend_chunk(kc):
        attend(c_ref[rows(kc), :], ct_ref[kc], s_scr[rows(kc), :])

    def attend_pair(i, _):
        attend_chunk(2 * i)
        attend_chunk(2 * i + 1)
        return 0
    lax.fori_loop(0, nkc // 2, attend_pair, 0)

    @pl.when(nkc % 2 == 1)
    def _():
        attend_chunk(j)

    for h in range(N_HEADS):
        olat = acc_scr[0:KV_LATENT, head(h)] / acc_scr[KV_LATENT:KV_LATENT + 1, head(h)]
        ot_scr[h * HEAD_DIM:(h + 1) * HEAD_DIM, :] = _dot(wuvt_ref[h], olat.astype(BF16))
    o_ref[...] = ot_scr[...].T.astype(BF16)


def _mix_kernel(x_ref, g1_ref, wgate_ref, attn_ref, yp_ref, wba_ref, wbp_ref, wout_ref,
                g2_ref, wr_ref, wrhi_ref, br_ref, h1_ref, u2_ref, route_ref, count_ref,
                merged_scr):
    nsub = D_MODEL // LANES
    counts = None
    for r0 in range(0, x_ref.shape[0], MIX_ROWS):
        rows = pl.ds(r0, MIX_ROWS)
        c = _mix_group(x_ref.at[rows], g1_ref, wgate_ref, attn_ref.at[rows], yp_ref.at[rows],
                       wba_ref, wbp_ref, wout_ref, g2_ref, wr_ref, wrhi_ref, br_ref,
                       h1_ref.at[rows], u2_ref.at[pl.ds(r0 * nsub, MIX_ROWS * nsub)],
                       route_ref.at[:, rows], merged_scr.at[rows])
        counts = c if counts is None else counts + c
    count_ref[...] = jnp.broadcast_to(counts, count_ref.shape)


def _mix_group(x_ref, g1_ref, wgate_ref, attn_ref, yp_ref, wba_ref, wbp_ref, wout_ref,
               g2_ref, wr_ref, wrhi_ref, br_ref, h1_ref, u2_ref, route_ref, merged_scr):
    x = x_ref[...]
    u = _rms(x, g1_ref[...]).astype(BF16)
    attn = attn_ref[...]
    yp = yp_ref[...]
    for cb in range(D_MODEL // MIX_COLS):
        ca = slice(cb * MIX_COLS, (cb + 1) * MIX_COLS)
        cp = slice(D_MODEL + cb * MIX_COLS, D_MODEL + (cb + 1) * MIX_COLS)
        g_attn = 1.0 / (1.0 + jnp.exp(-_dot(u, wgate_ref[:, ca])))
        g_pool = 1.0 / (1.0 + jnp.exp(-_dot(u, wgate_ref[:, cp])))
        merged_scr[:, ca] = (g_attn * _dot(attn, wba_ref[:, ca])
                             + g_pool * _dot(yp, wbp_ref[:, ca])).astype(BF16)
    h1 = x + _dot(merged_scr[...], wout_ref[...])
    h1_ref[...] = h1
    u2 = _rms(h1, g2_ref[...])
    nsub = D_MODEL // LANES
    for s in range(nsub):
        u2_ref[pl.ds(s, x.shape[0], stride=nsub), :] = u2[:, s * LANES:(s + 1) * LANES]
    u2_hi = u2.astype(BF16)

    u2_lo = (u2 - u2_hi.astype(F32)).astype(BF16)
    hi_both = _dot(u2_hi, wr_ref[...])
    lg = hi_both[:, :LANES] + hi_both[:, LANES:] + _dot(u2_lo, wrhi_ref[...]) + br_ref[...]
    lane = lax.broadcasted_iota(jnp.int32, lg.shape, 1)
    is_g = lane < N_GROUPS
    gl = jnp.where(is_g, lg, NEG)
    gmax = jnp.max(gl, axis=1, keepdims=True)
    gidx = jnp.min(jnp.where(gl == gmax, lane, LANES), axis=1, keepdims=True)
    p_g = 1.0 / jnp.sum(jnp.where(is_g, jnp.exp(gl - gmax), 0.0), axis=1, keepdims=True)
    e_lane = lane - N_GROUPS
    lane_grp = jnp.where(e_lane >= 0,
                         jnp.where(e_lane < N_EXPERTS, e_lane // EXPERTS_PER_GROUP, -1), -1)
    in_grp = lane_grp == gidx
    el = jnp.where(in_grp, lg, NEG)
    t1 = jnp.max(el, axis=1, keepdims=True)
    i1 = jnp.min(jnp.where(el == t1, lane, LANES), axis=1, keepdims=True)
    el2 = jnp.where(lane == i1, NEG, el)
    t2 = jnp.max(el2, axis=1, keepdims=True)
    i2 = jnp.min(jnp.where(el2 == t2, lane, LANES), axis=1, keepdims=True)
    r = jnp.exp(t2 - t1)
    p1 = 1.0 / (1.0 + r)
    p2 = r * p1
    e1 = (i1 - N_GROUPS).astype(F32)
    e2 = (i2 - N_GROUPS).astype(F32)
    record = jnp.where(lane == 0, e1, jnp.where(lane == 1, e2, jnp.where(
        lane == 2, p1 * p_g, jnp.where(lane == 3, p2 * p_g, 0.0))))
    route_ref[...] = record.T[0:SUBLANES, :]
    chosen = jnp.where(lane == i1, 1.0, jnp.where(lane == i2, 1.0, 0.0))
    return jnp.sum(chosen, axis=0, keepdims=True)


def _moe_kernel(start_ref, cnt_ref, tok_ref, slot_ref, wt_ref, x2_ref, wg_hbm, wu_hbm, wd_hbm,
                y2_ref, r2_scr, ga_scr, gb_scr, rt_scr, wg_buf, wu_buf, wd_buf, w_sem):
    ts = x2_ref.shape[0] // SUBLANES
    tmx = ga_scr.shape[0] // SUBLANES
    st = pl.program_id(0)
    e = pl.program_id(1)
    n_e = pl.num_programs(1)
    nsub = D_MODEL // LANES
    step = st * n_e + e
    n_steps = pl.num_programs(0) * n_e

    def weight_copies(g):
        ex = g % n_e
        buf = g % MOE_WEIGHT_BUFFERS
        return [pltpu.make_async_copy(hbm.at[ex], vbuf.at[buf], w_sem.at[k, buf])
                for k, (hbm, vbuf) in enumerate(((wg_hbm, wg_buf), (wu_hbm, wu_buf),
                                                 (wd_hbm, wd_buf)))]

    @pl.when(step == 0)
    def _():
        for g in range(MOE_WEIGHT_BUFFERS - 1):
            for cp in weight_copies(g):
                cp.start()

    @pl.when(step + MOE_WEIGHT_BUFFERS - 1 < n_steps)
    def _():
        for cp in weight_copies(step + MOE_WEIGHT_BUFFERS - 1):
            cp.start()

    for cp in weight_copies(step):
        cp.wait()
    wbuf = step % MOE_WEIGHT_BUFFERS

    def vreg_rows(i):
        return pl.ds(pl.multiple_of(i * SUBLANES, SUBLANES), SUBLANES)

    def gather(g_scr, base):
        def gather_body(r8, _):
            for u in range(SUBLANES):
                r = r8 * SUBLANES + u
                g_scr[vreg_rows(r), :] = x2_ref[vreg_rows(tok_ref[base + r]), :]
            return 0
        lax.fori_loop(0, tmx // SUBLANES, gather_body, 0)

    def expert_ffn(g_scr, base):
        xg = jnp.concatenate([g_scr[pl.ds(s, tmx, stride=nsub), :] for s in range(nsub)],
                             axis=1).astype(BF16)
        a = _dot(xg, wg_buf[wbuf])
        b = _dot(xg, wu_buf[wbuf])
        hg = (a * (1.0 / (1.0 + jnp.exp(-a))) * b).astype(BF16)
        yr = _dot(hg, wd_buf[wbuf])
        for s in range(nsub):
            rt_scr[pl.ds(s, tmx, stride=nsub), :] = yr[:, s * LANES:(s + 1) * LANES]
        r2_scr[pl.ds(pl.multiple_of(base * SUBLANES, SUBLANES), tmx * SUBLANES), :] = rt_scr[...]

    start = start_ref[step]
    cnt = cnt_ref[step]

    @pl.when(e == 0)
    def _():
        gather(ga_scr, start)

    def run(cur_scr, nxt_scr):
        expert_ffn(cur_scr, start)
        if nxt_scr is not None:
            nxt = start_ref[step + 1]
            for r in range(tmx):
                nxt_scr[r * SUBLANES:(r + 1) * SUBLANES, :] = (
                    x2_ref[vreg_rows(tok_ref[nxt + r]), :])

        def extra_tile(i, _):
            gather(cur_scr, start + i * tmx)
            expert_ffn(cur_scr, start + i * tmx)
            return 0
        lax.fori_loop(1, (cnt + tmx - 1) // tmx, extra_tile, 0)

    last = e == n_e - 1
    even = e % 2 == 0

    @pl.when(jnp.logical_and(even, jnp.logical_not(last)))
    def _():
        run(ga_scr, gb_scr)

    @pl.when(jnp.logical_and(jnp.logical_not(even), jnp.logical_not(last)))
    def _():
        run(gb_scr, ga_scr)

    @pl.when(last)
    def _():
        run(ga_scr if (N_EXPERTS - 1) % 2 == 0 else gb_scr, None)

        def combine_body(t8, _):
            for u in range(SUBLANES):
                t = t8 * SUBLANES + u
                y2_ref[vreg_rows(t), :] = (
                    wt_ref[t] * r2_scr[vreg_rows(slot_ref[t]), :]
                    + wt_ref[ts + t] * r2_scr[vreg_rows(slot_ref[ts + t]), :])
            return 0
        lax.fori_loop(0, ts // SUBLANES, combine_body, 0)


def _final_kernel(h1_ref, y2_ref, gf_ref, o_ref):
    nsub = D_MODEL // LANES
    y = jnp.concatenate([y2_ref[pl.ds(s, h1_ref.shape[0], stride=nsub), :] for s in range(nsub)],
                        axis=1)
    o_ref[...] = _rms(h1_ref[...] + y, gf_ref[...])


def _const_spec(shape):
    nd = len(shape)
    return pl.BlockSpec(shape, lambda *_: (0,) * nd, pipeline_mode=pl.Buffered(1))


def _params(n_axes):
    return pltpu.CompilerParams(dimension_semantics=("arbitrary",) * n_axes,
                                vmem_limit_bytes=VMEM_LIMIT)


def kernel(x, meta_tokens, norm1_g, w_in, kv_norm_g, w_uk, w_uv, w_pool, pool_scale,
           w_branch_attn, w_branch_pool, w_out, norm2_g, w_group_router, b_group_router,
           w_expert_router, b_expert_router, w_expert_gate, w_expert_up, w_expert_down,
           final_norm_g):
    B, S, D = x.shape
    assert D == D_MODEL and S % QB == 0 and S % TM_PROJ == 0 and w_in.shape[0] == 1
    assert QB == KC and TM_PROJ % KC == 0 and QB % CHUNK == 0
    N = B * S
    k_top = min(TOPK_MAX, S // 4)
    xr = x.reshape(N, D)

    wi = w_in[0]
    w1 = jnp.concatenate(
        [wi[:, 0:640], wi[:, 640:896], wi[:, 936:1448], wi[:, 896:936],
         jnp.zeros((D, W1_WIDTH - 1448), F32)], axis=1).astype(BF16)
    wgate = wi[:, 1448:].astype(BF16)
    g1 = norm1_g[0].reshape(1, D)
    kvg = kv_norm_g[0].reshape(1, KV_LATENT)
    wpool = w_pool[0].astype(BF16)
    pscale = pool_scale[0].reshape(1, POOL_WIDTH)
    wuk = jnp.transpose(w_uk[0], (1, 0, 2)).astype(BF16)
    wuvt = jnp.transpose(w_uv[0], (1, 2, 0)).astype(BF16)
    wr = jnp.concatenate(
        [w_group_router[0], w_expert_router[0].reshape(D, N_EXPERTS),
         jnp.zeros((D, LANES - N_GROUPS - N_EXPERTS), F32)], axis=1)
    br = jnp.concatenate(
        [b_group_router[0], b_expert_router[0].reshape(N_EXPERTS),
         jnp.zeros((LANES - N_GROUPS - N_EXPERTS,), F32)]).reshape(1, LANES)

    c_m, tail_m, pv_m = pl.pallas_call(
        _meta_kernel,
        out_shape=(jax.ShapeDtypeStruct((N_META, KV_LATENT), F32),
                   jax.ShapeDtypeStruct((N_META, LANES), F32),
                   jax.ShapeDtypeStruct((N_META, POOL_WIDTH), F32)),
        name="meta",
    )(meta_tokens, g1, w1, kvg)
    cm = c_m.astype(BF16)
    cmt = jnp.concatenate([cm.T, jnp.ones((1, N_META), BF16),
                           jnp.zeros((CT_ROWS - KV_LATENT - 1, N_META), BF16)], axis=0)
    ikm = tail_m[:, :IDX_DIM].astype(BF16)

    tpb = S // TM_PROJ
    tok = lambda w: pl.BlockSpec((TM_PROJ, w), lambda b, i: (b * tpb + i, 0))
    tok_t = lambda w: pl.BlockSpec((w, TM_PROJ), lambda b, i: (0, b * tpb + i))
    qt, c, ct, iqt, ik, iwt, yp = pl.pallas_call(
        _proj_kernel,
        grid=(B, tpb),
        in_specs=[tok(D), _const_spec((1, D)), _const_spec((D, W1_WIDTH)),
                  _const_spec((1, KV_LATENT)), _const_spec((N_META, POOL_WIDTH)),
                  _const_spec((len(POOL_WINDOWS), POOL_GROUP, POOL_GROUP)),
                  _const_spec((1, POOL_WIDTH))],
        out_specs=[tok_t(ATTN_WIDTH), tok(KV_LATENT),
                   pl.BlockSpec((None, TM_PROJ // KC, CT_ROWS, KC), lambda b, i: (b, i, 0, 0)),
                   tok_t(IDX_HEADS * IDX_DIM), tok(IDX_DIM), tok_t(IDX_HEADS), tok(POOL_WIDTH)],
        out_shape=(jax.ShapeDtypeStruct((ATTN_WIDTH, N), BF16),
                   jax.ShapeDtypeStruct((N, KV_LATENT), BF16),
                   jax.ShapeDtypeStruct((B, S // KC, CT_ROWS, KC), BF16),
                   jax.ShapeDtypeStruct((IDX_HEADS * IDX_DIM, N), BF16),
                   jax.ShapeDtypeStruct((N, IDX_DIM), BF16),
                   jax.ShapeDtypeStruct((IDX_HEADS, N), F32),
                   jax.ShapeDtypeStruct((N, POOL_WIDTH), BF16)),
        scratch_shapes=[pltpu.VMEM((TM_PROJ + N_META, POOL_WIDTH), F32)],
        compiler_params=_params(2),
        name="proj",
    )(xr, g1, w1, kvg, pv_m, wpool, pscale)

    nqb = S // QB
    qcol = lambda w: pl.BlockSpec((w, QB), lambda b, j: (0, b * nqb + j))
    attn = pl.pallas_call(
        functools.partial(_attn_kernel, k_top=float(k_top)),
        grid=(B, nqb),
        in_specs=[pl.BlockSpec((S, IDX_DIM), lambda b, j: (b, 0)),
                  _const_spec((N_META, IDX_DIM)),
                  pl.BlockSpec((S, KV_LATENT), lambda b, j: (b, 0)),
                  _const_spec((N_META, KV_LATENT)),
                  pl.BlockSpec((None, S // KC, CT_ROWS, KC), lambda b, j: (b, 0, 0, 0)),
                  _const_spec((CT_ROWS, N_META)),
                  qcol(IDX_HEADS * IDX_DIM), qcol(IDX_HEADS), qcol(ATTN_WIDTH),
                  _const_spec((N_HEADS, KV_LATENT, HEAD_DIM)),
                  _const_spec((N_HEADS, HEAD_DIM, KV_LATENT))],
        out_specs=pl.BlockSpec((QB, ATTN_WIDTH), lambda b, j: (b * nqb + j, 0)),
        out_shape=jax.ShapeDtypeStruct((N, ATTN_WIDTH), BF16),
        scratch_shapes=[pltpu.VMEM((S + N_META, QB), F32),
                        pltpu.VMEM((KV_LATENT, N_HEADS * QB), BF16),
                        pltpu.VMEM((KC, N_HEADS * QB), F32),
                        pltpu.VMEM((1, N_HEADS * QB), F32),
                        pltpu.VMEM((CT_ROWS, N_HEADS * QB), F32),
                        pltpu.VMEM((ATTN_WIDTH, QB), F32)],
        compiler_params=_params(2),
        name="attn",
    )(ik, ikm, c, cm, ct, cmt, iqt, iwt, qt, wuk, wuvt)

    wr_hi = wr.astype(BF16)
    wr_lo = (wr - wr_hi.astype(F32)).astype(BF16)
    tok1 = lambda w: pl.BlockSpec((TM_MIX, w), lambda i: (i, 0))
    h1, u2, route, tile_counts = pl.pallas_call(
        _mix_kernel,
        grid=(N // TM_MIX,),
        in_specs=[tok1(D), _const_spec((1, D)), _const_spec((D, 2 * D)), tok1(ATTN_WIDTH),
                  tok1(POOL_WIDTH), _const_spec((ATTN_WIDTH, D)), _const_spec((POOL_WIDTH, D)),
                  _const_spec((D, D)), _const_spec((1, D)), _const_spec((D, 2 * LANES)),
                  _const_spec((D, LANES)), _const_spec((1, LANES))],
        out_specs=[tok1(D), pl.BlockSpec((TM_MIX * SUBLANES, LANES), lambda i: (i, 0)),
                   pl.BlockSpec((SUBLANES, TM_MIX), lambda i: (0, i)),
                   pl.BlockSpec((SUBLANES, LANES), lambda i: (i, 0))],
        out_shape=(jax.ShapeDtypeStruct((N, D), F32),
                   jax.ShapeDtypeStruct((N * SUBLANES, LANES), F32),
                   jax.ShapeDtypeStruct((SUBLANES, N), F32),
                   jax.ShapeDtypeStruct((N // TM_MIX * SUBLANES, LANES), F32)),
        scratch_shapes=[pltpu.VMEM((TM_MIX, D), BF16)],
        compiler_params=_params(1),
        name="mix",
    )(xr, g1, wgate, attn, yp, w_branch_attn[0].astype(BF16), w_branch_pool[0].astype(BF16),
      w_out[0].astype(BF16), norm2_g[0].reshape(1, D),
      jnp.concatenate([wr_hi, wr_lo], axis=1), wr_hi, br)

    nst = N // TS_MOE
    n_asg = 2 * TS_MOE
    per_tile = lambda a: a.reshape(2, nst, TS_MOE).transpose(1, 0, 2).reshape(nst, n_asg)
    eid = per_tile(route[0:2].astype(jnp.int32))
    wts = per_tile(route[2:4]).reshape(nst * n_asg)
    order = jnp.argsort(eid, axis=1, stable=True).astype(jnp.int32)
    slot = jnp.argsort(order, axis=1).astype(jnp.int32).reshape(nst * n_asg)
    tok_sorted = jnp.pad(order % TS_MOE, ((0, 0), (0, LIST_PAD))).reshape(
        nst * (n_asg + LIST_PAD))
    counts = tile_counts.reshape(nst, TS_MOE // TM_MIX, SUBLANES, LANES)[
        :, :, 0, N_GROUPS:N_GROUPS + N_EXPERTS].sum(axis=1).astype(jnp.int32)
    starts = (jnp.cumsum(counts, axis=1) - counts).reshape(nst * N_EXPERTS)
    counts = counts.reshape(nst * N_EXPERTS)

    weg = w_expert_gate[0].reshape(N_EXPERTS, D, EXPERT_HIDDEN).astype(BF16)
    weu = w_expert_up[0].reshape(N_EXPERTS, D, EXPERT_HIDDEN).astype(BF16)
    wed = w_expert_down[0].reshape(N_EXPERTS, EXPERT_HIDDEN, D).astype(BF16)
    smem = lambda n: pl.BlockSpec((n,), lambda s, e, *_: (s,), memory_space=pltpu.SMEM)
    y = pl.pallas_call(
        _moe_kernel,
        grid_spec=pltpu.PrefetchScalarGridSpec(
            num_scalar_prefetch=2,
            grid=(nst, N_EXPERTS),
            in_specs=[smem(n_asg + LIST_PAD), smem(n_asg), smem(n_asg),
                      pl.BlockSpec((TS_MOE * SUBLANES, LANES), lambda s, e, *_: (s, 0)),
                      pl.BlockSpec(memory_space=pl.ANY), pl.BlockSpec(memory_space=pl.ANY),
                      pl.BlockSpec(memory_space=pl.ANY)],
            out_specs=pl.BlockSpec((TS_MOE * SUBLANES, LANES), lambda s, e, *_: (s, 0)),
            scratch_shapes=[pltpu.VMEM(((n_asg + TMX_MOE) * SUBLANES, LANES), F32),
                            pltpu.VMEM((TMX_MOE * SUBLANES, LANES), F32),
                            pltpu.VMEM((TMX_MOE * SUBLANES, LANES), F32),
                            pltpu.VMEM((TMX_MOE * SUBLANES, LANES), F32),
                            pltpu.VMEM((MOE_WEIGHT_BUFFERS, D, EXPERT_HIDDEN), BF16),
                            pltpu.VMEM((MOE_WEIGHT_BUFFERS, D, EXPERT_HIDDEN), BF16),
                            pltpu.VMEM((MOE_WEIGHT_BUFFERS, EXPERT_HIDDEN, D), BF16),
                            pltpu.SemaphoreType.DMA((3, MOE_WEIGHT_BUFFERS))]),
        out_shape=jax.ShapeDtypeStruct((N * SUBLANES, LANES), F32),
        compiler_params=_params(2),
        name="moe",
    )(starts, counts, tok_sorted, slot, wts, u2, weg, weu, wed)

    ftok = pl.BlockSpec((TM_FINAL, D), lambda i: (i, 0))
    out = pl.pallas_call(
        _final_kernel,
        grid=(N // TM_FINAL,),
        in_specs=[ftok, pl.BlockSpec((TM_FINAL * SUBLANES, LANES), lambda i: (i, 0)),
                  _const_spec((1, D))],
        out_specs=ftok,
        out_shape=jax.ShapeDtypeStruct((N, D), F32),
        compiler_params=_params(1),
        name="final",
    )(h1, y, final_norm_g.reshape(1, D))
    return out.reshape(B, S, D)
```

```python
import functools

import jax
import jax.numpy as jnp
from jax import lax
from jax.experimental import pallas as pl
from jax.experimental.pallas import tpu as pltpu

F32 = jnp.float32
BF16 = jnp.bfloat16

D_MODEL = 1024
CHUNK = 64
N_META = 16
N_HEADS = 8
HEAD_DIM = 64
ATTN_WIDTH = N_HEADS * HEAD_DIM
KV_LATENT = 128
IDX_HEADS = 8
IDX_DIM = 32
TOPK_MAX = 256
ATTN_SCALE = HEAD_DIM ** -0.5
IDX_SCALE = (IDX_HEADS ** -0.5) * (IDX_DIM ** -0.5)
POOL_WINDOWS = (2, 4, 8, 16)
POOL_WIDTH = 512
POOL_GROUP = 128
N_GROUPS = 4
EXPERTS_PER_GROUP = 8
N_EXPERTS = N_GROUPS * EXPERTS_PER_GROUP
EXPERT_HIDDEN = 256
EPS = 1e-6

LANES = 128
SUBLANES = 8
W1_WIDTH = 1536
NEG = -1e30
POS = 1e30
VMEM_LIMIT = 56 * 1024 * 1024

TM_PROJ = 1024
TM_MIX = 1024
MIX_ROWS = 256
MIX_COLS = 256
QB = 256
KC = 256
MAX_BISECT = 40
BISECT_PER_CHECK = 4
FIRST_TIE_CHECK = 8
CT_ROWS = KV_LATENT + 16
LOG2E = 1.4426950408889634
TS_MOE = 2048
TMX_MOE = 160
MOE_WEIGHT_BUFFERS = 3
LIST_PAD = 1024
TM_FINAL = 1024


def _rms(x, g):
    return x * lax.rsqrt(jnp.mean(x * x, axis=-1, keepdims=True) + EPS) * g


def _dot(a, b):
    return jnp.dot(a, b, preferred_element_type=F32)


def _meta_kernel(m_ref, g1_ref, w1_ref, kvg_ref, c_ref, tail_ref, pv_ref):
    u = _rms(m_ref[...], g1_ref[...]).astype(BF16)
    p = _dot(u, w1_ref[...])
    c_ref[...] = _rms(p[:, 512:640], kvg_ref[...])
    pv_ref[...] = p[:, 896:1408]
    tail_ref[...] = p[:, 1408:1536]


def _proj_kernel(x_ref, g1_ref, w1_ref, kvg_ref, pvmeta_ref, wpool_ref, pscale_ref,
                 qt_ref, c_ref, ct_ref, iqt_ref, ik_ref, iwt_ref, yp_ref, ext_ref):
    tm = x_ref.shape[0]

    @pl.when(pl.program_id(1) == 0)
    def _():
        ext_ref[0:N_META, :] = pvmeta_ref[...]

    for r0 in range(0, tm, KC):
        rs = slice(r0, r0 + KC)
        u = _rms(x_ref[rs, :], g1_ref[...]).astype(BF16)
        p = _dot(u, w1_ref[...])
        qt_ref[:, rs] = p[:, 0:512].T.astype(BF16)
        c = _rms(p[:, 512:640], kvg_ref[...])
        c_ref[rs, :] = c.astype(BF16)
        ct_ref[r0 // KC] = jnp.concatenate(
            [c.T, jnp.ones((1, KC), F32), jnp.zeros((CT_ROWS - KV_LATENT - 1, KC), F32)],
            axis=0).astype(BF16)
        iqt_ref[:, rs] = p[:, 640:896].T.astype(BF16)
        tail = p[:, 1408:1536]
        ik_ref[rs, :] = tail[:, 0:IDX_DIM].astype(BF16)
        iwt_ref[:, rs] = tail.T[IDX_DIM:IDX_DIM + IDX_HEADS, :] * IDX_SCALE
        pv = p[:, 896:1408]
        ext_ref[N_META + r0:N_META + r0 + KC, :] = pv
        for g, w in enumerate(POOL_WINDOWS):
            cols = slice(g * POOL_GROUP, (g + 1) * POOL_GROUP)
            acc = pv[:, cols]
            for k in range(1, w):
                acc = acc + ext_ref[N_META + r0 - k:N_META + r0 - k + KC, cols]
            d = acc * (1.0 / w) - pv[:, cols]
            y = _dot(d.astype(BF16), wpool_ref[g]) * pscale_ref[:, cols]
            yp_ref[rs, cols] = y.astype(BF16)
    ext_ref[0:N_META, :] = ext_ref[tm:tm + N_META, :]


def _attn_kernel(ik_ref, ikm_ref, c_ref, cm_ref, ct_ref, cmt_ref, iqt_ref, iwt_ref, qt_ref,
                 wuk_ref, wuvt_ref, o_ref,
                 s_scr, s16_scr, qat_scr, lg_scr, m_scr, acc_scr, ot_scr, *, k_top):
    n_real = c_ref.shape[0]
    qb = iqt_ref.shape[1]
    j = pl.program_id(1)
    nkc = j + 1
    qchunk = (j * qb + lax.broadcasted_iota(jnp.int32, (1, qb), 1)) // CHUNK
    meta_rows = pl.ds(n_real, N_META)

    def rows(kc):
        return pl.ds(pl.multiple_of(kc * KC, KC), KC)

    def fold(x, op):
        groups = x.shape[0] // SUBLANES
        chains = 4 if groups % 4 == 0 else 1
        x = x.reshape(groups // chains, chains, SUBLANES, qb)
        return op(op(x, axis=0), axis=0)

    iqt_heads = [iqt_ref[h * IDX_DIM:(h + 1) * IDX_DIM, :] for h in range(IDX_HEADS)]

    def scores(ik_rows):
        acc = None
        for h in range(IDX_HEADS):
            t = jnp.maximum(_dot(ik_rows, iqt_heads[h]), 0.0) * iwt_ref[h:h + 1, :]
            acc = t if acc is None else acc + t
        return acc

    sm = scores(ikm_ref[...])
    s16_scr[meta_rows, :] = sm.astype(BF16)
    mn0 = jnp.min(sm, axis=0, keepdims=True)
    mx0 = jnp.max(sm, axis=0, keepdims=True)

    def score_body(kc, carry):
        mn, mx = carry
        sc = scores(ik_ref[rows(kc), :])
        s16_scr[rows(kc), :] = sc.astype(BF16)
        return jnp.minimum(mn, fold(sc, jnp.min)), jnp.maximum(mx, fold(sc, jnp.max))

    mn8, mx8 = lax.fori_loop(0, j, score_body, (jnp.full((SUBLANES, qb), POS, F32),
                                                jnp.full((SUBLANES, qb), NEG, F32)))
    sc = scores(ik_ref[rows(j), :])
    adm = (j * KC + lax.broadcasted_iota(jnp.int32, (KC, 1), 0)) // CHUNK <= qchunk
    s16_scr[rows(j), :] = jnp.where(adm, sc, NEG).astype(BF16)
    mn8 = jnp.minimum(mn8, fold(jnp.where(adm, sc, POS), jnp.min))
    mx8 = jnp.maximum(mx8, fold(jnp.where(adm, sc, NEG), jnp.max))
    mn = jnp.minimum(mn0, jnp.min(mn8, axis=0, keepdims=True))
    mx = jnp.maximum(mx0, jnp.max(mx8, axis=0, keepdims=True))

    pack = 2 * SUBLANES

    def as16(v):
        return jnp.broadcast_to(v, (pack, qb)).astype(BF16)

    def tree(parts, op):
        while len(parts) > 1:
            parts = ([op(parts[i], parts[i + 1]) for i in range(0, len(parts) - 1, 2)]
                     + ([parts[-1]] if len(parts) % 2 else []))
        return parts[0]

    def groups(x):
        return [x[i * pack:(i + 1) * pack, :] for i in range(x.shape[0] // pack)]

    one16 = jnp.ones((pack, qb), BF16)
    zero16 = jnp.zeros((pack, qb), BF16)

    def count_ge(th):
        th16 = as16(th)

        def ones(x):
            return tree([jnp.where(g >= th16, one16, zero16) for g in groups(x)], jnp.add)

        def body(kc, cnt):
            return cnt + ones(s16_scr[rows(kc), :])
        cnt = lax.fori_loop(0, nkc, body, ones(s16_scr[meta_rows, :]))
        return jnp.sum(cnt.astype(F32), axis=0, keepdims=True)

    def band_extent(lo, hi):
        lo16, hi16 = as16(lo), as16(hi)
        pos16 = jnp.full((pack, qb), POS, BF16)
        neg16 = jnp.full((pack, qb), NEG, BF16)

        def ext(x):
            return (tree([jnp.where(g >= lo16, g, pos16) for g in groups(x)], jnp.minimum),
                    tree([jnp.where(g < hi16, g, neg16) for g in groups(x)], jnp.maximum))

        def body(kc, carry):
            bmin, bmax = ext(s16_scr[rows(kc), :])
            return jnp.minimum(carry[0], bmin), jnp.maximum(carry[1], bmax)
        bmin, bmax = lax.fori_loop(0, nkc, body, ext(s16_scr[meta_rows, :]))
        return (jnp.min(bmin.astype(F32), axis=0, keepdims=True),
                jnp.max(bmax.astype(F32), axis=0, keepdims=True))

    def any_lane(flags):
        return jnp.max(flags)

    def bisect_cond(carry):
        it, _, _, _, pending = carry
        return jnp.logical_and(it < MAX_BISECT, pending > 0)

    def bisect_body(carry):
        it, lo, hi, cnt_lo, _ = carry
        for _ in range(BISECT_PER_CHECK):
            mid = (lo + (hi - lo) * 0.5).astype(BF16).astype(F32)
            cnt = count_ge(mid)
            ge = cnt >= k_top
            lo = jnp.where(ge, mid, lo)
            hi = jnp.where(ge, hi, mid)
            cnt_lo = jnp.where(ge, cnt, cnt_lo)
        it = it + BISECT_PER_CHECK
        over = jnp.where(cnt_lo > k_top, 1, 0)
        pending = any_lane(over)

        def tied_check():
            bmin, bmax = band_extent(lo, hi)
            return any_lane(jnp.where(bmin < bmax, over, 0))

        pending = lax.cond(jnp.logical_and(pending > 0, it >= FIRST_TIE_CHECK),
                           tied_check, lambda: pending)
        return it, lo, hi, cnt_lo, pending

    n_adm = (N_META + CHUNK * (qchunk + 1)).astype(F32)
    lo0 = mn.astype(BF16).astype(F32)
    hi0 = (mx + jnp.abs(mx) * (2.0 ** -6) + 1e-30).astype(BF16).astype(F32)
    _, lo, hi, cnt_lo, _ = lax.while_loop(
        bisect_cond, bisect_body,
        (jnp.int32(0), lo0, hi0, n_adm, any_lane(jnp.where(n_adm > k_top, 1, 0))))
    pending = any_lane(jnp.where(cnt_lo > k_top, 1, 0))

    @pl.when(pending == 0)
    def _():
        def body(kc, _):
            s_scr[rows(kc), :] = jnp.where(s16_scr[rows(kc), :].astype(F32) >= lo, 0.0, NEG)
            return 0
        lax.fori_loop(0, nkc, body, 0)
        s_scr[meta_rows, :] = jnp.where(s16_scr[meta_rows, :].astype(F32) >= lo, 0.0, NEG)

    @pl.when(pending > 0)
    def _():
        need = k_top - count_ge(hi)

        def pick(sv, tri, before):
            band = jnp.where(sv >= lo, jnp.where(sv < hi, 1.0, 0.0), 0.0)
            rank = _dot(tri, band.astype(BF16)) + before
            take = jnp.where(rank <= need, band, 0.0)
            bias = jnp.where(sv >= hi, 0.0, jnp.where(take > 0.5, 0.0, NEG))
            return bias, before + jnp.sum(fold(band, jnp.sum), axis=0, keepdims=True)

        def tri(n):
            return jnp.where(lax.broadcasted_iota(jnp.int32, (n, n), 0)
                             >= lax.broadcasted_iota(jnp.int32, (n, n), 1), 1.0, 0.0).astype(BF16)

        bias_m, before = pick(s16_scr[meta_rows, :].astype(F32), tri(N_META),
                              jnp.zeros((1, qb), F32))
        s_scr[meta_rows, :] = bias_m
        tri_kc = tri(KC)

        def body(kc, before):
            bias, before = pick(s16_scr[rows(kc), :].astype(F32), tri_kc, before)
            s_scr[rows(kc), :] = bias
            return before
        lax.fori_loop(0, nkc, body, before)

    def head(h):
        return slice(h * qb, (h + 1) * qb)

    for h in range(N_HEADS):
        qat_scr[:, head(h)] = (_dot(wuk_ref[h], qt_ref[h * HEAD_DIM:(h + 1) * HEAD_DIM, :])
                               * (ATTN_SCALE * LOG2E)).astype(BF16)
    m_scr[...] = jnp.full(m_scr.shape, 0.5 * NEG, F32)
    acc_scr[...] = jnp.zeros(acc_scr.shape, F32)

    def attend(c_rows, ct_cols, bias):
        r = c_rows.shape[0]
        lg_scr[0:r, :] = _dot(c_rows, qat_scr[...])
        for h in range(N_HEADS):
            lg = lg_scr[0:r, head(h)] + bias
            m_old = m_scr[:, head(h)]
            m_new = jnp.maximum(m_old, jnp.max(lg, axis=0, keepdims=True))
            p = jnp.exp2(lg - m_new).astype(BF16)
            m_scr[:, head(h)] = m_new
            acc_scr[:, head(h)] = (acc_scr[:, head(h)] * jnp.exp2(m_old - m_new)
                                   + _dot(ct_cols, p))

    attend(cm_ref[...], cmt_ref[...], s_scr[meta_rows, :])

    def attend_chunk(kc):
        attend(c_ref[rows(kc), :], ct_ref[kc], s_scr[rows(kc), :])

    def attend_pair(i, _):
        attend_chunk(2 * i)
        attend_chunk(2 * i + 1)
        return 0
    lax.fori_loop(0, nkc // 2, attend_pair, 0)

    @pl.when(nkc % 2 == 1)
    def _():
        attend_chunk(j)

    for h in range(N_HEADS):
        olat = acc_scr[0:KV_LATENT, head(h)] / acc_scr[KV_LATENT:KV_LATENT + 1, head(h)]
        ot_scr[h * HEAD_DIM:(h + 1) * HEAD_DIM, :] = _dot(wuvt_ref[h], olat.astype(BF16))
    o_ref[...] = ot_scr[...].T.astype(BF16)


def _mix_kernel(x_ref, g1_ref, wgate_ref, attn_ref, yp_ref, wba_ref, wbp_ref, wout_ref,
                g2_ref, wr_ref, wrhi_ref, br_ref, h1_ref, u2_ref, route_ref, count_ref,
                merged_scr):
    nsub = D_MODEL // LANES
    counts = None
    for r0 in range(0, x_ref.shape[0], MIX_ROWS):
        rows = pl.ds(r0, MIX_ROWS)
        c = _mix_group(x_ref.at[rows], g1_ref, wgate_ref, attn_ref.at[rows], yp_ref.at[rows],
                       wba_ref, wbp_ref, wout_ref, g2_ref, wr_ref, wrhi_ref, br_ref,
                       h1_ref.at[rows], u2_ref.at[pl.ds(r0 * nsub, MIX_ROWS * nsub)],
                       route_ref.at[:, rows], merged_scr.at[rows])
        counts = c if counts is None else counts + c
    count_ref[...] = jnp.broadcast_to(counts, count_ref.shape)


def _mix_group(x_ref, g1_ref, wgate_ref, attn_ref, yp_ref, wba_ref, wbp_ref, wout_ref,
               g2_ref, wr_ref, wrhi_ref, br_ref, h1_ref, u2_ref, route_ref, merged_scr):
    x = x_ref[...]
    u = _rms(x, g1_ref[...]).astype(BF16)
    attn = attn_ref[...]
    yp = yp_ref[...]
    for cb in range(D_MODEL // MIX_COLS):
        ca = slice(cb * MIX_COLS, (cb + 1) * MIX_COLS)
        cp = slice(D_MODEL + cb * MIX_COLS, D_MODEL + (cb + 1) * MIX_COLS)
        g_attn = 1.0 / (1.0 + jnp.exp(-_dot(u, wgate_ref[:, ca])))
        g_pool = 1.0 / (1.0 + jnp.exp(-_dot(u, wgate_ref[:, cp])))
        merged_scr[:, ca] = (g_attn * _dot(attn, wba_ref[:, ca])
                             + g_pool * _dot(yp, wbp_ref[:, ca])).astype(BF16)
    h1 = x + _dot(merged_scr[...], wout_ref[...])
    h1_ref[...] = h1
    u2 = _rms(h1, g2_ref[...])
    nsub = D_MODEL // LANES
    for s in range(nsub):
        u2_ref[pl.ds(s, x.shape[0], stride=nsub), :] = u2[:, s * LANES:(s + 1) * LANES]
    u2_hi = u2.astype(BF16)

    u2_lo = (u2 - u2_hi.astype(F32)).astype(BF16)
    hi_both = _dot(u2_hi, wr_ref[...])
    lg = hi_both[:, :LANES] + hi_both[:, LANES:] + _dot(u2_lo, wrhi_ref[...]) + br_ref[...]
    lane = lax.broadcasted_iota(jnp.int32, lg.shape, 1)
    is_g = lane < N_GROUPS
    gl = jnp.where(is_g, lg, NEG)
    gmax = jnp.max(gl, axis=1, keepdims=True)
    gidx = jnp.min(jnp.where(gl == gmax, lane, LANES), axis=1, keepdims=True)
    p_g = 1.0 / jnp.sum(jnp.where(is_g, jnp.exp(gl - gmax), 0.0), axis=1, keepdims=True)
    e_lane = lane - N_GROUPS
    lane_grp = jnp.where(e_lane >= 0,
                         jnp.where(e_lane < N_EXPERTS, e_lane // EXPERTS_PER_GROUP, -1), -1)
    in_grp = lane_grp == gidx
    el = jnp.where(in_grp, lg, NEG)
    t1 = jnp.max(el, axis=1, keepdims=True)
    i1 = jnp.min(jnp.where(el == t1, lane, LANES), axis=1, keepdims=True)
    el2 = jnp.where(lane == i1, NEG, el)
    t2 = jnp.max(el2, axis=1, keepdims=True)
    i2 = jnp.min(jnp.where(el2 == t2, lane, LANES), axis=1, keepdims=True)
    r = jnp.exp(t2 - t1)
    p1 = 1.0 / (1.0 + r)
    p2 = r * p1
    e1 = (i1 - N_GROUPS).astype(F32)
    e2 = (i2 - N_GROUPS).astype(F32)
    record = jnp.where(lane == 0, e1, jnp.where(lane == 1, e2, jnp.where(
        lane == 2, p1 * p_g, jnp.where(lane == 3, p2 * p_g, 0.0))))
    route_ref[...] = record.T[0:SUBLANES, :]
    chosen = jnp.where(lane == i1, 1.0, jnp.where(lane == i2, 1.0, 0.0))
    return jnp.sum(chosen, axis=0, keepdims=True)


def _moe_kernel(start_ref, cnt_ref, tok_ref, slot_ref, wt_ref, x2_ref, wg_hbm, wu_hbm, wd_hbm,
                y2_ref, r2_scr, ga_scr, gb_scr, rt_scr, wg_buf, wu_buf, wd_buf, w_sem):
    ts = x2_ref.shape[0] // SUBLANES
    tmx = ga_scr.shape[0] // SUBLANES
    st = pl.program_id(0)
    e = pl.program_id(1)
    n_e = pl.num_programs(1)
    nsub = D_MODEL // LANES
    step = st * n_e + e
    n_steps = pl.num_programs(0) * n_e

    def weight_copies(g):
        ex = g % n_e
        buf = g % MOE_WEIGHT_BUFFERS
        return [pltpu.make_async_copy(hbm.at[ex], vbuf.at[buf], w_sem.at[k, buf])
                for k, (hbm, vbuf) in enumerate(((wg_hbm, wg_buf), (wu_hbm, wu_buf),
                                                 (wd_hbm, wd_buf)))]

    @pl.when(step == 0)
    def _():
        for g in range(MOE_WEIGHT_BUFFERS - 1):
            for cp in weight_copies(g):
                cp.start()

    @pl.when(step + MOE_WEIGHT_BUFFERS - 1 < n_steps)
    def _():
        for cp in weight_copies(step + MOE_WEIGHT_BUFFERS - 1):
            cp.start()

    for cp in weight_copies(step):
        cp.wait()
    wbuf = step % MOE_WEIGHT_BUFFERS

    def vreg_rows(i):
        return pl.ds(pl.multiple_of(i * SUBLANES, SUBLANES), SUBLANES)

    def gather(g_scr, base):
        def gather_body(r8, _):
            for u in range(SUBLANES):
                r = r8 * SUBLANES + u
                g_scr[vreg_rows(r), :] = x2_ref[vreg_rows(tok_ref[base + r]), :]
            return 0
        lax.fori_loop(0, tmx // SUBLANES, gather_body, 0)

    def expert_ffn(g_scr, base):
        xg = jnp.concatenate([g_scr[pl.ds(s, tmx, stride=nsub), :] for s in range(nsub)],
                             axis=1).astype(BF16)
        a = _dot(xg, wg_buf[wbuf])
        b = _dot(xg, wu_buf[wbuf])
        hg = (a * (1.0 / (1.0 + jnp.exp(-a))) * b).astype(BF16)
        yr = _dot(hg, wd_buf[wbuf])
        for s in range(nsub):
            rt_scr[pl.ds(s, tmx, stride=nsub), :] = yr[:, s * LANES:(s + 1) * LANES]
        r2_scr[pl.ds(pl.multiple_of(base * SUBLANES, SUBLANES), tmx * SUBLANES), :] = rt_scr[...]

    start = start_ref[step]
    cnt = cnt_ref[step]

    @pl.when(e == 0)
    def _():
        gather(ga_scr, start)

    def run(cur_scr, nxt_scr):
        expert_ffn(cur_scr, start)
        if nxt_scr is not None:
            nxt = start_ref[step + 1]
            for r in range(tmx):
                nxt_scr[r * SUBLANES:(r + 1) * SUBLANES, :] = (
                    x2_ref[vreg_rows(tok_ref[nxt + r]), :])

        def extra_tile(i, _):
            gather(cur_scr, start + i * tmx)
            expert_ffn(cur_scr, start + i * tmx)
            return 0
        lax.fori_loop(1, (cnt + tmx - 1) // tmx, extra_tile, 0)

    last = e == n_e - 1
    even = e % 2 == 0

    @pl.when(jnp.logical_and(even, jnp.logical_not(last)))
    def _():
        run(ga_scr, gb_scr)

    @pl.when(jnp.logical_and(jnp.logical_not(even), jnp.logical_not(last)))
    def _():
        run(gb_scr, ga_scr)

    @pl.when(last)
    def _():
        run(ga_scr if (N_EXPERTS - 1) % 2 == 0 else gb_scr, None)

        def combine_body(t8, _):
            for u in range(SUBLANES):
                t = t8 * SUBLANES + u
                y2_ref[vreg_rows(t), :] = (
                    wt_ref[t] * r2_scr[vreg_rows(slot_ref[t]), :]
                    + wt_ref[ts + t] * r2_scr[vreg_rows(slot_ref[ts + t]), :])
            return 0
        lax.fori_loop(0, ts // SUBLANES, combine_body, 0)


def _final_kernel(h1_ref, y2_ref, gf_ref, o_ref):
    nsub = D_MODEL // LANES
    y = jnp.concatenate([y2_ref[pl.ds(s, h1_ref.shape[0], stride=nsub), :] for s in range(nsub)],
                        axis=1)
    o_ref[...] = _rms(h1_ref[...] + y, gf_ref[...])


def _const_spec(shape):
    nd = len(shape)
    return pl.BlockSpec(shape, lambda *_: (0,) * nd, pipeline_mode=pl.Buffered(1))


def _params(n_axes):
    return pltpu.CompilerParams(dimension_semantics=("arbitrary",) * n_axes,
                                vmem_limit_bytes=VMEM_LIMIT)


def kernel(x, meta_tokens, norm1_g, w_in, kv_norm_g, w_uk, w_uv, w_pool, pool_scale,
           w_branch_attn, w_branch_pool, w_out, norm2_g, w_group_router, b_group_router,
           w_expert_router, b_expert_router, w_expert_gate, w_expert_up, w_expert_down,
           final_norm_g):
    B, S, D = x.shape
    assert D == D_MODEL and S % QB == 0 and S % TM_PROJ == 0 and w_in.shape[0] == 1
    assert QB == KC and TM_PROJ % KC == 0 and QB % CHUNK == 0
    N = B * S
    k_top = min(TOPK_MAX, S // 4)
    xr = x.reshape(N, D)

    wi = w_in[0]
    w1 = jnp.concatenate(
        [wi[:, 0:640], wi[:, 640:896], wi[:, 936:1448], wi[:, 896:936],
         jnp.zeros((D, W1_WIDTH - 1448), F32)], axis=1).astype(BF16)
    wgate = wi[:, 1448:].astype(BF16)
    g1 = norm1_g[0].reshape(1, D)
    kvg = kv_norm_g[0].reshape(1, KV_LATENT)
    wpool = w_pool[0].astype(BF16)
    pscale = pool_scale[0].reshape(1, POOL_WIDTH)
    wuk = jnp.transpose(w_uk[0], (1, 0, 2)).astype(BF16)
    wuvt = jnp.transpose(w_uv[0], (1, 2, 0)).astype(BF16)
    wr = jnp.concatenate(
        [w_group_router[0], w_expert_router[0].reshape(D, N_EXPERTS),
         jnp.zeros((D, LANES - N_GROUPS - N_EXPERTS), F32)], axis=1)
    br = jnp.concatenate(
        [b_group_router[0], b_expert_router[0].reshape(N_EXPERTS),
         jnp.zeros((LANES - N_GROUPS - N_EXPERTS,), F32)]).reshape(1, LANES)

    c_m, tail_m, pv_m = pl.pallas_call(
        _meta_kernel,
        out_shape=(jax.ShapeDtypeStruct((N_META, KV_LATENT), F32),
                   jax.ShapeDtypeStruct((N_META, LANES), F32),
                   jax.ShapeDtypeStruct((N_META, POOL_WIDTH), F32)),
        name="meta",
    )(meta_tokens, g1, w1, kvg)
    cm = c_m.astype(BF16)
    cmt = jnp.concatenate([cm.T, jnp.ones((1, N_META), BF16),
                           jnp.zeros((CT_ROWS - KV_LATENT - 1, N_META), BF16)], axis=0)
    ikm = tail_m[:, :IDX_DIM].astype(BF16)

    tpb = S // TM_PROJ
    tok = lambda w: pl.BlockSpec((TM_PROJ, w), lambda b, i: (b * tpb + i, 0))
    tok_t = lambda w: pl.BlockSpec((w, TM_PROJ), lambda b, i: (0, b * tpb + i))
    qt, c, ct, iqt, ik, iwt, yp = pl.pallas_call(
        _proj_kernel,
        grid=(B, tpb),
        in_specs=[tok(D), _const_spec((1, D)), _const_spec((D, W1_WIDTH)),
                  _const_spec((1, KV_LATENT)), _const_spec((N_META, POOL_WIDTH)),
                  _const_spec((len(POOL_WINDOWS), POOL_GROUP, POOL_GROUP)),
                  _const_spec((1, POOL_WIDTH))],
        out_specs=[tok_t(ATTN_WIDTH), tok(KV_LATENT),
                   pl.BlockSpec((None, TM_PROJ // KC, CT_ROWS, KC), lambda b, i: (b, i, 0, 0)),
                   tok_t(IDX_HEADS * IDX_DIM), tok(IDX_DIM), tok_t(IDX_HEADS), tok(POOL_WIDTH)],
        out_shape=(jax.ShapeDtypeStruct((ATTN_WIDTH, N), BF16),
                   jax.ShapeDtypeStruct((N, KV_LATENT), BF16),
                   jax.ShapeDtypeStruct((B, S // KC, CT_ROWS, KC), BF16),
                   jax.ShapeDtypeStruct((IDX_HEADS * IDX_DIM, N), BF16),
                   jax.ShapeDtypeStruct((N, IDX_DIM), BF16),
                   jax.ShapeDtypeStruct((IDX_HEADS, N), F32),
                   jax.ShapeDtypeStruct((N, POOL_WIDTH), BF16)),
        scratch_shapes=[pltpu.VMEM((TM_PROJ + N_META, POOL_WIDTH), F32)],
        compiler_params=_params(2),
        name="proj",
    )(xr, g1, w1, kvg, pv_m, wpool, pscale)

    nqb = S // QB
    qcol = lambda w: pl.BlockSpec((w, QB), lambda b, j: (0, b * nqb + j))
    attn = pl.pallas_call(
        functools.partial(_attn_kernel, k_top=float(k_top)),
        grid=(B, nqb),
        in_specs=[pl.BlockSpec((S, IDX_DIM), lambda b, j: (b, 0)),
                  _const_spec((N_META, IDX_DIM)),
                  pl.BlockSpec((S, KV_LATENT), lambda b, j: (b, 0)),
                  _const_spec((N_META, KV_LATENT)),
                  pl.BlockSpec((None, S // KC, CT_ROWS, KC), lambda b, j: (b, 0, 0, 0)),
                  _const_spec((CT_ROWS, N_META)),
                  qcol(IDX_HEADS * IDX_DIM), qcol(IDX_HEADS), qcol(ATTN_WIDTH),
                  _const_spec((N_HEADS, KV_LATENT, HEAD_DIM)),
                  _const_spec((N_HEADS, HEAD_DIM, KV_LATENT))],
        out_specs=pl.BlockSpec((QB, ATTN_WIDTH), lambda b, j: (b * nqb + j, 0)),
        out_shape=jax.ShapeDtypeStruct((N, ATTN_WIDTH), BF16),
        scratch_shapes=[pltpu.VMEM((S + N_META, QB), F32), pltpu.VMEM((S + N_META, QB), BF16),
                        pltpu.VMEM((KV_LATENT, N_HEADS * QB), BF16),
                        pltpu.VMEM((KC, N_HEADS * QB), F32),
                        pltpu.VMEM((1, N_HEADS * QB), F32),
                        pltpu.VMEM((CT_ROWS, N_HEADS * QB), F32),
                        pltpu.VMEM((ATTN_WIDTH, QB), F32)],
        compiler_params=_params(2),
        name="attn",
    )(ik, ikm, c, cm, ct, cmt, iqt, iwt, qt, wuk, wuvt)

    wr_hi = wr.astype(BF16)
    wr_lo = (wr - wr_hi.astype(F32)).astype(BF16)
    tok1 = lambda w: pl.BlockSpec((TM_MIX, w), lambda i: (i, 0))
    h1, u2, route, tile_counts = pl.pallas_call(
        _mix_kernel,
        grid=(N // TM_MIX,),
        in_specs=[tok1(D), _const_spec((1, D)), _const_spec((D, 2 * D)), tok1(ATTN_WIDTH),
                  tok1(POOL_WIDTH), _const_spec((ATTN_WIDTH, D)), _const_spec((POOL_WIDTH, D)),
                  _const_spec((D, D)), _const_spec((1, D)), _const_spec((D, 2 * LANES)),
                  _const_spec((D, LANES)), _const_spec((1, LANES))],
        out_specs=[tok1(D), pl.BlockSpec((TM_MIX * SUBLANES, LANES), lambda i: (i, 0)),
                   pl.BlockSpec((SUBLANES, TM_MIX), lambda i: (0, i)),
                   pl.BlockSpec((SUBLANES, LANES), lambda i: (i, 0))],
        out_shape=(jax.ShapeDtypeStruct((N, D), F32),
                   jax.ShapeDtypeStruct((N * SUBLANES, LANES), F32),
                   jax.ShapeDtypeStruct((SUBLANES, N), F32),
                   jax.ShapeDtypeStruct((N // TM_MIX * SUBLANES, LANES), F32)),
        scratch_shapes=[pltpu.VMEM((TM_MIX, D), BF16)],
        compiler_params=_params(1),
        name="mix",
    )(xr, g1, wgate, attn, yp, w_branch_attn[0].astype(BF16), w_branch_pool[0].astype(BF16),
      w_out[0].astype(BF16), norm2_g[0].reshape(1, D),
      jnp.concatenate([wr_hi, wr_lo], axis=1), wr_hi, br)

    nst = N // TS_MOE
    n_asg = 2 * TS_MOE
    per_tile = lambda a: a.reshape(2, nst, TS_MOE).transpose(1, 0, 2).reshape(nst, n_asg)
    eid = per_tile(route[0:2].astype(jnp.int32))
    wts = per_tile(route[2:4]).reshape(nst * n_asg)
    order = jnp.argsort(eid, axis=1, stable=True).astype(jnp.int32)
    slot = jnp.argsort(order, axis=1).astype(jnp.int32).reshape(nst * n_asg)
    tok_sorted = jnp.pad(order % TS_MOE, ((0, 0), (0, LIST_PAD))).reshape(
        nst * (n_asg + LIST_PAD))
    counts = tile_counts.reshape(nst, TS_MOE // TM_MIX, SUBLANES, LANES)[
        :, :, 0, N_GROUPS:N_GROUPS + N_EXPERTS].sum(axis=1).astype(jnp.int32)
    starts = (jnp.cumsum(counts, axis=1) - counts).reshape(nst * N_EXPERTS)
    counts = counts.reshape(nst * N_EXPERTS)

    weg = w_expert_gate[0].reshape(N_EXPERTS, D, EXPERT_HIDDEN).astype(BF16)
    weu = w_expert_up[0].reshape(N_EXPERTS, D, EXPERT_HIDDEN).astype(BF16)
    wed = w_expert_down[0].reshape(N_EXPERTS, EXPERT_HIDDEN, D).astype(BF16)
    smem = lambda n: pl.BlockSpec((n,), lambda s, e, *_: (s,), memory_space=pltpu.SMEM)
    y = pl.pallas_call(
        _moe_kernel,
        grid_spec=pltpu.PrefetchScalarGridSpec(
            num_scalar_prefetch=2,
            grid=(nst, N_EXPERTS),
            in_specs=[smem(n_asg + LIST_PAD), smem(n_asg), smem(n_asg),
                      pl.BlockSpec((TS_MOE * SUBLANES, LANES), lambda s, e, *_: (s, 0)),
                      pl.BlockSpec(memory_space=pl.ANY), pl.BlockSpec(memory_space=pl.ANY),
                      pl.BlockSpec(memory_space=pl.ANY)],
            out_specs=pl.BlockSpec((TS_MOE * SUBLANES, LANES), lambda s, e, *_: (s, 0)),
            scratch_shapes=[pltpu.VMEM(((n_asg + TMX_MOE) * SUBLANES, LANES), F32),
                            pltpu.VMEM((TMX_MOE * SUBLANES, LANES), F32),
                            pltpu.VMEM((TMX_MOE * SUBLANES, LANES), F32),
                            pltpu.VMEM((TMX_MOE * SUBLANES, LANES), F32),
                            pltpu.VMEM((MOE_WEIGHT_BUFFERS, D, EXPERT_HIDDEN), BF16),
                            pltpu.VMEM((MOE_WEIGHT_BUFFERS, D, EXPERT_HIDDEN), BF16),
                            pltpu.VMEM((MOE_WEIGHT_BUFFERS, EXPERT_HIDDEN, D), BF16),
                            pltpu.SemaphoreType.DMA((3, MOE_WEIGHT_BUFFERS))]),
        out_shape=jax.ShapeDtypeStruct((N * SUBLANES, LANES), F32),
        compiler_params=_params(2),
        name="moe",
    )(starts, counts, tok_sorted, slot, wts, u2, weg, weu, wed)

    ftok = pl.BlockSpec((TM_FINAL, D), lambda i: (i, 0))
    out = pl.pallas_call(
        _final_kernel,
        grid=(N // TM_FINAL,),
        in_specs=[ftok, pl.BlockSpec((TM_FINAL * SUBLANES, LANES), lambda i: (i, 0)),
                  _const_spec((1, D))],
        out_specs=ftok,
        out_shape=jax.ShapeDtypeStruct((N, D), F32),
        compiler_params=_params(1),
        name="final",
    )(h1, y, final_norm_g.reshape(1, D))
    return out.reshape(B, S, D)
```

```python
import functools

import jax
import jax.numpy as jnp
from jax import lax
from jax.experimental import pallas as pl
from jax.experimental.pallas import tpu as pltpu

F32 = jnp.float32
BF16 = jnp.bfloat16

D_MODEL = 1024
CHUNK = 64
N_META = 16
N_HEADS = 8
HEAD_DIM = 64
ATTN_WIDTH = N_HEADS * HEAD_DIM
KV_LATENT = 128
IDX_HEADS = 8
IDX_DIM = 32
TOPK_MAX = 256
ATTN_SCALE = HEAD_DIM ** -0.5
IDX_SCALE = (IDX_HEADS ** -0.5) * (IDX_DIM ** -0.5)
POOL_WINDOWS = (2, 4, 8, 16)
POOL_WIDTH = 512
POOL_GROUP = 128
N_GROUPS = 4
EXPERTS_PER_GROUP = 8
N_EXPERTS = N_GROUPS * EXPERTS_PER_GROUP
EXPERT_HIDDEN = 256
EPS = 1e-6

LANES = 128
SUBLANES = 8
W1_WIDTH = 1536
NEG = -1e30
POS = 1e30
VMEM_LIMIT = 56 * 1024 * 1024

TM_PROJ = 1024
TM_MIX = 1024
MIX_ROWS = 256
MIX_COLS = 256
QB = 256
KC = 256
MAX_BISECT = 40
BISECT_PER_CHECK = 4
FIRST_TIE_CHECK = 12
CT_ROWS = KV_LATENT + 16
LOG2E = 1.4426950408889634
TS_MOE = 2048
TMX_MOE = 160
MOE_WEIGHT_BUFFERS = 3
LIST_PAD = 1024
TM_FINAL = 1024


def _rms(x, g):
    return x * lax.rsqrt(jnp.mean(x * x, axis=-1, keepdims=True) + EPS) * g


def _dot(a, b):
    return jnp.dot(a, b, preferred_element_type=F32)


def _meta_kernel(m_ref, g1_ref, w1_ref, kvg_ref, c_ref, tail_ref, pv_ref):
    u = _rms(m_ref[...], g1_ref[...]).astype(BF16)
    p = _dot(u, w1_ref[...])
    c_ref[...] = _rms(p[:, 512:640], kvg_ref[...])
    pv_ref[...] = p[:, 896:1408]
    tail_ref[...] = p[:, 1408:1536]


def _proj_kernel(x_ref, g1_ref, w1_ref, kvg_ref, pvmeta_ref, wpool_ref, pscale_ref,
                 qt_ref, c_ref, ct_ref, iqt_ref, ik_ref, iwt_ref, yp_ref, ext_ref):
    tm = x_ref.shape[0]

    @pl.when(pl.program_id(1) == 0)
    def _():
        ext_ref[0:N_META, :] = pvmeta_ref[...]

    for r0 in range(0, tm, KC):
        rs = slice(r0, r0 + KC)
        u = _rms(x_ref[rs, :], g1_ref[...]).astype(BF16)
        p = _dot(u, w1_ref[...])
        qt_ref[:, rs] = p[:, 0:512].T.astype(BF16)
        c = _rms(p[:, 512:640], kvg_ref[...])
        c_ref[rs, :] = c.astype(BF16)
        ct_ref[r0 // KC] = jnp.concatenate(
            [c.T, jnp.ones((1, KC), F32), jnp.zeros((CT_ROWS - KV_LATENT - 1, KC), F32)],
            axis=0).astype(BF16)
        iqt_ref[:, rs] = p[:, 640:896].T.astype(BF16)
        tail = p[:, 1408:1536]
        ik_ref[rs, :] = tail[:, 0:IDX_DIM].astype(BF16)
        iwt_ref[:, rs] = tail.T[IDX_DIM:IDX_DIM + IDX_HEADS, :] * IDX_SCALE
        pv = p[:, 896:1408]
        ext_ref[N_META + r0:N_META + r0 + KC, :] = pv
        for g, w in enumerate(POOL_WINDOWS):
            cols = slice(g * POOL_GROUP, (g + 1) * POOL_GROUP)
            acc = pv[:, cols]
            for k in range(1, w):
                acc = acc + ext_ref[N_META + r0 - k:N_META + r0 - k + KC, cols]
            d = acc * (1.0 / w) - pv[:, cols]
            y = _dot(d.astype(BF16), wpool_ref[g]) * pscale_ref[:, cols]
            yp_ref[rs, cols] = y.astype(BF16)
    ext_ref[0:N_META, :] = ext_ref[tm:tm + N_META, :]


def _attn_kernel(ik_ref, ikm_ref, c_ref, cm_ref, ct_ref, cmt_ref, iqt_ref, iwt_ref, qt_ref,
                 wuk_ref, wuvt_ref, o_ref,
                 s_scr, s16_scr, qat_scr, lg_scr, m_scr, acc_scr, ot_scr, *, k_top):
    n_real = c_ref.shape[0]
    qb = iqt_ref.shape[1]
    j = pl.program_id(1)
    nkc = j + 1
    qchunk = (j * qb + lax.broadcasted_iota(jnp.int32, (1, qb), 1)) // CHUNK
    meta_rows = pl.ds(n_real, N_META)

    def rows(kc):
        return pl.ds(pl.multiple_of(kc * KC, KC), KC)

    def fold(x, op):
        groups = x.shape[0] // SUBLANES
        chains = 4 if groups % 4 == 0 else 1
        x = x.reshape(groups // chains, chains, SUBLANES, qb)
        return op(op(x, axis=0), axis=0)

    iqt_heads = [iqt_ref[h * IDX_DIM:(h + 1) * IDX_DIM, :] for h in range(IDX_HEADS)]

    def scores(ik_rows):
        acc = None
        for h in range(IDX_HEADS):
            t = jnp.maximum(_dot(ik_rows, iqt_heads[h]), 0.0) * iwt_ref[h:h + 1, :]
            acc = t if acc is None else acc + t
        return acc

    def score_body(kc, carry):
        mn, mx = carry
        sc = scores(ik_ref[rows(kc), :])
        s16_scr[rows(kc), :] = sc.astype(BF16)
        return jnp.minimum(mn, fold(sc, jnp.min)), jnp.maximum(mx, fold(sc, jnp.max))

    mn8, mx8 = lax.fori_loop(0, j, score_body, (jnp.full((SUBLANES, qb), POS, F32),
                                                jnp.full((SUBLANES, qb), NEG, F32)))
    sc_all = scores(jnp.concatenate([ik_ref[rows(j), :], ikm_ref[...]], axis=0))
    sc = sc_all[0:KC, :]
    sm = sc_all[KC:KC + N_META, :]
    s16_scr[meta_rows, :] = sm.astype(BF16)
    mn0 = jnp.min(sm, axis=0, keepdims=True)
    mx0 = jnp.max(sm, axis=0, keepdims=True)
    adm = (j * KC + lax.broadcasted_iota(jnp.int32, (KC, 1), 0)) // CHUNK <= qchunk
    s16_scr[rows(j), :] = jnp.where(adm, sc, NEG).astype(BF16)
    mn8 = jnp.minimum(mn8, fold(jnp.where(adm, sc, POS), jnp.min))
    mx8 = jnp.maximum(mx8, fold(jnp.where(adm, sc, NEG), jnp.max))
    mn = jnp.minimum(mn0, jnp.min(mn8, axis=0, keepdims=True))
    mx = jnp.maximum(mx0, jnp.max(mx8, axis=0, keepdims=True))

    pack = 2 * SUBLANES

    def as16(v):
        return jnp.broadcast_to(v, (pack, qb)).astype(BF16)

    def tree(parts, op):
        while len(parts) > 1:
            parts = ([op(parts[i], parts[i + 1]) for i in range(0, len(parts) - 1, 2)]
                     + ([parts[-1]] if len(parts) % 2 else []))
        return parts[0]

    def groups(x):
        return [x[i * pack:(i + 1) * pack, :] for i in range(x.shape[0] // pack)]

    one16 = jnp.ones((pack, qb), BF16)
    zero16 = jnp.zeros((pack, qb), BF16)

    def count_ge(th):
        th16 = as16(th)

        def ones(x):
            return tree([jnp.where(g >= th16, one16, zero16) for g in groups(x)], jnp.add)

        def body(kc, cnt):
            return cnt + ones(s16_scr[rows(kc), :])
        cnt = lax.fori_loop(0, nkc, body, ones(s16_scr[meta_rows, :]))
        return jnp.sum(cnt.astype(F32), axis=0, keepdims=True)

    def band_extent(lo, hi):
        lo16, hi16 = as16(lo), as16(hi)
        pos16 = jnp.full((pack, qb), POS, BF16)
        neg16 = jnp.full((pack, qb), NEG, BF16)

        def ext(x):
            return (tree([jnp.where(g >= lo16, g, pos16) for g in groups(x)], jnp.minimum),
                    tree([jnp.where(g < hi16, g, neg16) for g in groups(x)], jnp.maximum))

        def body(kc, carry):
            bmin, bmax = ext(s16_scr[rows(kc), :])
            return jnp.minimum(carry[0], bmin), jnp.maximum(carry[1], bmax)
        bmin, bmax = lax.fori_loop(0, nkc, body, ext(s16_scr[meta_rows, :]))
        return (jnp.min(bmin.astype(F32), axis=0, keepdims=True),
                jnp.max(bmax.astype(F32), axis=0, keepdims=True))

    def any_lane(flags):
        return jnp.max(flags)

    def bisect_cond(carry):
        it, _, _, _, pending = carry
        return jnp.logical_and(it < MAX_BISECT, pending > 0)

    def bisect_body(carry):
        it, lo, hi, cnt_lo, _ = carry
        for _ in range(BISECT_PER_CHECK):
            mid = (lo + (hi - lo) * 0.5).astype(BF16).astype(F32)
            cnt = count_ge(mid)
            ge = cnt >= k_top
            lo = jnp.where(ge, mid, lo)
            hi = jnp.where(ge, hi, mid)
            cnt_lo = jnp.where(ge, cnt, cnt_lo)
        it = it + BISECT_PER_CHECK
        over = jnp.where(cnt_lo > k_top, 1, 0)
        pending = any_lane(over)

        def tied_check():
            bmin, bmax = band_extent(lo, hi)
            return any_lane(jnp.where(bmin < bmax, over, 0))

        pending = lax.cond(jnp.logical_and(pending > 0, it >= FIRST_TIE_CHECK),
                           tied_check, lambda: pending)
        return it, lo, hi, cnt_lo, pending

    n_adm = (N_META + CHUNK * (qchunk + 1)).astype(F32)
    lo0 = mn.astype(BF16).astype(F32)
    hi0 = (mx + jnp.abs(mx) * (2.0 ** -6) + 1e-30).astype(BF16).astype(F32)
    _, lo, hi, cnt_lo, _ = lax.while_loop(
        bisect_cond, bisect_body,
        (jnp.int32(0), lo0, hi0, n_adm, any_lane(jnp.where(n_adm > k_top, 1, 0))))
    pending = any_lane(jnp.where(cnt_lo > k_top, 1, 0))

    @pl.when(pending == 0)
    def _():
        def body(kc, _):
            s_scr[rows(kc), :] = jnp.where(s16_scr[rows(kc), :].astype(F32) >= lo, 0.0, NEG)
            return 0
        lax.fori_loop(0, nkc, body, 0)
        s_scr[meta_rows, :] = jnp.where(s16_scr[meta_rows, :].astype(F32) >= lo, 0.0, NEG)

    @pl.when(pending > 0)
    def _():
        need = k_top - count_ge(hi)

        def pick(sv, tri, before):
            band = jnp.where(sv >= lo, jnp.where(sv < hi, 1.0, 0.0), 0.0)
            rank = _dot(tri, band.astype(BF16)) + before
            take = jnp.where(rank <= need, band, 0.0)
            bias = jnp.where(sv >= hi, 0.0, jnp.where(take > 0.5, 0.0, NEG))
            return bias, before + jnp.sum(fold(band, jnp.sum), axis=0, keepdims=True)

        def tri(n):
            return jnp.where(lax.broadcasted_iota(jnp.int32, (n, n), 0)
                             >= lax.broadcasted_iota(jnp.int32, (n, n), 1), 1.0, 0.0).astype(BF16)

        bias_m, before = pick(s16_scr[meta_rows, :].astype(F32), tri(N_META),
                              jnp.zeros((1, qb), F32))
        s_scr[meta_rows, :] = bias_m
        tri_kc = tri(KC)

        def body(kc, before):
            bias, before = pick(s16_scr[rows(kc), :].astype(F32), tri_kc, before)
            s_scr[rows(kc), :] = bias
            return before
        lax.fori_loop(0, nkc, body, before)

    def head(h):
        return slice(h * qb, (h + 1) * qb)

    for h in range(N_HEADS):
        qat_scr[:, head(h)] = (_dot(wuk_ref[h], qt_ref[h * HEAD_DIM:(h + 1) * HEAD_DIM, :])
                               * (ATTN_SCALE * LOG2E)).astype(BF16)
    m_scr[...] = jnp.full(m_scr.shape, 0.5 * NEG, F32)
    acc_scr[...] = jnp.zeros(acc_scr.shape, F32)

    def attend(c_rows, bias, values):
        r = c_rows.shape[0]
        lg_scr[0:r, :] = _dot(c_rows, qat_scr[...])
        for h in range(N_HEADS):
            lg = lg_scr[0:r, head(h)] + bias
            m_old = m_scr[:, head(h)]
            m_new = jnp.maximum(m_old, jnp.max(lg, axis=0, keepdims=True))
            p = jnp.exp2(lg - m_new).astype(BF16)
            m_scr[:, head(h)] = m_new
            pv = None
            for ct_cols, rs in values:
                t = _dot(ct_cols, p[rs, :])
                pv = t if pv is None else pv + t
            acc_scr[:, head(h)] = acc_scr[:, head(h)] * jnp.exp2(m_old - m_new) + pv

    def attend_chunk(kc):
        attend(c_ref[rows(kc), :], s_scr[rows(kc), :], [(ct_ref[kc], slice(0, KC))])

    def attend_pair(i, _):
        attend_chunk(2 * i)
        attend_chunk(2 * i + 1)
        return 0
    lax.fori_loop(0, j // 2, attend_pair, 0)

    @pl.when(j % 2 == 1)
    def _():
        attend_chunk(j - 1)

    attend(jnp.concatenate([c_ref[rows(j), :], cm_ref[...]], axis=0),
           jnp.concatenate([s_scr[rows(j), :], s_scr[meta_rows, :]], axis=0),
           [(ct_ref[j], slice(0, KC)), (cmt_ref[...], slice(KC, KC + N_META))])

    for h in range(N_HEADS):
        olat = acc_scr[0:KV_LATENT, head(h)] / acc_scr[KV_LATENT:KV_LATENT + 1, head(h)]
        ot_scr[h * HEAD_DIM:(h + 1) * HEAD_DIM, :] = _dot(wuvt_ref[h], olat.astype(BF16))
    o_ref[...] = ot_scr[...].T.astype(BF16)


def _mix_kernel(x_ref, g1_ref, wgate_ref, attn_ref, yp_ref, wba_ref, wbp_ref, wout_ref,
                g2_ref, wr_ref, wrhi_ref, br_ref, h1_ref, u2_ref, route_ref, count_ref,
                merged_scr):
    nsub = D_MODEL // LANES
    counts = None
    for r0 in range(0, x_ref.shape[0], MIX_ROWS):
        rows = pl.ds(r0, MIX_ROWS)
        c = _mix_group(x_ref.at[rows], g1_ref, wgate_ref, attn_ref.at[rows], yp_ref.at[rows],
                       wba_ref, wbp_ref, wout_ref, g2_ref, wr_ref, wrhi_ref, br_ref,
                       h1_ref.at[rows], u2_ref.at[pl.ds(r0 * nsub, MIX_ROWS * nsub)],
                       route_ref.at[:, rows], merged_scr.at[rows])
        counts = c if counts is None else counts + c
    count_ref[...] = jnp.broadcast_to(counts, count_ref.shape)


def _mix_group(x_ref, g1_ref, wgate_ref, attn_ref, yp_ref, wba_ref, wbp_ref, wout_ref,
               g2_ref, wr_ref, wrhi_ref, br_ref, h1_ref, u2_ref, route_ref, merged_scr):
    x = x_ref[...]
    u = _rms(x, g1_ref[...]).astype(BF16)
    attn = attn_ref[...]
    yp = yp_ref[...]
    for cb in range(D_MODEL // MIX_COLS):
        ca = slice(cb * MIX_COLS, (cb + 1) * MIX_COLS)
        cp = slice(D_MODEL + cb * MIX_COLS, D_MODEL + (cb + 1) * MIX_COLS)
        g_attn = 1.0 / (1.0 + jnp.exp(-_dot(u, wgate_ref[:, ca])))
        g_pool = 1.0 / (1.0 + jnp.exp(-_dot(u, wgate_ref[:, cp])))
        merged_scr[:, ca] = (g_attn * _dot(attn, wba_ref[:, ca])
                             + g_pool * _dot(yp, wbp_ref[:, ca])).astype(BF16)
    h1 = x + _dot(merged_scr[...], wout_ref[...])
    h1_ref[...] = h1
    u2 = _rms(h1, g2_ref[...])
    nsub = D_MODEL // LANES
    for s in range(nsub):
        u2_ref[pl.ds(s, x.shape[0], stride=nsub), :] = u2[:, s * LANES:(s + 1) * LANES]
    u2_hi = u2.astype(BF16)

    u2_lo = (u2 - u2_hi.astype(F32)).astype(BF16)
    hi_both = _dot(u2_hi, wr_ref[...])
    lg = hi_both[:, :LANES] + hi_both[:, LANES:] + _dot(u2_lo, wrhi_ref[...]) + br_ref[...]
    lane = lax.broadcasted_iota(jnp.int32, lg.shape, 1)
    is_g = lane < N_GROUPS
    gl = jnp.where(is_g, lg, NEG)
    gmax = jnp.max(gl, axis=1, keepdims=True)
    gidx = jnp.min(jnp.where(gl == gmax, lane, LANES), axis=1, keepdims=True)
    p_g = 1.0 / jnp.sum(jnp.where(is_g, jnp.exp(gl - gmax), 0.0), axis=1, keepdims=True)
    e_lane = lane - N_GROUPS
    lane_grp = jnp.where(e_lane >= 0,
                         jnp.where(e_lane < N_EXPERTS, e_lane // EXPERTS_PER_GROUP, -1), -1)
    in_grp = lane_grp == gidx
    el = jnp.where(in_grp, lg, NEG)
    t1 = jnp.max(el, axis=1, keepdims=True)
    i1 = jnp.min(jnp.where(el == t1, lane, LANES), axis=1, keepdims=True)
    el2 = jnp.where(lane == i1, NEG, el)
    t2 = jnp.max(el2, axis=1, keepdims=True)
    i2 = jnp.min(jnp.where(el2 == t2, lane, LANES), axis=1, keepdims=True)
    r = jnp.exp(t2 - t1)
    p1 = 1.0 / (1.0 + r)
    p2 = r * p1
    e1 = (i1 - N_GROUPS).astype(F32)
    e2 = (i2 - N_GROUPS).astype(F32)
    record = jnp.where(lane == 0, e1, jnp.where(lane == 1, e2, jnp.where(
        lane == 2, p1 * p_g, jnp.where(lane == 3, p2 * p_g, 0.0))))
    route_ref[...] = record.T[0:SUBLANES, :]
    chosen = jnp.where(lane == i1, 1.0, jnp.where(lane == i2, 1.0, 0.0))
    return jnp.sum(chosen, axis=0, keepdims=True)


def _moe_kernel(start_ref, cnt_ref, tok_ref, slot_ref, wt_ref, x2_ref, wg_hbm, wu_hbm, wd_hbm,
                y2_ref, r2_scr, ga_scr, gb_scr, rt_scr, wg_buf, wu_buf, wd_buf, w_sem):
    ts = x2_ref.shape[0] // SUBLANES
    tmx = ga_scr.shape[0] // SUBLANES
    st = pl.program_id(0)
    e = pl.program_id(1)
    n_e = pl.num_programs(1)
    nsub = D_MODEL // LANES
    step = st * n_e + e
    n_steps = pl.num_programs(0) * n_e

    def weight_copies(g):
        ex = g % n_e
        buf = g % MOE_WEIGHT_BUFFERS
        return [pltpu.make_async_copy(hbm.at[ex], vbuf.at[buf], w_sem.at[k, buf])
                for k, (hbm, vbuf) in enumerate(((wg_hbm, wg_buf), (wu_hbm, wu_buf),
                                                 (wd_hbm, wd_buf)))]

    @pl.when(step == 0)
    def _():
        for g in range(MOE_WEIGHT_BUFFERS - 1):
            for cp in weight_copies(g):
                cp.start()

    @pl.when(step + MOE_WEIGHT_BUFFERS - 1 < n_steps)
    def _():
        for cp in weight_copies(step + MOE_WEIGHT_BUFFERS - 1):
            cp.start()

    for cp in weight_copies(step):
        cp.wait()
    wbuf = step % MOE_WEIGHT_BUFFERS

    def vreg_rows(i):
        return pl.ds(pl.multiple_of(i * SUBLANES, SUBLANES), SUBLANES)

    def gather(g_scr, base):
        def gather_body(r8, _):
            for u in range(SUBLANES):
                r = r8 * SUBLANES + u
                g_scr[vreg_rows(r), :] = x2_ref[vreg_rows(tok_ref[base + r]), :]
            return 0
        lax.fori_loop(0, tmx // SUBLANES, gather_body, 0)

    def expert_ffn(g_scr, base):
        xg = jnp.concatenate([g_scr[pl.ds(s, tmx, stride=nsub), :] for s in range(nsub)],
                             axis=1).astype(BF16)
        a = _dot(xg, wg_buf[wbuf])
        b = _dot(xg, wu_buf[wbuf])
        hg = (a * (1.0 / (1.0 + jnp.exp(-a))) * b).astype(BF16)
        yr = _dot(hg, wd_buf[wbuf])
        for s in range(nsub):
            rt_scr[pl.ds(s, tmx, stride=nsub), :] = yr[:, s * LANES:(s + 1) * LANES]
        r2_scr[pl.ds(pl.multiple_of(base * SUBLANES, SUBLANES), tmx * SUBLANES), :] = rt_scr[...]

    start = start_ref[step]
    cnt = cnt_ref[step]

    @pl.when(e == 0)
    def _():
        gather(ga_scr, start)

    def run(cur_scr, nxt_scr):
        expert_ffn(cur_scr, start)
        if nxt_scr is not None:
            nxt = start_ref[step + 1]
            for r in range(tmx):
                nxt_scr[r * SUBLANES:(r + 1) * SUBLANES, :] = (
                    x2_ref[vreg_rows(tok_ref[nxt + r]), :])

        def extra_tile(i, _):
            gather(cur_scr, start + i * tmx)
            expert_ffn(cur_scr, start + i * tmx)
            return 0
        lax.fori_loop(1, (cnt + tmx - 1) // tmx, extra_tile, 0)

    last = e == n_e - 1
    even = e % 2 == 0

    @pl.when(jnp.logical_and(even, jnp.logical_not(last)))
    def _():
        run(ga_scr, gb_scr)

    @pl.when(jnp.logical_and(jnp.logical_not(even), jnp.logical_not(last)))
    def _():
        run(gb_scr, ga_scr)

    @pl.when(last)
    def _():
        run(ga_scr if (N_EXPERTS - 1) % 2 == 0 else gb_scr, None)

        def combine_body(t8, _):
            for u in range(SUBLANES):
                t = t8 * SUBLANES + u
                y2_ref[vreg_rows(t), :] = (
                    wt_ref[t] * r2_scr[vreg_rows(slot_ref[t]), :]
                    + wt_ref[ts + t] * r2_scr[vreg_rows(slot_ref[ts + t]), :])
            return 0
        lax.fori_loop(0, ts // SUBLANES, combine_body, 0)


def _final_kernel(h1_ref, y2_ref, gf_ref, o_ref):
    nsub = D_MODEL // LANES
    y = jnp.concatenate([y2_ref[pl.ds(s, h1_ref.shape[0], stride=nsub), :] for s in range(nsub)],
                        axis=1)
    o_ref[...] = _rms(h1_ref[...] + y, gf_ref[...])


def _const_spec(shape):
    nd = len(shape)
    return pl.BlockSpec(shape, lambda *_: (0,) * nd, pipeline_mode=pl.Buffered(1))


def _params(n_axes):
    return pltpu.CompilerParams(dimension_semantics=("arbitrary",) * n_axes,
                                vmem_limit_bytes=VMEM_LIMIT)


def kernel(x, meta_tokens, norm1_g, w_in, kv_norm_g, w_uk, w_uv, w_pool, pool_scale,
           w_branch_attn, w_branch_pool, w_out, norm2_g, w_group_router, b_group_router,
           w_expert_router, b_expert_router, w_expert_gate, w_expert_up, w_expert_down,
           final_norm_g):
    B, S, D = x.shape
    assert D == D_MODEL and S % QB == 0 and S % TM_PROJ == 0 and w_in.shape[0] == 1
    assert QB == KC and TM_PROJ % KC == 0 and QB % CHUNK == 0
    N = B * S
    k_top = min(TOPK_MAX, S // 4)
    xr = x.reshape(N, D)

    wi = w_in[0]
    w1 = jnp.concatenate(
        [wi[:, 0:640], wi[:, 640:896], wi[:, 936:1448], wi[:, 896:936],
         jnp.zeros((D, W1_WIDTH - 1448), F32)], axis=1).astype(BF16)
    wgate = wi[:, 1448:].astype(BF16)
    g1 = norm1_g[0].reshape(1, D)
    kvg = kv_norm_g[0].reshape(1, KV_LATENT)
    wpool = w_pool[0].astype(BF16)
    pscale = pool_scale[0].reshape(1, POOL_WIDTH)
    wuk = jnp.transpose(w_uk[0], (1, 0, 2)).astype(BF16)
    wuvt = jnp.transpose(w_uv[0], (1, 2, 0)).astype(BF16)
    wr = jnp.concatenate(
        [w_group_router[0], w_expert_router[0].reshape(D, N_EXPERTS),
         jnp.zeros((D, LANES - N_GROUPS - N_EXPERTS), F32)], axis=1)
    br = jnp.concatenate(
        [b_group_router[0], b_expert_router[0].reshape(N_EXPERTS),
         jnp.zeros((LANES - N_GROUPS - N_EXPERTS,), F32)]).reshape(1, LANES)

    c_m, tail_m, pv_m = pl.pallas_call(
        _meta_kernel,
        out_shape=(jax.ShapeDtypeStruct((N_META, KV_LATENT), F32),
                   jax.ShapeDtypeStruct((N_META, LANES), F32),
                   jax.ShapeDtypeStruct((N_META, POOL_WIDTH), F32)),
        name="meta",
    )(meta_tokens, g1, w1, kvg)
    cm = c_m.astype(BF16)
    cmt = jnp.concatenate([cm.T, jnp.ones((1, N_META), BF16),
                           jnp.zeros((CT_ROWS - KV_LATENT - 1, N_META), BF16)], axis=0)
    ikm = tail_m[:, :IDX_DIM].astype(BF16)

    tpb = S // TM_PROJ
    tok = lambda w: pl.BlockSpec((TM_PROJ, w), lambda b, i: (b * tpb + i, 0))
    tok_t = lambda w: pl.BlockSpec((w, TM_PROJ), lambda b, i: (0, b * tpb + i))
    qt, c, ct, iqt, ik, iwt, yp = pl.pallas_call(
        _proj_kernel,
        grid=(B, tpb),
        in_specs=[tok(D), _const_spec((1, D)), _const_spec((D, W1_WIDTH)),
                  _const_spec((1, KV_LATENT)), _const_spec((N_META, POOL_WIDTH)),
                  _const_spec((len(POOL_WINDOWS), POOL_GROUP, POOL_GROUP)),
                  _const_spec((1, POOL_WIDTH))],
        out_specs=[tok_t(ATTN_WIDTH), tok(KV_LATENT),
                   pl.BlockSpec((None, TM_PROJ // KC, CT_ROWS, KC), lambda b, i: (b, i, 0, 0)),
                   tok_t(IDX_HEADS * IDX_DIM), tok(IDX_DIM), tok_t(IDX_HEADS), tok(POOL_WIDTH)],
        out_shape=(jax.ShapeDtypeStruct((ATTN_WIDTH, N), BF16),
                   jax.ShapeDtypeStruct((N, KV_LATENT), BF16),
                   jax.ShapeDtypeStruct((B, S // KC, CT_ROWS, KC), BF16),
                   jax.ShapeDtypeStruct((IDX_HEADS * IDX_DIM, N), BF16),
                   jax.ShapeDtypeStruct((N, IDX_DIM), BF16),
                   jax.ShapeDtypeStruct((IDX_HEADS, N), F32),
                   jax.ShapeDtypeStruct((N, POOL_WIDTH), BF16)),
        scratch_shapes=[pltpu.VMEM((TM_PROJ + N_META, POOL_WIDTH), F32)],
        compiler_params=_params(2),
        name="proj",
    )(xr, g1, w1, kvg, pv_m, wpool, pscale)

    nqb = S // QB
    qcol = lambda w: pl.BlockSpec((w, QB), lambda b, j: (0, b * nqb + j))
    attn = pl.pallas_call(
        functools.partial(_attn_kernel, k_top=float(k_top)),
        grid=(B, nqb),
        in_specs=[pl.BlockSpec((S, IDX_DIM), lambda b, j: (b, 0)),
                  _const_spec((N_META, IDX_DIM)),
                  pl.BlockSpec((S, KV_LATENT), lambda b, j: (b, 0)),
                  _const_spec((N_META, KV_LATENT)),
                  pl.BlockSpec((None, S // KC, CT_ROWS, KC), lambda b, j: (b, 0, 0, 0)),
                  _const_spec((CT_ROWS, N_META)),
                  qcol(IDX_HEADS * IDX_DIM), qcol(IDX_HEADS), qcol(ATTN_WIDTH),
                  _const_spec((N_HEADS, KV_LATENT, HEAD_DIM)),
                  _const_spec((N_HEADS, HEAD_DIM, KV_LATENT))],
        out_specs=pl.BlockSpec((QB, ATTN_WIDTH), lambda b, j: (b * nqb + j, 0)),
        out_shape=jax.ShapeDtypeStruct((N, ATTN_WIDTH), BF16),
        scratch_shapes=[pltpu.VMEM((S + N_META, QB), F32), pltpu.VMEM((S + N_META, QB), BF16),
                        pltpu.VMEM((KV_LATENT, N_HEADS * QB), BF16),
                        pltpu.VMEM((KC + N_META, N_HEADS * QB), F32),
                        pltpu.VMEM((1, N_HEADS * QB), F32),
                        pltpu.VMEM((CT_ROWS, N_HEADS * QB), F32),
                        pltpu.VMEM((ATTN_WIDTH, QB), F32)],
        compiler_params=_params(2),
        name="attn",
    )(ik, ikm, c, cm, ct, cmt, iqt, iwt, qt, wuk, wuvt)

    wr_hi = wr.astype(BF16)
    wr_lo = (wr - wr_hi.astype(F32)).astype(BF16)
    tok1 = lambda w: pl.BlockSpec((TM_MIX, w), lambda i: (i, 0))
    h1, u2, route, tile_counts = pl.pallas_call(
        _mix_kernel,
        grid=(N // TM_MIX,),
        in_specs=[tok1(D), _const_spec((1, D)), _const_spec((D, 2 * D)), tok1(ATTN_WIDTH),
                  tok1(POOL_WIDTH), _const_spec((ATTN_WIDTH, D)), _const_spec((POOL_WIDTH, D)),
                  _const_spec((D, D)), _const_spec((1, D)), _const_spec((D, 2 * LANES)),
                  _const_spec((D, LANES)), _const_spec((1, LANES))],
        out_specs=[tok1(D), pl.BlockSpec((TM_MIX * SUBLANES, LANES), lambda i: (i, 0)),
                   pl.BlockSpec((SUBLANES, TM_MIX), lambda i: (0, i)),
                   pl.BlockSpec((SUBLANES, LANES), lambda i: (i, 0))],
        out_shape=(jax.ShapeDtypeStruct((N, D), F32),
                   jax.ShapeDtypeStruct((N * SUBLANES, LANES), F32),
                   jax.ShapeDtypeStruct((SUBLANES, N), F32),
                   jax.ShapeDtypeStruct((N // TM_MIX * SUBLANES, LANES), F32)),
        scratch_shapes=[pltpu.VMEM((TM_MIX, D), BF16)],
        compiler_params=_params(1),
        name="mix",
    )(xr, g1, wgate, attn, yp, w_branch_attn[0].astype(BF16), w_branch_pool[0].astype(BF16),
      w_out[0].astype(BF16), norm2_g[0].reshape(1, D),
      jnp.concatenate([wr_hi, wr_lo], axis=1), wr_hi, br)

    nst = N // TS_MOE
    n_asg = 2 * TS_MOE
    per_tile = lambda a: a.reshape(2, nst, TS_MOE).transpose(1, 0, 2).reshape(nst, n_asg)
    eid = per_tile(route[0:2].astype(jnp.int32))
    wts = per_tile(route[2:4]).reshape(nst * n_asg)
    order = jnp.argsort(eid, axis=1, stable=True).astype(jnp.int32)
    slot = jnp.argsort(order, axis=1).astype(jnp.int32).reshape(nst * n_asg)
    tok_sorted = jnp.pad(order % TS_MOE, ((0, 0), (0, LIST_PAD))).reshape(
        nst * (n_asg + LIST_PAD))
    counts = tile_counts.reshape(nst, TS_MOE // TM_MIX, SUBLANES, LANES)[
        :, :, 0, N_GROUPS:N_GROUPS + N_EXPERTS].sum(axis=1).astype(jnp.int32)
    starts = (jnp.cumsum(counts, axis=1) - counts).reshape(nst * N_EXPERTS)
    counts = counts.reshape(nst * N_EXPERTS)

    weg = w_expert_gate[0].reshape(N_EXPERTS, D, EXPERT_HIDDEN).astype(BF16)
    weu = w_expert_up[0].reshape(N_EXPERTS, D, EXPERT_HIDDEN).astype(BF16)
    wed = w_expert_down[0].reshape(N_EXPERTS, EXPERT_HIDDEN, D).astype(BF16)
    smem = lambda n: pl.BlockSpec((n,), lambda s, e, *_: (s,), memory_space=pltpu.SMEM)
    y = pl.pallas_call(
        _moe_kernel,
        grid_spec=pltpu.PrefetchScalarGridSpec(
            num_scalar_prefetch=2,
            grid=(nst, N_EXPERTS),
            in_specs=[smem(n_asg + LIST_PAD), smem(n_asg), smem(n_asg),
                      pl.BlockSpec((TS_MOE * SUBLANES, LANES), lambda s, e, *_: (s, 0)),
                      pl.BlockSpec(memory_space=pl.ANY), pl.BlockSpec(memory_space=pl.ANY),
                      pl.BlockSpec(memory_space=pl.ANY)],
            out_specs=pl.BlockSpec((TS_MOE * SUBLANES, LANES), lambda s, e, *_: (s, 0)),
            scratch_shapes=[pltpu.VMEM(((n_asg + TMX_MOE) * SUBLANES, LANES), F32),
                            pltpu.VMEM((TMX_MOE * SUBLANES, LANES), F32),
                            pltpu.VMEM((TMX_MOE * SUBLANES, LANES), F32),
                            pltpu.VMEM((TMX_MOE * SUBLANES, LANES), F32),
                            pltpu.VMEM((MOE_WEIGHT_BUFFERS, D, EXPERT_HIDDEN), BF16),
                            pltpu.VMEM((MOE_WEIGHT_BUFFERS, D, EXPERT_HIDDEN), BF16),
                            pltpu.VMEM((MOE_WEIGHT_BUFFERS, EXPERT_HIDDEN, D), BF16),
                            pltpu.SemaphoreType.DMA((3, MOE_WEIGHT_BUFFERS))]),
        out_shape=jax.ShapeDtypeStruct((N * SUBLANES, LANES), F32),
        compiler_params=_params(2),
        name="moe",
    )(starts, counts, tok_sorted, slot, wts, u2, weg, weu, wed)

    ftok = pl.BlockSpec((TM_FINAL, D), lambda i: (i, 0))
    out = pl.pallas_call(
        _final_kernel,
        grid=(N // TM_FINAL,),
        in_specs=[ftok, pl.BlockSpec((TM_FINAL * SUBLANES, LANES), lambda i: (i, 0)),
                  _const_spec((1, D))],
        out_specs=ftok,
        out_shape=jax.ShapeDtypeStruct((N, D), F32),
        compiler_params=_params(1),
        name="final",
    )(h1, y, final_norm_g.reshape(1, D))
    return out.reshape(B, S, D)
```

```python
import functools

import jax
import jax.numpy as jnp
from jax import lax
from jax.experimental import pallas as pl
from jax.experimental.pallas import tpu as pltpu

F32 = jnp.float32
BF16 = jnp.bfloat16

D_MODEL = 1024
CHUNK = 64
N_META = 16
N_HEADS = 8
HEAD_DIM = 64
ATTN_WIDTH = N_HEADS * HEAD_DIM
KV_LATENT = 128
IDX_HEADS = 8
IDX_DIM = 32
TOPK_MAX = 256
ATTN_SCALE = HEAD_DIM ** -0.5
IDX_SCALE = (IDX_HEADS ** -0.5) * (IDX_DIM ** -0.5)
POOL_WINDOWS = (2, 4, 8, 16)
POOL_WIDTH = 512
POOL_GROUP = 128
N_GROUPS = 4
EXPERTS_PER_GROUP = 8
N_EXPERTS = N_GROUPS * EXPERTS_PER_GROUP
EXPERT_HIDDEN = 256
EPS = 1e-6

LANES = 128
SUBLANES = 8
W1_WIDTH = 1536
NEG = -1e30
POS = 1e30
VMEM_LIMIT = 56 * 1024 * 1024

TM_PROJ = 1024
TM_MIX = 1024
MIX_ROWS = 256
MIX_COLS = 256
QB = 256
KC = 256
MAX_BISECT = 40
BISECT_PER_CHECK = 4
FIRST_TIE_CHECK = 12
CT_ROWS = KV_LATENT + 16
LOG2E = 1.4426950408889634
TS_MOE = 2048
TMX_MOE = 160
MOE_WEIGHT_BUFFERS = 4
LIST_PAD = 1024
TM_FINAL = 1024


def _rms(x, g):
    return x * lax.rsqrt(jnp.mean(x * x, axis=-1, keepdims=True) + EPS) * g


def _dot(a, b):
    return jnp.dot(a, b, preferred_element_type=F32)


def _meta_kernel(m_ref, g1_ref, w1_ref, kvg_ref, c_ref, tail_ref, pv_ref):
    u = _rms(m_ref[...], g1_ref[...]).astype(BF16)
    p = _dot(u, w1_ref[...])
    c_ref[...] = _rms(p[:, 512:640], kvg_ref[...])
    pv_ref[...] = p[:, 896:1408]
    tail_ref[...] = p[:, 1408:1536]


def _proj_kernel(x_ref, g1_ref, w1_ref, kvg_ref, pvmeta_ref, wpool_ref, pscale_ref,
                 qt_ref, c_ref, ct_ref, iqt_ref, ik_ref, iwt_ref, yp_ref, ext_ref):
    tm = x_ref.shape[0]

    @pl.when(pl.program_id(1) == 0)
    def _():
        ext_ref[0:N_META, :] = pvmeta_ref[...]

    for r0 in range(0, tm, KC):
        rs = slice(r0, r0 + KC)
        u = _rms(x_ref[rs, :], g1_ref[...]).astype(BF16)
        p = _dot(u, w1_ref[...])
        qt_ref[:, rs] = p[:, 0:512].T.astype(BF16)
        c = _rms(p[:, 512:640], kvg_ref[...])
        c_ref[rs, :] = c.astype(BF16)
        ct_ref[r0 // KC] = jnp.concatenate(
            [c.T, jnp.ones((1, KC), F32), jnp.zeros((CT_ROWS - KV_LATENT - 1, KC), F32)],
            axis=0).astype(BF16)
        iqt_ref[:, rs] = p[:, 640:896].T.astype(BF16)
        tail = p[:, 1408:1536]
        ik_ref[rs, :] = tail[:, 0:IDX_DIM].astype(BF16)
        iwt_ref[:, rs] = tail.T[IDX_DIM:IDX_DIM + IDX_HEADS, :] * IDX_SCALE
        pv = p[:, 896:1408]
        ext_ref[N_META + r0:N_META + r0 + KC, :] = pv
        for g, w in enumerate(POOL_WINDOWS):
            cols = slice(g * POOL_GROUP, (g + 1) * POOL_GROUP)
            acc = pv[:, cols]
            for k in range(1, w):
                acc = acc + ext_ref[N_META + r0 - k:N_META + r0 - k + KC, cols]
            d = acc * (1.0 / w) - pv[:, cols]
            y = _dot(d.astype(BF16), wpool_ref[g]) * pscale_ref[:, cols]
            yp_ref[rs, cols] = y.astype(BF16)
    ext_ref[0:N_META, :] = ext_ref[tm:tm + N_META, :]


def _attn_kernel(ik_ref, ikm_ref, c_ref, cm_ref, ct_ref, cmt_ref, iqt_ref, iwt_ref, qt_ref,
                 wuk_ref, wuvt_ref, o_ref,
                 s_scr, s16_scr, qat_scr, lg_scr, m_scr, acc_scr, ot_scr, *, k_top):
    n_real = c_ref.shape[0]
    qb = iqt_ref.shape[1]
    j = pl.program_id(1)
    nkc = j + 1
    qchunk = (j * qb + lax.broadcasted_iota(jnp.int32, (1, qb), 1)) // CHUNK
    meta_rows = pl.ds(n_real, N_META)

    def rows(kc):
        return pl.ds(pl.multiple_of(kc * KC, KC), KC)

    def fold(x, op):
        groups = x.shape[0] // SUBLANES
        chains = 4 if groups % 4 == 0 else 1
        x = x.reshape(groups // chains, chains, SUBLANES, qb)
        return op(op(x, axis=0), axis=0)

    iqt_heads = [iqt_ref[h * IDX_DIM:(h + 1) * IDX_DIM, :] for h in range(IDX_HEADS)]

    def scores(ik_rows):
        acc = None
        for h in range(IDX_HEADS):
            t = jnp.maximum(_dot(ik_rows, iqt_heads[h]), 0.0) * iwt_ref[h:h + 1, :]
            acc = t if acc is None else acc + t
        return acc

    def score_body(kc, carry):
        mn, mx = carry
        sc = scores(ik_ref[rows(kc), :])
        s16_scr[rows(kc), :] = sc.astype(BF16)
        return jnp.minimum(mn, fold(sc, jnp.min)), jnp.maximum(mx, fold(sc, jnp.max))

    def score_pair(i, carry):
        return score_body(2 * i + 1, score_body(2 * i, carry))

    mn8, mx8 = lax.fori_loop(0, j // 2, score_pair, (jnp.full((SUBLANES, qb), POS, F32),
                                                     jnp.full((SUBLANES, qb), NEG, F32)))
    mn8, mx8 = lax.cond(j % 2 == 1, lambda c: score_body(j - 1, c), lambda c: c, (mn8, mx8))
    sc_all = scores(jnp.concatenate([ik_ref[rows(j), :], ikm_ref[...]], axis=0))
    sc = sc_all[0:KC, :]
    sm = sc_all[KC:KC + N_META, :]
    s16_scr[meta_rows, :] = sm.astype(BF16)
    mn0 = jnp.min(sm, axis=0, keepdims=True)
    mx0 = jnp.max(sm, axis=0, keepdims=True)
    adm = (j * KC + lax.broadcasted_iota(jnp.int32, (KC, 1), 0)) // CHUNK <= qchunk
    s16_scr[rows(j), :] = jnp.where(adm, sc, NEG).astype(BF16)
    mn8 = jnp.minimum(mn8, fold(jnp.where(adm, sc, POS), jnp.min))
    mx8 = jnp.maximum(mx8, fold(jnp.where(adm, sc, NEG), jnp.max))
    mn = jnp.minimum(mn0, jnp.min(mn8, axis=0, keepdims=True))
    mx = jnp.maximum(mx0, jnp.max(mx8, axis=0, keepdims=True))

    pack = 2 * SUBLANES

    def as16(v):
        return jnp.broadcast_to(v, (pack, qb)).astype(BF16)

    def tree(parts, op):
        while len(parts) > 1:
            parts = ([op(parts[i], parts[i + 1]) for i in range(0, len(parts) - 1, 2)]
                     + ([parts[-1]] if len(parts) % 2 else []))
        return parts[0]

    def groups(x):
        return [x[i * pack:(i + 1) * pack, :] for i in range(x.shape[0] // pack)]

    one16 = jnp.ones((pack, qb), BF16)
    zero16 = jnp.zeros((pack, qb), BF16)

    def count_ge(th):
        th16 = as16(th)

        def ones(x):
            return tree([jnp.where(g >= th16, one16, zero16) for g in groups(x)], jnp.add)

        def body(kc, cnt):
            return cnt + ones(s16_scr[rows(kc), :])
        cnt = lax.fori_loop(0, nkc, body, ones(s16_scr[meta_rows, :]))
        return jnp.sum(cnt.astype(F32), axis=0, keepdims=True)

    def band_extent(lo, hi):
        lo16, hi16 = as16(lo), as16(hi)
        pos16 = jnp.full((pack, qb), POS, BF16)
        neg16 = jnp.full((pack, qb), NEG, BF16)

        def ext(x):
            return (tree([jnp.where(g >= lo16, g, pos16) for g in groups(x)], jnp.minimum),
                    tree([jnp.where(g < hi16, g, neg16) for g in groups(x)], jnp.maximum))

        def body(kc, carry):
            bmin, bmax = ext(s16_scr[rows(kc), :])
            return jnp.minimum(carry[0], bmin), jnp.maximum(carry[1], bmax)
        bmin, bmax = lax.fori_loop(0, nkc, body, ext(s16_scr[meta_rows, :]))
        return (jnp.min(bmin.astype(F32), axis=0, keepdims=True),
                jnp.max(bmax.astype(F32), axis=0, keepdims=True))

    def any_lane(flags):
        return jnp.max(flags)

    def bisect_cond(carry):
        it, _, _, _, pending = carry
        return jnp.logical_and(it < MAX_BISECT, pending > 0)

    def bisect_body(carry):
        it, lo, hi, cnt_lo, _ = carry
        for _ in range(BISECT_PER_CHECK):
            mid = (lo + (hi - lo) * 0.5).astype(BF16).astype(F32)
            cnt = count_ge(mid)
            ge = cnt >= k_top
            lo = jnp.where(ge, mid, lo)
            hi = jnp.where(ge, hi, mid)
            cnt_lo = jnp.where(ge, cnt, cnt_lo)
        it = it + BISECT_PER_CHECK
        over = jnp.where(cnt_lo > k_top, 1, 0)
        pending = any_lane(over)

        def tied_check():
            bmin, bmax = band_extent(lo, hi)
            return any_lane(jnp.where(bmin < bmax, over, 0))

        pending = lax.cond(jnp.logical_and(pending > 0, it >= FIRST_TIE_CHECK),
                           tied_check, lambda: pending)
        return it, lo, hi, cnt_lo, pending

    n_adm = (N_META + CHUNK * (qchunk + 1)).astype(F32)
    lo0 = mn.astype(BF16).astype(F32)
    hi0 = (mx + jnp.abs(mx) * (2.0 ** -6) + 1e-30).astype(BF16).astype(F32)
    _, lo, hi, cnt_lo, _ = lax.while_loop(
        bisect_cond, bisect_body,
        (jnp.int32(0), lo0, hi0, n_adm, any_lane(jnp.where(n_adm > k_top, 1, 0))))
    pending = any_lane(jnp.where(cnt_lo > k_top, 1, 0))

    @pl.when(pending == 0)
    def _():
        def body(kc, _):
            s_scr[rows(kc), :] = jnp.where(s16_scr[rows(kc), :].astype(F32) >= lo, 0.0, NEG)
            return 0
        lax.fori_loop(0, nkc, body, 0)
        s_scr[meta_rows, :] = jnp.where(s16_scr[meta_rows, :].astype(F32) >= lo, 0.0, NEG)

    @pl.when(pending > 0)
    def _():
        need = k_top - count_ge(hi)

        def pick(sv, tri, before):
            band = jnp.where(sv >= lo, jnp.where(sv < hi, 1.0, 0.0), 0.0)
            rank = _dot(tri, band.astype(BF16)) + before
            take = jnp.where(rank <= need, band, 0.0)
            bias = jnp.where(sv >= hi, 0.0, jnp.where(take > 0.5, 0.0, NEG))
            return bias, before + jnp.sum(fold(band, jnp.sum), axis=0, keepdims=True)

        def tri(n):
            return jnp.where(lax.broadcasted_iota(jnp.int32, (n, n), 0)
                             >= lax.broadcasted_iota(jnp.int32, (n, n), 1), 1.0, 0.0).astype(BF16)

        bias_m, before = pick(s16_scr[meta_rows, :].astype(F32), tri(N_META),
                              jnp.zeros((1, qb), F32))
        s_scr[meta_rows, :] = bias_m
        tri_kc = tri(KC)

        def body(kc, before):
            bias, before = pick(s16_scr[rows(kc), :].astype(F32), tri_kc, before)
            s_scr[rows(kc), :] = bias
            return before
        lax.fori_loop(0, nkc, body, before)

    def head(h):
        return slice(h * qb, (h + 1) * qb)

    for h in range(N_HEADS):
        qat_scr[:, head(h)] = (_dot(wuk_ref[h], qt_ref[h * HEAD_DIM:(h + 1) * HEAD_DIM, :])
                               * (ATTN_SCALE * LOG2E)).astype(BF16)
    m_scr[...] = jnp.full(m_scr.shape, 0.5 * NEG, F32)
    acc_scr[...] = jnp.zeros(acc_scr.shape, F32)

    def attend(c_rows, bias, values):
        r = c_rows.shape[0]
        lg_scr[0:r, :] = _dot(c_rows, qat_scr[...])
        for h in range(N_HEADS):
            lg = lg_scr[0:r, head(h)] + bias
            m_old = m_scr[:, head(h)]
            m_new = jnp.maximum(m_old, jnp.max(lg, axis=0, keepdims=True))
            p = jnp.exp2(lg - m_new).astype(BF16)
            m_scr[:, head(h)] = m_new
            pv = None
            for ct_cols, rs in values:
                t = _dot(ct_cols, p[rs, :])
                pv = t if pv is None else pv + t
            acc_scr[:, head(h)] = acc_scr[:, head(h)] * jnp.exp2(m_old - m_new) + pv

    def attend_chunk(kc):
        attend(c_ref[rows(kc), :], s_scr[rows(kc), :], [(ct_ref[kc], slice(0, KC))])

    def attend_pair(i, _):
        attend_chunk(2 * i)
        attend_chunk(2 * i + 1)
        return 0
    lax.fori_loop(0, j // 2, attend_pair, 0)

    @pl.when(j % 2 == 1)
    def _():
        attend_chunk(j - 1)

    attend(jnp.concatenate([c_ref[rows(j), :], cm_ref[...]], axis=0),
           jnp.concatenate([s_scr[rows(j), :], s_scr[meta_rows, :]], axis=0),
           [(ct_ref[j], slice(0, KC)), (cmt_ref[...], slice(KC, KC + N_META))])

    for h in range(N_HEADS):
        olat = acc_scr[0:KV_LATENT, head(h)] / acc_scr[KV_LATENT:KV_LATENT + 1, head(h)]
        ot_scr[h * HEAD_DIM:(h + 1) * HEAD_DIM, :] = _dot(wuvt_ref[h], olat.astype(BF16))
    o_ref[...] = ot_scr[...].T.astype(BF16)


def _mix_kernel(x_ref, g1_ref, wgate_ref, attn_ref, yp_ref, wba_ref, wbp_ref, wout_ref,
                g2_ref, wr_ref, wrhi_ref, br_ref, h1_ref, u2_ref, route_ref, count_ref,
                merged_scr):
    nsub = D_MODEL // LANES
    counts = None
    for r0 in range(0, x_ref.shape[0], MIX_ROWS):
        rows = pl.ds(r0, MIX_ROWS)
        c = _mix_group(x_ref.at[rows], g1_ref, wgate_ref, attn_ref.at[rows], yp_ref.at[rows],
                       wba_ref, wbp_ref, wout_ref, g2_ref, wr_ref, wrhi_ref, br_ref,
                       h1_ref.at[rows], u2_ref.at[pl.ds(r0 * nsub, MIX_ROWS * nsub)],
                       route_ref.at[:, rows], merged_scr.at[rows])
        counts = c if counts is None else counts + c
    count_ref[...] = jnp.broadcast_to(counts, count_ref.shape)


def _mix_group(x_ref, g1_ref, wgate_ref, attn_ref, yp_ref, wba_ref, wbp_ref, wout_ref,
               g2_ref, wr_ref, wrhi_ref, br_ref, h1_ref, u2_ref, route_ref, merged_scr):
    x = x_ref[...]
    u = _rms(x, g1_ref[...]).astype(BF16)
    attn = attn_ref[...]
    yp = yp_ref[...]
    for cb in range(D_MODEL // MIX_COLS):
        ca = slice(cb * MIX_COLS, (cb + 1) * MIX_COLS)
        cp = slice(D_MODEL + cb * MIX_COLS, D_MODEL + (cb + 1) * MIX_COLS)
        g_attn = 1.0 / (1.0 + jnp.exp(-_dot(u, wgate_ref[:, ca])))
        g_pool = 1.0 / (1.0 + jnp.exp(-_dot(u, wgate_ref[:, cp])))
        merged_scr[:, ca] = (g_attn * _dot(attn, wba_ref[:, ca])
                             + g_pool * _dot(yp, wbp_ref[:, ca])).astype(BF16)
    h1 = x + _dot(merged_scr[...], wout_ref[...])
    h1_ref[...] = h1
    u2 = _rms(h1, g2_ref[...])
    nsub = D_MODEL // LANES
    for s in range(nsub):
        u2_ref[pl.ds(s, x.shape[0], stride=nsub), :] = u2[:, s * LANES:(s + 1) * LANES]
    u2_hi = u2.astype(BF16)

    u2_lo = (u2 - u2_hi.astype(F32)).astype(BF16)
    hi_both = _dot(u2_hi, wr_ref[...])
    lg = hi_both[:, :LANES] + hi_both[:, LANES:] + _dot(u2_lo, wrhi_ref[...]) + br_ref[...]
    lane = lax.broadcasted_iota(jnp.int32, lg.shape, 1)
    is_g = lane < N_GROUPS
    gl = jnp.where(is_g, lg, NEG)
    gmax = jnp.max(gl, axis=1, keepdims=True)
    gidx = jnp.min(jnp.where(gl == gmax, lane, LANES), axis=1, keepdims=True)
    p_g = 1.0 / jnp.sum(jnp.where(is_g, jnp.exp(gl - gmax), 0.0), axis=1, keepdims=True)
    e_lane = lane - N_GROUPS
    lane_grp = jnp.where(e_lane >= 0,
                         jnp.where(e_lane < N_EXPERTS, e_lane // EXPERTS_PER_GROUP, -1), -1)
    in_grp = lane_grp == gidx
    el = jnp.where(in_grp, lg, NEG)
    t1 = jnp.max(el, axis=1, keepdims=True)
    i1 = jnp.min(jnp.where(el == t1, lane, LANES), axis=1, keepdims=True)
    el2 = jnp.where(lane == i1, NEG, el)
    t2 = jnp.max(el2, axis=1, keepdims=True)
    i2 = jnp.min(jnp.where(el2 == t2, lane, LANES), axis=1, keepdims=True)
    r = jnp.exp(t2 - t1)
    p1 = 1.0 / (1.0 + r)
    p2 = r * p1
    e1 = (i1 - N_GROUPS).astype(F32)
    e2 = (i2 - N_GROUPS).astype(F32)
    record = jnp.where(lane == 0, e1, jnp.where(lane == 1, e2, jnp.where(
        lane == 2, p1 * p_g, jnp.where(lane == 3, p2 * p_g, 0.0))))
    route_ref[...] = record.T[0:SUBLANES, :]
    chosen = jnp.where(lane == i1, 1.0, jnp.where(lane == i2, 1.0, 0.0))
    return jnp.sum(chosen, axis=0, keepdims=True)


def _moe_kernel(start_ref, cnt_ref, tok_ref, slot_ref, wt_ref, x2_ref, wg_hbm, wu_hbm, wd_hbm,
                y2_ref, r2_scr, ga_scr, gb_scr, rt_scr, wg_buf, wu_buf, wd_buf, w_sem):
    ts = x2_ref.shape[0] // SUBLANES
    tmx = ga_scr.shape[0] // SUBLANES
    st = pl.program_id(0)
    e = pl.program_id(1)
    n_e = pl.num_programs(1)
    nsub = D_MODEL // LANES
    step = st * n_e + e
    n_steps = pl.num_programs(0) * n_e

    def weight_copies(g):
        ex = g % n_e
        buf = g % MOE_WEIGHT_BUFFERS
        return [pltpu.make_async_copy(hbm.at[ex], vbuf.at[buf], w_sem.at[k, buf])
                for k, (hbm, vbuf) in enumerate(((wg_hbm, wg_buf), (wu_hbm, wu_buf),
                                                 (wd_hbm, wd_buf)))]

    @pl.when(step == 0)
    def _():
        for g in range(MOE_WEIGHT_BUFFERS - 1):
            for cp in weight_copies(g):
                cp.start()

    @pl.when(step + MOE_WEIGHT_BUFFERS - 1 < n_steps)
    def _():
        for cp in weight_copies(step + MOE_WEIGHT_BUFFERS - 1):
            cp.start()

    for cp in weight_copies(step):
        cp.wait()
    wbuf = step % MOE_WEIGHT_BUFFERS

    def vreg_rows(i):
        return pl.ds(pl.multiple_of(i * SUBLANES, SUBLANES), SUBLANES)

    def gather(g_scr, base):
        def gather_body(r8, _):
            for u in range(SUBLANES):
                r = r8 * SUBLANES + u
                g_scr[vreg_rows(r), :] = x2_ref[vreg_rows(tok_ref[base + r]), :]
            return 0
        lax.fori_loop(0, tmx // SUBLANES, gather_body, 0)

    def expert_ffn(g_scr, base):
        xg = jnp.concatenate([g_scr[pl.ds(s, tmx, stride=nsub), :] for s in range(nsub)],
                             axis=1).astype(BF16)
        a = _dot(xg, wg_buf[wbuf])
        b = _dot(xg, wu_buf[wbuf])
        hg = (a * (1.0 / (1.0 + jnp.exp(-a))) * b).astype(BF16)
        yr = _dot(hg, wd_buf[wbuf])
        for s in range(nsub):
            rt_scr[pl.ds(s, tmx, stride=nsub), :] = yr[:, s * LANES:(s + 1) * LANES]
        r2_scr[pl.ds(pl.multiple_of(base * SUBLANES, SUBLANES), tmx * SUBLANES), :] = rt_scr[...]

    start = start_ref[step]
    cnt = cnt_ref[step]

    @pl.when(e == 0)
    def _():
        gather(ga_scr, start)

    def run(cur_scr, nxt_scr):
        expert_ffn(cur_scr, start)
        if nxt_scr is not None:
            nxt = start_ref[step + 1]
            for r in range(tmx):
                nxt_scr[r * SUBLANES:(r + 1) * SUBLANES, :] = (
                    x2_ref[vreg_rows(tok_ref[nxt + r]), :])

        def extra_tile(i, _):
            gather(cur_scr, start + i * tmx)
            expert_ffn(cur_scr, start + i * tmx)
            return 0
        lax.fori_loop(1, (cnt + tmx - 1) // tmx, extra_tile, 0)

    last = e == n_e - 1
    even = e % 2 == 0

    @pl.when(jnp.logical_and(even, jnp.logical_not(last)))
    def _():
        run(ga_scr, gb_scr)

    @pl.when(jnp.logical_and(jnp.logical_not(even), jnp.logical_not(last)))
    def _():
        run(gb_scr, ga_scr)

    @pl.when(last)
    def _():
        run(ga_scr if (N_EXPERTS - 1) % 2 == 0 else gb_scr, None)

        def combine_body(t8, _):
            for u in range(SUBLANES):
                t = t8 * SUBLANES + u
                y2_ref[vreg_rows(t), :] = (
                    wt_ref[t] * r2_scr[vreg_rows(slot_ref[t]), :]
                    + wt_ref[ts + t] * r2_scr[vreg_rows(slot_ref[ts + t]), :])
            return 0
        lax.fori_loop(0, ts // SUBLANES, combine_body, 0)


def _final_kernel(h1_ref, y2_ref, gf_ref, o_ref):
    nsub = D_MODEL // LANES
    y = jnp.concatenate([y2_ref[pl.ds(s, h1_ref.shape[0], stride=nsub), :] for s in range(nsub)],
                        axis=1)
    o_ref[...] = _rms(h1_ref[...] + y, gf_ref[...])


def _const_spec(shape):
    nd = len(shape)
    return pl.BlockSpec(shape, lambda *_: (0,) * nd, pipeline_mode=pl.Buffered(1))


def _params(n_axes):
    return pltpu.CompilerParams(dimension_semantics=("arbitrary",) * n_axes,
                                vmem_limit_bytes=VMEM_LIMIT)


def kernel(x, meta_tokens, norm1_g, w_in, kv_norm_g, w_uk, w_uv, w_pool, pool_scale,
           w_branch_attn, w_branch_pool, w_out, norm2_g, w_group_router, b_group_router,
           w_expert_router, b_expert_router, w_expert_gate, w_expert_up, w_expert_down,
           final_norm_g):
    B, S, D = x.shape
    assert D == D_MODEL and S % QB == 0 and S % TM_PROJ == 0 and w_in.shape[0] == 1
    assert QB == KC and TM_PROJ % KC == 0 and QB % CHUNK == 0
    N = B * S
    k_top = min(TOPK_MAX, S // 4)
    xr = x.reshape(N, D)

    wi = w_in[0]
    w1 = jnp.concatenate(
        [wi[:, 0:640], wi[:, 640:896], wi[:, 936:1448], wi[:, 896:936],
         jnp.zeros((D, W1_WIDTH - 1448), F32)], axis=1).astype(BF16)
    wgate = wi[:, 1448:].astype(BF16)
    g1 = norm1_g[0].reshape(1, D)
    kvg = kv_norm_g[0].reshape(1, KV_LATENT)
    wpool = w_pool[0].astype(BF16)
    pscale = pool_scale[0].reshape(1, POOL_WIDTH)
    wuk = jnp.transpose(w_uk[0], (1, 0, 2)).astype(BF16)
    wuvt = jnp.transpose(w_uv[0], (1, 2, 0)).astype(BF16)
    wr = jnp.concatenate(
        [w_group_router[0], w_expert_router[0].reshape(D, N_EXPERTS),
         jnp.zeros((D, LANES - N_GROUPS - N_EXPERTS), F32)], axis=1)
    br = jnp.concatenate(
        [b_group_router[0], b_expert_router[0].reshape(N_EXPERTS),
         jnp.zeros((LANES - N_GROUPS - N_EXPERTS,), F32)]).reshape(1, LANES)

    c_m, tail_m, pv_m = pl.pallas_call(
        _meta_kernel,
        out_shape=(jax.ShapeDtypeStruct((N_META, KV_LATENT), F32),
                   jax.ShapeDtypeStruct((N_META, LANES), F32),
                   jax.ShapeDtypeStruct((N_META, POOL_WIDTH), F32)),
        name="meta",
    )(meta_tokens, g1, w1, kvg)
    cm = c_m.astype(BF16)
    cmt = jnp.concatenate([cm.T, jnp.ones((1, N_META), BF16),
                           jnp.zeros((CT_ROWS - KV_LATENT - 1, N_META), BF16)], axis=0)
    ikm = tail_m[:, :IDX_DIM].astype(BF16)

    tpb = S // TM_PROJ
    tok = lambda w: pl.BlockSpec((TM_PROJ, w), lambda b, i: (b * tpb + i, 0))
    tok_t = lambda w: pl.BlockSpec((w, TM_PROJ), lambda b, i: (0, b * tpb + i))
    qt, c, ct, iqt, ik, iwt, yp = pl.pallas_call(
        _proj_kernel,
        grid=(B, tpb),
        in_specs=[tok(D), _const_spec((1, D)), _const_spec((D, W1_WIDTH)),
                  _const_spec((1, KV_LATENT)), _const_spec((N_META, POOL_WIDTH)),
                  _const_spec((len(POOL_WINDOWS), POOL_GROUP, POOL_GROUP)),
                  _const_spec((1, POOL_WIDTH))],
        out_specs=[tok_t(ATTN_WIDTH), tok(KV_LATENT),
                   pl.BlockSpec((None, TM_PROJ // KC, CT_ROWS, KC), lambda b, i: (b, i, 0, 0)),
                   tok_t(IDX_HEADS * IDX_DIM), tok(IDX_DIM), tok_t(IDX_HEADS), tok(POOL_WIDTH)],
        out_shape=(jax.ShapeDtypeStruct((ATTN_WIDTH, N), BF16),
                   jax.ShapeDtypeStruct((N, KV_LATENT), BF16),
                   jax.ShapeDtypeStruct((B, S // KC, CT_ROWS, KC), BF16),
                   jax.ShapeDtypeStruct((IDX_HEADS * IDX_DIM, N), BF16),
                   jax.ShapeDtypeStruct((N, IDX_DIM), BF16),
                   jax.ShapeDtypeStruct((IDX_HEADS, N), F32),
                   jax.ShapeDtypeStruct((N, POOL_WIDTH), BF16)),
        scratch_shapes=[pltpu.VMEM((TM_PROJ + N_META, POOL_WIDTH), F32)],
        compiler_params=_params(2),
        name="proj",
    )(xr, g1, w1, kvg, pv_m, wpool, pscale)

    nqb = S // QB
    qcol = lambda w: pl.BlockSpec((w, QB), lambda b, j: (0, b * nqb + j))
    attn = pl.pallas_call(
        functools.partial(_attn_kernel, k_top=float(k_top)),
        grid=(B, nqb),
        in_specs=[pl.BlockSpec((S, IDX_DIM), lambda b, j: (b, 0)),
                  _const_spec((N_META, IDX_DIM)),
                  pl.BlockSpec((S, KV_LATENT), lambda b, j: (b, 0)),
                  _const_spec((N_META, KV_LATENT)),
                  pl.BlockSpec((None, S // KC, CT_ROWS, KC), lambda b, j: (b, 0, 0, 0)),
                  _const_spec((CT_ROWS, N_META)),
                  qcol(IDX_HEADS * IDX_DIM), qcol(IDX_HEADS), qcol(ATTN_WIDTH),
                  _const_spec((N_HEADS, KV_LATENT, HEAD_DIM)),
                  _const_spec((N_HEADS, HEAD_DIM, KV_LATENT))],
        out_specs=pl.BlockSpec((QB, ATTN_WIDTH), lambda b, j: (b * nqb + j, 0)),
        out_shape=jax.ShapeDtypeStruct((N, ATTN_WIDTH), BF16),
        scratch_shapes=[pltpu.VMEM((S + N_META, QB), F32), pltpu.VMEM((S + N_META, QB), BF16),
                        pltpu.VMEM((KV_LATENT, N_HEADS * QB), BF16),
                        pltpu.VMEM((KC + N_META, N_HEADS * QB), F32),
                        pltpu.VMEM((1, N_HEADS * QB), F32),
                        pltpu.VMEM((CT_ROWS, N_HEADS * QB), F32),
                        pltpu.VMEM((ATTN_WIDTH, QB), F32)],
        compiler_params=_params(2),
        name="attn",
    )(ik, ikm, c, cm, ct, cmt, iqt, iwt, qt, wuk, wuvt)

    wr_hi = wr.astype(BF16)
    wr_lo = (wr - wr_hi.astype(F32)).astype(BF16)
    tok1 = lambda w: pl.BlockSpec((TM_MIX, w), lambda i: (i, 0))
    h1, u2, route, tile_counts = pl.pallas_call(
        _mix_kernel,
        grid=(N // TM_MIX,),
        in_specs=[tok1(D), _const_spec((1, D)), _const_spec((D, 2 * D)), tok1(ATTN_WIDTH),
                  tok1(POOL_WIDTH), _const_spec((ATTN_WIDTH, D)), _const_spec((POOL_WIDTH, D)),
                  _const_spec((D, D)), _const_spec((1, D)), _const_spec((D, 2 * LANES)),
                  _const_spec((D, LANES)), _const_spec((1, LANES))],
        out_specs=[tok1(D), pl.BlockSpec((TM_MIX * SUBLANES, LANES), lambda i: (i, 0)),
                   pl.BlockSpec((SUBLANES, TM_MIX), lambda i: (0, i)),
                   pl.BlockSpec((SUBLANES, LANES), lambda i: (i, 0))],
        out_shape=(jax.ShapeDtypeStruct((N, D), F32),
                   jax.ShapeDtypeStruct((N * SUBLANES, LANES), F32),
                   jax.ShapeDtypeStruct((SUBLANES, N), F32),
                   jax.ShapeDtypeStruct((N // TM_MIX * SUBLANES, LANES), F32)),
        scratch_shapes=[pltpu.VMEM((TM_MIX, D), BF16)],
        compiler_params=_params(1),
        name="mix",
    )(xr, g1, wgate, attn, yp, w_branch_attn[0].astype(BF16), w_branch_pool[0].astype(BF16),
      w_out[0].astype(BF16), norm2_g[0].reshape(1, D),
      jnp.concatenate([wr_hi, wr_lo], axis=1), wr_hi, br)

    nst = N // TS_MOE
    n_asg = 2 * TS_MOE
    per_tile = lambda a: a.reshape(2, nst, TS_MOE).transpose(1, 0, 2).reshape(nst, n_asg)
    eid = per_tile(route[0:2].astype(jnp.int32))
    wts = per_tile(route[2:4]).reshape(nst * n_asg)
    order = jnp.argsort(eid, axis=1, stable=True).astype(jnp.int32)
    slot = jnp.argsort(order, axis=1).astype(jnp.int32).reshape(nst * n_asg)
    tok_sorted = jnp.pad(order % TS_MOE, ((0, 0), (0, LIST_PAD))).reshape(
        nst * (n_asg + LIST_PAD))
    counts = tile_counts.reshape(nst, TS_MOE // TM_MIX, SUBLANES, LANES)[
        :, :, 0, N_GROUPS:N_GROUPS + N_EXPERTS].sum(axis=1).astype(jnp.int32)
    starts = (jnp.cumsum(counts, axis=1) - counts).reshape(nst * N_EXPERTS)
    counts = counts.reshape(nst * N_EXPERTS)

    weg = w_expert_gate[0].reshape(N_EXPERTS, D, EXPERT_HIDDEN).astype(BF16)
    weu = w_expert_up[0].reshape(N_EXPERTS, D, EXPERT_HIDDEN).astype(BF16)
    wed = w_expert_down[0].reshape(N_EXPERTS, EXPERT_HIDDEN, D).astype(BF16)
    smem = lambda n: pl.BlockSpec((n,), lambda s, e, *_: (s,), memory_space=pltpu.SMEM)
    y = pl.pallas_call(
        _moe_kernel,
        grid_spec=pltpu.PrefetchScalarGridSpec(
            num_scalar_prefetch=2,
            grid=(nst, N_EXPERTS),
            in_specs=[smem(n_asg + LIST_PAD), smem(n_asg), smem(n_asg),
                      pl.BlockSpec((TS_MOE * SUBLANES, LANES), lambda s, e, *_: (s, 0)),
                      pl.BlockSpec(memory_space=pl.ANY), pl.BlockSpec(memory_space=pl.ANY),
                      pl.BlockSpec(memory_space=pl.ANY)],
            out_specs=pl.BlockSpec((TS_MOE * SUBLANES, LANES), lambda s, e, *_: (s, 0),
                                   pipeline_mode=pl.Buffered(1)),
            scratch_shapes=[pltpu.VMEM(((n_asg + TMX_MOE) * SUBLANES, LANES), F32),
                            pltpu.VMEM((TMX_MOE * SUBLANES, LANES), F32),
                            pltpu.VMEM((TMX_MOE * SUBLANES, LANES), F32),
                            pltpu.VMEM((TMX_MOE * SUBLANES, LANES), F32),
                            pltpu.VMEM((MOE_WEIGHT_BUFFERS, D, EXPERT_HIDDEN), BF16),
                            pltpu.VMEM((MOE_WEIGHT_BUFFERS, D, EXPERT_HIDDEN), BF16),
                            pltpu.VMEM((MOE_WEIGHT_BUFFERS, EXPERT_HIDDEN, D), BF16),
                            pltpu.SemaphoreType.DMA((3, MOE_WEIGHT_BUFFERS))]),
        out_shape=jax.ShapeDtypeStruct((N * SUBLANES, LANES), F32),
        compiler_params=_params(2),
        name="moe",
    )(starts, counts, tok_sorted, slot, wts, u2, weg, weu, wed)

    ftok = pl.BlockSpec((TM_FINAL, D), lambda i: (i, 0))
    out = pl.pallas_call(
        _final_kernel,
        grid=(N // TM_FINAL,),
        in_specs=[ftok, pl.BlockSpec((TM_FINAL * SUBLANES, LANES), lambda i: (i, 0)),
                  _const_spec((1, D))],
        out_specs=ftok,
        out_shape=jax.ShapeDtypeStruct((N, D), F32),
        compiler_params=_params(1),
        name="final",
    )(h1, y, final_norm_g.reshape(1, D))
    return out.reshape(B, S, D)
```

```python
import functools

import jax
import jax.numpy as jnp
from jax import lax
from jax.experimental import pallas as pl
from jax.experimental.pallas import tpu as pltpu

F32 = jnp.float32
BF16 = jnp.bfloat16

D_MODEL = 1024
CHUNK = 64
N_META = 16
N_HEADS = 8
HEAD_DIM = 64
ATTN_WIDTH = N_HEADS * HEAD_DIM
KV_LATENT = 128
IDX_HEADS = 8
IDX_DIM = 32
TOPK_MAX = 256
ATTN_SCALE = HEAD_DIM ** -0.5
IDX_SCALE = (IDX_HEADS ** -0.5) * (IDX_DIM ** -0.5)
POOL_WINDOWS = (2, 4, 8, 16)
POOL_WIDTH = 512
POOL_GROUP = 128
N_GROUPS = 4
EXPERTS_PER_GROUP = 8
N_EXPERTS = N_GROUPS * EXPERTS_PER_GROUP
EXPERT_HIDDEN = 256
EPS = 1e-6

LANES = 128
SUBLANES = 8
W1_WIDTH = 1536
NEG = -1e30
POS = 1e30
VMEM_LIMIT = 56 * 1024 * 1024

TM_PROJ = 1024
TM_MIX = 1024
MIX_ROWS = 256
MIX_COLS = 256
QB = 256
KC = 256
MAX_BISECT = 40
BISECT_PER_CHECK = 4
FIRST_TIE_CHECK = 12
CT_ROWS = KV_LATENT + 16
LOG2E = 1.4426950408889634
TS_MOE = 2048
TMX_MOE = 160
MOE_WEIGHT_BUFFERS = 3
LIST_PAD = 1024
TM_FINAL = 1024


def _rms(x, g):
    return x * lax.rsqrt(jnp.mean(x * x, axis=-1, keepdims=True) + EPS) * g


def _dot(a, b):
    return jnp.dot(a, b, preferred_element_type=F32)


def _meta_kernel(m_ref, g1_ref, w1_ref, kvg_ref, c_ref, tail_ref, pv_ref):
    u = _rms(m_ref[...], g1_ref[...]).astype(BF16)
    p = _dot(u, w1_ref[...])
    c_ref[...] = _rms(p[:, 512:640], kvg_ref[...])
    pv_ref[...] = p[:, 896:1408]
    tail_ref[...] = p[:, 1408:1536]


def _proj_kernel(x_ref, g1_ref, w1_ref, kvg_ref, pvmeta_ref, wpool_ref, pscale_ref,
                 qt_ref, c_ref, ct_ref, iqt_ref, ik_ref, iwt_ref, yp_ref, ext_ref):
    tm = x_ref.shape[0]

    @pl.when(pl.program_id(1) == 0)
    def _():
        ext_ref[0:N_META, :] = pvmeta_ref[...]

    for r0 in range(0, tm, KC):
        rs = slice(r0, r0 + KC)
        u = _rms(x_ref[rs, :], g1_ref[...]).astype(BF16)
        p = _dot(u, w1_ref[...])
        qt_ref[:, rs] = p[:, 0:512].T.astype(BF16)
        c = _rms(p[:, 512:640], kvg_ref[...])
        c_ref[rs, :] = c.astype(BF16)
        ct_ref[r0 // KC] = jnp.concatenate(
            [c.T, jnp.ones((1, KC), F32), jnp.zeros((CT_ROWS - KV_LATENT - 1, KC), F32)],
            axis=0).astype(BF16)
        iqt_ref[:, rs] = p[:, 640:896].T.astype(BF16)
        tail = p[:, 1408:1536]
        ik_ref[rs, :] = tail[:, 0:IDX_DIM].astype(BF16)
        iwt_ref[:, rs] = tail.T[IDX_DIM:IDX_DIM + IDX_HEADS, :] * IDX_SCALE
        pv = p[:, 896:1408]
        ext_ref[N_META + r0:N_META + r0 + KC, :] = pv
        for g, w in enumerate(POOL_WINDOWS):
            cols = slice(g * POOL_GROUP, (g + 1) * POOL_GROUP)
            acc = pv[:, cols]
            for k in range(1, w):
                acc = acc + ext_ref[N_META + r0 - k:N_META + r0 - k + KC, cols]
            d = acc * (1.0 / w) - pv[:, cols]
            y = _dot(d.astype(BF16), wpool_ref[g]) * pscale_ref[:, cols]
            yp_ref[rs, cols] = y.astype(BF16)
    ext_ref[0:N_META, :] = ext_ref[tm:tm + N_META, :]


def _attn_kernel(ik_ref, ikm_ref, c_ref, cm_ref, ct_ref, cmt_ref, iqt_ref, iwt_ref, qt_ref,
                 wuk_ref, wuvt_ref, o_ref,
                 s_scr, s16_scr, qat_scr, lg_scr, m_scr, acc_scr, ot_scr, *, k_top):
    n_real = c_ref.shape[0]
    qb = iqt_ref.shape[1]
    j = pl.program_id(1)
    nkc = j + 1
    qchunk = (j * qb + lax.broadcasted_iota(jnp.int32, (1, qb), 1)) // CHUNK
    meta_rows = pl.ds(n_real, N_META)

    def rows(kc):
        return pl.ds(pl.multiple_of(kc * KC, KC), KC)

    def fold(x, op):
        groups = x.shape[0] // SUBLANES
        chains = 4 if groups % 4 == 0 else 1
        x = x.reshape(groups // chains, chains, SUBLANES, qb)
        return op(op(x, axis=0), axis=0)

    iqt_heads = [iqt_ref[h * IDX_DIM:(h + 1) * IDX_DIM, :] for h in range(IDX_HEADS)]

    def scores(ik_rows):
        acc = None
        for h in range(IDX_HEADS):
            t = jnp.maximum(_dot(ik_rows, iqt_heads[h]), 0.0) * iwt_ref[h:h + 1, :]
            acc = t if acc is None else acc + t
        return acc

    def score_body(kc, carry):
        mn, mx = carry
        sc = scores(ik_ref[rows(kc), :])
        s16_scr[rows(kc), :] = sc.astype(BF16)
        return jnp.minimum(mn, fold(sc, jnp.min)), jnp.maximum(mx, fold(sc, jnp.max))

    def score_pair(i, carry):
        return score_body(2 * i + 1, score_body(2 * i, carry))

    mn8, mx8 = lax.fori_loop(0, j // 2, score_pair, (jnp.full((SUBLANES, qb), POS, F32),
                                                     jnp.full((SUBLANES, qb), NEG, F32)))
    mn8, mx8 = lax.cond(j % 2 == 1, lambda c: score_body(j - 1, c), lambda c: c, (mn8, mx8))
    sc_all = scores(jnp.concatenate([ik_ref[rows(j), :], ikm_ref[...]], axis=0))
    sc = sc_all[0:KC, :]
    sm = sc_all[KC:KC + N_META, :]
    s16_scr[meta_rows, :] = sm.astype(BF16)
    mn0 = jnp.min(sm, axis=0, keepdims=True)
    mx0 = jnp.max(sm, axis=0, keepdims=True)
    adm = (j * KC + lax.broadcasted_iota(jnp.int32, (KC, 1), 0)) // CHUNK <= qchunk
    s16_scr[rows(j), :] = jnp.where(adm, sc, NEG).astype(BF16)
    mn8 = jnp.minimum(mn8, fold(jnp.where(adm, sc, POS), jnp.min))
    mx8 = jnp.maximum(mx8, fold(jnp.where(adm, sc, NEG), jnp.max))
    mn = jnp.minimum(mn0, jnp.min(mn8, axis=0, keepdims=True))
    mx = jnp.maximum(mx0, jnp.max(mx8, axis=0, keepdims=True))

    pack = 2 * SUBLANES

    def as16(v):
        return jnp.broadcast_to(v, (pack, qb)).astype(BF16)

    def tree(parts, op):
        while len(parts) > 1:
            parts = ([op(parts[i], parts[i + 1]) for i in range(0, len(parts) - 1, 2)]
                     + ([parts[-1]] if len(parts) % 2 else []))
        return parts[0]

    def groups(x):
        return [x[i * pack:(i + 1) * pack, :] for i in range(x.shape[0] // pack)]

    one16 = jnp.ones((pack, qb), BF16)
    zero16 = jnp.zeros((pack, qb), BF16)

    def count_ge(th):
        th16 = as16(th)

        def ones(x):
            return tree([jnp.where(g >= th16, one16, zero16) for g in groups(x)], jnp.add)

        def body(kc, cnt):
            return cnt + ones(s16_scr[rows(kc), :])
        cnt = lax.fori_loop(0, nkc, body, ones(s16_scr[meta_rows, :]))
        return jnp.sum(cnt.astype(F32), axis=0, keepdims=True)

    def band_extent(lo, hi):
        lo16, hi16 = as16(lo), as16(hi)
        pos16 = jnp.full((pack, qb), POS, BF16)
        neg16 = jnp.full((pack, qb), NEG, BF16)

        def ext(x):
            return (tree([jnp.where(g >= lo16, g, pos16) for g in groups(x)], jnp.minimum),
                    tree([jnp.where(g < hi16, g, neg16) for g in groups(x)], jnp.maximum))

        def body(kc, carry):
            bmin, bmax = ext(s16_scr[rows(kc), :])
            return jnp.minimum(carry[0], bmin), jnp.maximum(carry[1], bmax)
        bmin, bmax = lax.fori_loop(0, nkc, body, ext(s16_scr[meta_rows, :]))
        return (jnp.min(bmin.astype(F32), axis=0, keepdims=True),
                jnp.max(bmax.astype(F32), axis=0, keepdims=True))

    def any_lane(flags):
        return jnp.max(flags)

    def bisect_cond(carry):
        it, _, _, _, pending = carry
        return jnp.logical_and(it < MAX_BISECT, pending > 0)

    def bisect_body(carry):
        it, lo, hi, cnt_lo, _ = carry
        for _ in range(BISECT_PER_CHECK):
            mid = (lo + (hi - lo) * 0.5).astype(BF16).astype(F32)
            cnt = count_ge(mid)
            ge = cnt >= k_top
            lo = jnp.where(ge, mid, lo)
            hi = jnp.where(ge, hi, mid)
            cnt_lo = jnp.where(ge, cnt, cnt_lo)
        it = it + BISECT_PER_CHECK
        over = jnp.where(cnt_lo > k_top, 1, 0)
        pending = any_lane(over)

        def tied_check():
            bmin, bmax = band_extent(lo, hi)
            return any_lane(jnp.where(bmin < bmax, over, 0))

        pending = lax.cond(jnp.logical_and(pending > 0, it >= FIRST_TIE_CHECK),
                           tied_check, lambda: pending)
        return it, lo, hi, cnt_lo, pending

    n_adm = (N_META + CHUNK * (qchunk + 1)).astype(F32)
    lo0 = mn.astype(BF16).astype(F32)
    hi0 = (mx + jnp.abs(mx) * (2.0 ** -6) + 1e-30).astype(BF16).astype(F32)
    _, lo, hi, cnt_lo, _ = lax.while_loop(
        bisect_cond, bisect_body,
        (jnp.int32(0), lo0, hi0, n_adm, any_lane(jnp.where(n_adm > k_top, 1, 0))))
    pending = any_lane(jnp.where(cnt_lo > k_top, 1, 0))

    @pl.when(pending == 0)
    def _():
        def body(kc, _):
            s_scr[rows(kc), :] = jnp.where(s16_scr[rows(kc), :].astype(F32) >= lo, 0.0, NEG)
            return 0
        lax.fori_loop(0, nkc, body, 0)
        s_scr[meta_rows, :] = jnp.where(s16_scr[meta_rows, :].astype(F32) >= lo, 0.0, NEG)

    @pl.when(pending > 0)
    def _():
        need = k_top - count_ge(hi)

        def pick(sv, tri, before):
            band = jnp.where(sv >= lo, jnp.where(sv < hi, 1.0, 0.0), 0.0)
            rank = _dot(tri, band.astype(BF16)) + before
            take = jnp.where(rank <= need, band, 0.0)
            bias = jnp.where(sv >= hi, 0.0, jnp.where(take > 0.5, 0.0, NEG))
            return bias, before + jnp.sum(fold(band, jnp.sum), axis=0, keepdims=True)

        def tri(n):
            return jnp.where(lax.broadcasted_iota(jnp.int32, (n, n), 0)
                             >= lax.broadcasted_iota(jnp.int32, (n, n), 1), 1.0, 0.0).astype(BF16)

        bias_m, before = pick(s16_scr[meta_rows, :].astype(F32), tri(N_META),
                              jnp.zeros((1, qb), F32))
        s_scr[meta_rows, :] = bias_m
        tri_kc = tri(KC)

        def body(kc, before):
            bias, before = pick(s16_scr[rows(kc), :].astype(F32), tri_kc, before)
            s_scr[rows(kc), :] = bias
            return before
        lax.fori_loop(0, nkc, body, before)

    def head(h):
        return slice(h * qb, (h + 1) * qb)

    for h in range(N_HEADS):
        qat_scr[:, head(h)] = (_dot(wuk_ref[h], qt_ref[h * HEAD_DIM:(h + 1) * HEAD_DIM, :])
                               * (ATTN_SCALE * LOG2E)).astype(BF16)
    m_scr[...] = jnp.full(m_scr.shape, 0.5 * NEG, F32)
    acc_scr[...] = jnp.zeros(acc_scr.shape, F32)

    def attend(c_rows, bias, values):
        r = c_rows.shape[0]
        lg_scr[0:r, :] = _dot(c_rows, qat_scr[...])
        for h in range(N_HEADS):
            lg = lg_scr[0:r, head(h)] + bias
            m_old = m_scr[:, head(h)]
            m_new = jnp.maximum(m_old, jnp.max(lg, axis=0, keepdims=True))
            p = jnp.exp2(lg - m_new).astype(BF16)
            m_scr[:, head(h)] = m_new
            pv = None
            for ct_cols, rs in values:
                t = _dot(ct_cols, p[rs, :])
                pv = t if pv is None else pv + t
            acc_scr[:, head(h)] = acc_scr[:, head(h)] * jnp.exp2(m_old - m_new) + pv

    def attend_chunk(kc):
        attend(c_ref[rows(kc), :], s_scr[rows(kc), :], [(ct_ref[kc], slice(0, KC))])

    def attend_pair(i, _):
        attend_chunk(2 * i)
        attend_chunk(2 * i + 1)
        return 0
    lax.fori_loop(0, j // 2, attend_pair, 0)

    @pl.when(j % 2 == 1)
    def _():
        attend_chunk(j - 1)

    attend(jnp.concatenate([c_ref[rows(j), :], cm_ref[...]], axis=0),
           jnp.concatenate([s_scr[rows(j), :], s_scr[meta_rows, :]], axis=0),
           [(ct_ref[j], slice(0, KC)), (cmt_ref[...], slice(KC, KC + N_META))])

    for h in range(N_HEADS):
        olat = acc_scr[0:KV_LATENT, head(h)] / acc_scr[KV_LATENT:KV_LATENT + 1, head(h)]
        ot_scr[h * HEAD_DIM:(h + 1) * HEAD_DIM, :] = _dot(wuvt_ref[h], olat.astype(BF16))
    o_ref[...] = ot_scr[...].T.astype(BF16)


def _mix_kernel(x_ref, g1_ref, wgate_ref, attn_ref, yp_ref, wba_ref, wbp_ref, wout_ref,
                g2_ref, wr_ref, wrhi_ref, br_ref, h1_ref, u2_ref, route_ref, count_ref,
                merged_scr):
    nsub = D_MODEL // LANES
    counts = None
    for r0 in range(0, x_ref.shape[0], MIX_ROWS):
        rows = pl.ds(r0, MIX_ROWS)
        c = _mix_group(x_ref.at[rows], g1_ref, wgate_ref, attn_ref.at[rows], yp_ref.at[rows],
                       wba_ref, wbp_ref, wout_ref, g2_ref, wr_ref, wrhi_ref, br_ref,
                       h1_ref.at[rows], u2_ref.at[pl.ds(r0 * nsub, MIX_ROWS * nsub)],
                       route_ref.at[:, rows], merged_scr.at[rows])
        counts = c if counts is None else counts + c
    count_ref[...] = jnp.broadcast_to(counts, count_ref.shape)


def _mix_group(x_ref, g1_ref, wgate_ref, attn_ref, yp_ref, wba_ref, wbp_ref, wout_ref,
               g2_ref, wr_ref, wrhi_ref, br_ref, h1_ref, u2_ref, route_ref, merged_scr):
    x = x_ref[...]
    u = _rms(x, g1_ref[...]).astype(BF16)
    attn = attn_ref[...]
    yp = yp_ref[...]
    for cb in range(D_MODEL // MIX_COLS):
        ca = slice(cb * MIX_COLS, (cb + 1) * MIX_COLS)
        cp = slice(D_MODEL + cb * MIX_COLS, D_MODEL + (cb + 1) * MIX_COLS)
        g_attn = 1.0 / (1.0 + jnp.exp(-_dot(u, wgate_ref[:, ca])))
        g_pool = 1.0 / (1.0 + jnp.exp(-_dot(u, wgate_ref[:, cp])))
        merged_scr[:, ca] = (g_attn * _dot(attn, wba_ref[:, ca])
                             + g_pool * _dot(yp, wbp_ref[:, ca])).astype(BF16)
    h1 = x + _dot(merged_scr[...], wout_ref[...])
    h1_ref[...] = h1
    u2 = _rms(h1, g2_ref[...])
    nsub = D_MODEL // LANES
    for s in range(nsub):
        u2_ref[pl.ds(s, x.shape[0], stride=nsub), :] = u2[:, s * LANES:(s + 1) * LANES]
    u2_hi = u2.astype(BF16)

    u2_lo = (u2 - u2_hi.astype(F32)).astype(BF16)
    hi_both = _dot(u2_hi, wr_ref[...])
    lg = hi_both[:, :LANES] + hi_both[:, LANES:] + _dot(u2_lo, wrhi_ref[...]) + br_ref[...]
    lane = lax.broadcasted_iota(jnp.int32, lg.shape, 1)
    is_g = lane < N_GROUPS
    gl = jnp.where(is_g, lg, NEG)
    gmax = jnp.max(gl, axis=1, keepdims=True)
    gidx = jnp.min(jnp.where(gl == gmax, lane, LANES), axis=1, keepdims=True)
    p_g = 1.0 / jnp.sum(jnp.where(is_g, jnp.exp(gl - gmax), 0.0), axis=1, keepdims=True)
    e_lane = lane - N_GROUPS
    lane_grp = jnp.where(e_lane >= 0,
                         jnp.where(e_lane < N_EXPERTS, e_lane // EXPERTS_PER_GROUP, -1), -1)
    in_grp = lane_grp == gidx
    el = jnp.where(in_grp, lg, NEG)
    t1 = jnp.max(el, axis=1, keepdims=True)
    i1 = jnp.min(jnp.where(el == t1, lane, LANES), axis=1, keepdims=True)
    el2 = jnp.where(lane == i1, NEG, el)
    t2 = jnp.max(el2, axis=1, keepdims=True)
    i2 = jnp.min(jnp.where(el2 == t2, lane, LANES), axis=1, keepdims=True)
    r = jnp.exp(t2 - t1)
    p1 = 1.0 / (1.0 + r)
    p2 = r * p1
    e1 = (i1 - N_GROUPS).astype(F32)
    e2 = (i2 - N_GROUPS).astype(F32)
    record = jnp.where(lane == 0, e1, jnp.where(lane == 1, e2, jnp.where(
        lane == 2, p1 * p_g, jnp.where(lane == 3, p2 * p_g, 0.0))))
    route_ref[...] = record.T[0:SUBLANES, :]
    chosen = jnp.where(lane == i1, 1.0, jnp.where(lane == i2, 1.0, 0.0))
    return jnp.sum(chosen, axis=0, keepdims=True)


def _moe_kernel(start_ref, cnt_ref, tok_ref, slot_ref, wt_ref, x2_ref, wg_hbm, wu_hbm, wd_hbm,
                y2_ref, r2_scr, ga_scr, gb_scr, rt_scr, wg_buf, wu_buf, wd_buf, w_sem):
    ts = x2_ref.shape[0] // SUBLANES
    tmx = ga_scr.shape[0] // SUBLANES
    st = pl.program_id(0)
    e = pl.program_id(1)
    n_e = pl.num_programs(1)
    nsub = D_MODEL // LANES
    step = st * n_e + e
    n_steps = pl.num_programs(0) * n_e

    def weight_copies(g):
        ex = g % n_e
        buf = g % MOE_WEIGHT_BUFFERS
        return [pltpu.make_async_copy(hbm.at[ex], vbuf.at[buf], w_sem.at[k, buf])
                for k, (hbm, vbuf) in enumerate(((wg_hbm, wg_buf), (wu_hbm, wu_buf),
                                                 (wd_hbm, wd_buf)))]

    @pl.when(step == 0)
    def _():
        for g in range(MOE_WEIGHT_BUFFERS - 1):
            for cp in weight_copies(g):
                cp.start()

    @pl.when(step + MOE_WEIGHT_BUFFERS - 1 < n_steps)
    def _():
        for cp in weight_copies(step + MOE_WEIGHT_BUFFERS - 1):
            cp.start()

    for cp in weight_copies(step):
        cp.wait()
    wbuf = step % MOE_WEIGHT_BUFFERS

    def vreg_rows(i):
        return pl.ds(pl.multiple_of(i * SUBLANES, SUBLANES), SUBLANES)

    def gather(g_scr, base):
        def gather_body(r8, _):
            for u in range(SUBLANES):
                r = r8 * SUBLANES + u
                g_scr[vreg_rows(r), :] = x2_ref[vreg_rows(tok_ref[base + r]), :]
            return 0
        lax.fori_loop(0, tmx // SUBLANES, gather_body, 0)

    def expert_ffn(g_scr, base):
        xg = jnp.concatenate([g_scr[pl.ds(s, tmx, stride=nsub), :] for s in range(nsub)],
                             axis=1).astype(BF16)
        a = _dot(xg, wg_buf[wbuf].astype(BF16))
        b = _dot(xg, wu_buf[wbuf].astype(BF16))
        hg = (a * (1.0 / (1.0 + jnp.exp(-a))) * b).astype(BF16)
        yr = _dot(hg, wd_buf[wbuf].astype(BF16))
        for s in range(nsub):
            rt_scr[pl.ds(s, tmx, stride=nsub), :] = yr[:, s * LANES:(s + 1) * LANES]
        r2_scr[pl.ds(pl.multiple_of(base * SUBLANES, SUBLANES), tmx * SUBLANES), :] = rt_scr[...]

    start = start_ref[step]
    cnt = cnt_ref[step]

    @pl.when(e == 0)
    def _():
        gather(ga_scr, start)

    def run(cur_scr, nxt_scr):
        expert_ffn(cur_scr, start)
        if nxt_scr is not None:
            nxt = start_ref[step + 1]
            for r in range(tmx):
                nxt_scr[r * SUBLANES:(r + 1) * SUBLANES, :] = (
                    x2_ref[vreg_rows(tok_ref[nxt + r]), :])

        def extra_tile(i, _):
            gather(cur_scr, start + i * tmx)
            expert_ffn(cur_scr, start + i * tmx)
            return 0
        lax.fori_loop(1, (cnt + tmx - 1) // tmx, extra_tile, 0)

    last = e == n_e - 1
    even = e % 2 == 0

    @pl.when(jnp.logical_and(even, jnp.logical_not(last)))
    def _():
        run(ga_scr, gb_scr)

    @pl.when(jnp.logical_and(jnp.logical_not(even), jnp.logical_not(last)))
    def _():
        run(gb_scr, ga_scr)

    @pl.when(last)
    def _():
        run(ga_scr if (N_EXPERTS - 1) % 2 == 0 else gb_scr, None)

        def combine_body(t8, _):
            for u in range(SUBLANES):
                t = t8 * SUBLANES + u
                y2_ref[vreg_rows(t), :] = (
                    wt_ref[t] * r2_scr[vreg_rows(slot_ref[t]), :]
                    + wt_ref[ts + t] * r2_scr[vreg_rows(slot_ref[ts + t]), :])
            return 0
        lax.fori_loop(0, ts // SUBLANES, combine_body, 0)


def _final_kernel(h1_ref, y2_ref, gf_ref, o_ref):
    nsub = D_MODEL // LANES
    y = jnp.concatenate([y2_ref[pl.ds(s, h1_ref.shape[0], stride=nsub), :] for s in range(nsub)],
                        axis=1)
    o_ref[...] = _rms(h1_ref[...] + y, gf_ref[...])


def _const_spec(shape):
    nd = len(shape)
    return pl.BlockSpec(shape, lambda *_: (0,) * nd, pipeline_mode=pl.Buffered(1))


def _params(n_axes):
    return pltpu.CompilerParams(dimension_semantics=("arbitrary",) * n_axes,
                                vmem_limit_bytes=VMEM_LIMIT)


def kernel(x, meta_tokens, norm1_g, w_in, kv_norm_g, w_uk, w_uv, w_pool, pool_scale,
           w_branch_attn, w_branch_pool, w_out, norm2_g, w_group_router, b_group_router,
           w_expert_router, b_expert_router, w_expert_gate, w_expert_up, w_expert_down,
           final_norm_g):
    B, S, D = x.shape
    assert D == D_MODEL and S % QB == 0 and S % TM_PROJ == 0 and w_in.shape[0] == 1
    assert QB == KC and TM_PROJ % KC == 0 and QB % CHUNK == 0
    N = B * S
    k_top = min(TOPK_MAX, S // 4)
    xr = x.reshape(N, D)

    wi = w_in[0]
    w1 = jnp.concatenate(
        [wi[:, 0:640], wi[:, 640:896], wi[:, 936:1448], wi[:, 896:936],
         jnp.zeros((D, W1_WIDTH - 1448), F32)], axis=1).astype(BF16)
    wgate = wi[:, 1448:].astype(BF16)
    g1 = norm1_g[0].reshape(1, D)
    kvg = kv_norm_g[0].reshape(1, KV_LATENT)
    wpool = w_pool[0].astype(BF16)
    pscale = pool_scale[0].reshape(1, POOL_WIDTH)
    wuk = jnp.transpose(w_uk[0], (1, 0, 2)).astype(BF16)
    wuvt = jnp.transpose(w_uv[0], (1, 2, 0)).astype(BF16)
    wr = jnp.concatenate(
        [w_group_router[0], w_expert_router[0].reshape(D, N_EXPERTS),
         jnp.zeros((D, LANES - N_GROUPS - N_EXPERTS), F32)], axis=1)
    br = jnp.concatenate(
        [b_group_router[0], b_expert_router[0].reshape(N_EXPERTS),
         jnp.zeros((LANES - N_GROUPS - N_EXPERTS,), F32)]).reshape(1, LANES)

    c_m, tail_m, pv_m = pl.pallas_call(
        _meta_kernel,
        out_shape=(jax.ShapeDtypeStruct((N_META, KV_LATENT), F32),
                   jax.ShapeDtypeStruct((N_META, LANES), F32),
                   jax.ShapeDtypeStruct((N_META, POOL_WIDTH), F32)),
        name="meta",
    )(meta_tokens, g1, w1, kvg)
    cm = c_m.astype(BF16)
    cmt = jnp.concatenate([cm.T, jnp.ones((1, N_META), BF16),
                           jnp.zeros((CT_ROWS - KV_LATENT - 1, N_META), BF16)], axis=0)
    ikm = tail_m[:, :IDX_DIM].astype(BF16)

    tpb = S // TM_PROJ
    tok = lambda w: pl.BlockSpec((TM_PROJ, w), lambda b, i: (b * tpb + i, 0))
    tok_t = lambda w: pl.BlockSpec((w, TM_PROJ), lambda b, i: (0, b * tpb + i))
    qt, c, ct, iqt, ik, iwt, yp = pl.pallas_call(
        _proj_kernel,
        grid=(B, tpb),
        in_specs=[tok(D), _const_spec((1, D)), _const_spec((D, W1_WIDTH)),
                  _const_spec((1, KV_LATENT)), _const_spec((N_META, POOL_WIDTH)),
                  _const_spec((len(POOL_WINDOWS), POOL_GROUP, POOL_GROUP)),
                  _const_spec((1, POOL_WIDTH))],
        out_specs=[tok_t(ATTN_WIDTH), tok(KV_LATENT),
                   pl.BlockSpec((None, TM_PROJ // KC, CT_ROWS, KC), lambda b, i: (b, i, 0, 0)),
                   tok_t(IDX_HEADS * IDX_DIM), tok(IDX_DIM), tok_t(IDX_HEADS), tok(POOL_WIDTH)],
        out_shape=(jax.ShapeDtypeStruct((ATTN_WIDTH, N), BF16),
                   jax.ShapeDtypeStruct((N, KV_LATENT), BF16),
                   jax.ShapeDtypeStruct((B, S // KC, CT_ROWS, KC), BF16),
                   jax.ShapeDtypeStruct((IDX_HEADS * IDX_DIM, N), BF16),
                   jax.ShapeDtypeStruct((N, IDX_DIM), BF16),
                   jax.ShapeDtypeStruct((IDX_HEADS, N), F32),
                   jax.ShapeDtypeStruct((N, POOL_WIDTH), BF16)),
        scratch_shapes=[pltpu.VMEM((TM_PROJ + N_META, POOL_WIDTH), F32)],
        compiler_params=_params(2),
        name="proj",
    )(xr, g1, w1, kvg, pv_m, wpool, pscale)

    nqb = S // QB
    qcol = lambda w: pl.BlockSpec((w, QB), lambda b, j: (0, b * nqb + j))
    attn = pl.pallas_call(
        functools.partial(_attn_kernel, k_top=float(k_top)),
        grid=(B, nqb),
        in_specs=[pl.BlockSpec((S, IDX_DIM), lambda b, j: (b, 0)),
                  _const_spec((N_META, IDX_DIM)),
                  pl.BlockSpec((S, KV_LATENT), lambda b, j: (b, 0)),
                  _const_spec((N_META, KV_LATENT)),
                  pl.BlockSpec((None, S // KC, CT_ROWS, KC), lambda b, j: (b, 0, 0, 0)),
                  _const_spec((CT_ROWS, N_META)),
                  qcol(IDX_HEADS * IDX_DIM), qcol(IDX_HEADS), qcol(ATTN_WIDTH),
                  _const_spec((N_HEADS, KV_LATENT, HEAD_DIM)),
                  _const_spec((N_HEADS, HEAD_DIM, KV_LATENT))],
        out_specs=pl.BlockSpec((QB, ATTN_WIDTH), lambda b, j: (b * nqb + j, 0)),
        out_shape=jax.ShapeDtypeStruct((N, ATTN_WIDTH), BF16),
        scratch_shapes=[pltpu.VMEM((S + N_META, QB), F32), pltpu.VMEM((S + N_META, QB), BF16),
                        pltpu.VMEM((KV_LATENT, N_HEADS * QB), BF16),
                        pltpu.VMEM((KC + N_META, N_HEADS * QB), F32),
                        pltpu.VMEM((1, N_HEADS * QB), F32),
                        pltpu.VMEM((CT_ROWS, N_HEADS * QB), F32),
                        pltpu.VMEM((ATTN_WIDTH, QB), F32)],
        compiler_params=_params(2),
        name="attn",
    )(ik, ikm, c, cm, ct, cmt, iqt, iwt, qt, wuk, wuvt)

    wr_hi = wr.astype(BF16)
    wr_lo = (wr - wr_hi.astype(F32)).astype(BF16)
    tok1 = lambda w: pl.BlockSpec((TM_MIX, w), lambda i: (i, 0))
    h1, u2, route, tile_counts = pl.pallas_call(
        _mix_kernel,
        grid=(N // TM_MIX,),
        in_specs=[tok1(D), _const_spec((1, D)), _const_spec((D, 2 * D)), tok1(ATTN_WIDTH),
                  tok1(POOL_WIDTH), _const_spec((ATTN_WIDTH, D)), _const_spec((POOL_WIDTH, D)),
                  _const_spec((D, D)), _const_spec((1, D)), _const_spec((D, 2 * LANES)),
                  _const_spec((D, LANES)), _const_spec((1, LANES))],
        out_specs=[tok1(D), pl.BlockSpec((TM_MIX * SUBLANES, LANES), lambda i: (i, 0)),
                   pl.BlockSpec((SUBLANES, TM_MIX), lambda i: (0, i)),
                   pl.BlockSpec((SUBLANES, LANES), lambda i: (i, 0))],
        out_shape=(jax.ShapeDtypeStruct((N, D), F32),
                   jax.ShapeDtypeStruct((N * SUBLANES, LANES), F32),
                   jax.ShapeDtypeStruct((SUBLANES, N), F32),
                   jax.ShapeDtypeStruct((N // TM_MIX * SUBLANES, LANES), F32)),
        scratch_shapes=[pltpu.VMEM((TM_MIX, D), BF16)],
        compiler_params=_params(1),
        name="mix",
    )(xr, g1, wgate, attn, yp, w_branch_attn[0].astype(BF16), w_branch_pool[0].astype(BF16),
      w_out[0].astype(BF16), norm2_g[0].reshape(1, D),
      jnp.concatenate([wr_hi, wr_lo], axis=1), wr_hi, br)

    nst = N // TS_MOE
    n_asg = 2 * TS_MOE
    per_tile = lambda a: a.reshape(2, nst, TS_MOE).transpose(1, 0, 2).reshape(nst, n_asg)
    eid = per_tile(route[0:2].astype(jnp.int32))
    wts = per_tile(route[2:4]).reshape(nst * n_asg)
    order = jnp.argsort(eid, axis=1, stable=True).astype(jnp.int32)
    slot = jnp.argsort(order, axis=1).astype(jnp.int32).reshape(nst * n_asg)
    tok_sorted = jnp.pad(order % TS_MOE, ((0, 0), (0, LIST_PAD))).reshape(
        nst * (n_asg + LIST_PAD))
    counts = tile_counts.reshape(nst, TS_MOE // TM_MIX, SUBLANES, LANES)[
        :, :, 0, N_GROUPS:N_GROUPS + N_EXPERTS].sum(axis=1).astype(jnp.int32)
    starts = (jnp.cumsum(counts, axis=1) - counts).reshape(nst * N_EXPERTS)
    counts = counts.reshape(nst * N_EXPERTS)

    weg = w_expert_gate[0].reshape(N_EXPERTS, D, EXPERT_HIDDEN)
    weu = w_expert_up[0].reshape(N_EXPERTS, D, EXPERT_HIDDEN)
    wed = w_expert_down[0].reshape(N_EXPERTS, EXPERT_HIDDEN, D)
    smem = lambda n: pl.BlockSpec((n,), lambda s, e, *_: (s,), memory_space=pltpu.SMEM)
    y = pl.pallas_call(
        _moe_kernel,
        grid_spec=pltpu.PrefetchScalarGridSpec(
            num_scalar_prefetch=2,
            grid=(nst, N_EXPERTS),
            in_specs=[smem(n_asg + LIST_PAD), smem(n_asg), smem(n_asg),
                      pl.BlockSpec((TS_MOE * SUBLANES, LANES), lambda s, e, *_: (s, 0)),
                      pl.BlockSpec(memory_space=pl.ANY), pl.BlockSpec(memory_space=pl.ANY),
                      pl.BlockSpec(memory_space=pl.ANY)],
            out_specs=pl.BlockSpec((TS_MOE * SUBLANES, LANES), lambda s, e, *_: (s, 0),
                                   pipeline_mode=pl.Buffered(1)),
            scratch_shapes=[pltpu.VMEM(((n_asg + TMX_MOE) * SUBLANES, LANES), F32),
                            pltpu.VMEM((TMX_MOE * SUBLANES, LANES), F32),
                            pltpu.VMEM((TMX_MOE * SUBLANES, LANES), F32),
                            pltpu.VMEM((TMX_MOE * SUBLANES, LANES), F32),
                            pltpu.VMEM((MOE_WEIGHT_BUFFERS, D, EXPERT_HIDDEN), F32),
                            pltpu.VMEM((MOE_WEIGHT_BUFFERS, D, EXPERT_HIDDEN), F32),
                            pltpu.VMEM((MOE_WEIGHT_BUFFERS, EXPERT_HIDDEN, D), F32),
                            pltpu.SemaphoreType.DMA((3, MOE_WEIGHT_BUFFERS))]),
        out_shape=jax.ShapeDtypeStruct((N * SUBLANES, LANES), F32),
        compiler_params=_params(2),
        name="moe",
    )(starts, counts, tok_sorted, slot, wts, u2, weg, weu, wed)

    ftok = pl.BlockSpec((TM_FINAL, D), lambda i: (i, 0))
    out = pl.pallas_call(
        _final_kernel,
        grid=(N // TM_FINAL,),
        in_specs=[ftok, pl.BlockSpec((TM_FINAL * SUBLANES, LANES), lambda i: (i, 0)),
                  _const_spec((1, D))],
        out_specs=ftok,
        out_shape=jax.ShapeDtypeStruct((N, D), F32),
        compiler_params=_params(1),
        name="final",
    )(h1, y, final_norm_g.reshape(1, D))
    return out.reshape(B, S, D)
```

```python
import functools

import jax
import jax.numpy as jnp
from jax import lax
from jax.experimental import pallas as pl
from jax.experimental.pallas import tpu as pltpu

F32 = jnp.float32
BF16 = jnp.bfloat16

D_MODEL = 1024
CHUNK = 64
N_META = 16
N_HEADS = 8
HEAD_DIM = 64
ATTN_WIDTH = N_HEADS * HEAD_DIM
KV_LATENT = 128
IDX_HEADS = 8
IDX_DIM = 32
TOPK_MAX = 256
ATTN_SCALE = HEAD_DIM ** -0.5
IDX_SCALE = (IDX_HEADS ** -0.5) * (IDX_DIM ** -0.5)
POOL_WINDOWS = (2, 4, 8, 16)
POOL_WIDTH = 512
POOL_GROUP = 128
N_GROUPS = 4
EXPERTS_PER_GROUP = 8
N_EXPERTS = N_GROUPS * EXPERTS_PER_GROUP
EXPERT_HIDDEN = 256
EPS = 1e-6

LANES = 128
SUBLANES = 8
W1_WIDTH = 1536
NEG = -1e30
POS = 1e30
VMEM_LIMIT = 56 * 1024 * 1024

TM_PROJ = 2048
TM_MIX = 1024
MIX_ROWS = 256
MIX_COLS = 256
QB = 256
KC = 256
MAX_BISECT = 40
BISECT_PER_CHECK = 8
FIRST_TIE_CHECK = 16
CT_ROWS = KV_LATENT + 16
LOG2E = 1.4426950408889634
TS_MOE = 2048
TMX_MOE = 160
MOE_WEIGHT_BUFFERS = 3
LIST_PAD = 1024
TM_FINAL = 1024


def _rms(x, g):
    return x * lax.rsqrt(jnp.mean(x * x, axis=-1, keepdims=True) + EPS) * g


def _dot(a, b):
    return jnp.dot(a, b, preferred_element_type=F32)


def _meta_kernel(m_ref, g1_ref, w1_ref, kvg_ref, c_ref, tail_ref, pv_ref):
    u = _rms(m_ref[...], g1_ref[...]).astype(BF16)
    p = _dot(u, w1_ref[...])
    c_ref[...] = _rms(p[:, 512:640], kvg_ref[...])
    pv_ref[...] = p[:, 896:1408]
    tail_ref[...] = p[:, 1408:1536]


def _proj_kernel(x_ref, g1_ref, w1_ref, kvg_ref, pvmeta_ref, wpool_ref, pscale_ref,
                 qt_ref, c_ref, ct_ref, iqt_ref, ik_ref, iwt_ref, yp_ref, ext_ref):
    tm = x_ref.shape[0]

    @pl.when(pl.program_id(1) == 0)
    def _():
        ext_ref[0:N_META, :] = pvmeta_ref[...]

    for r0 in range(0, tm, KC):
        rs = slice(r0, r0 + KC)
        u = _rms(x_ref[rs, :], g1_ref[...]).astype(BF16)
        p = _dot(u, w1_ref[...])
        qt_ref[:, rs] = p[:, 0:512].T.astype(BF16)
        c = _rms(p[:, 512:640], kvg_ref[...])
        c_ref[rs, :] = c.astype(BF16)
        ct_ref[r0 // KC] = jnp.concatenate(
            [c.T, jnp.ones((1, KC), F32), jnp.zeros((CT_ROWS - KV_LATENT - 1, KC), F32)],
            axis=0).astype(BF16)
        iqt_ref[:, rs] = p[:, 640:896].T.astype(BF16)
        tail = p[:, 1408:1536]
        ik_ref[rs, :] = tail[:, 0:IDX_DIM].astype(BF16)
        iwt_ref[:, rs] = tail.T[IDX_DIM:IDX_DIM + IDX_HEADS, :] * IDX_SCALE
        pv = p[:, 896:1408]
        ext_ref[N_META + r0:N_META + r0 + KC, :] = pv
        for g, w in enumerate(POOL_WINDOWS):
            cols = slice(g * POOL_GROUP, (g + 1) * POOL_GROUP)
            acc = pv[:, cols]
            for k in range(1, w):
                acc = acc + ext_ref[N_META + r0 - k:N_META + r0 - k + KC, cols]
            d = acc * (1.0 / w) - pv[:, cols]
            y = _dot(d.astype(BF16), wpool_ref[g]) * pscale_ref[:, cols]
            yp_ref[rs, cols] = y.astype(BF16)
    ext_ref[0:N_META, :] = ext_ref[tm:tm + N_META, :]


def _attn_kernel(ik_ref, ikm_ref, c_ref, cm_ref, ct_ref, cmt_ref, iqt_ref, iwt_ref, qt_ref,
                 wuk_ref, wuvt_ref, o_ref,
                 s_scr, s16_scr, qat_scr, lg_scr, m_scr, acc_scr, ot_scr, *, k_top):
    n_real = c_ref.shape[0]
    qb = iqt_ref.shape[1]
    j = pl.program_id(1)
    nkc = j + 1
    qchunk = (j * qb + lax.broadcasted_iota(jnp.int32, (1, qb), 1)) // CHUNK
    meta_rows = pl.ds(n_real, N_META)

    def rows(kc):
        return pl.ds(pl.multiple_of(kc * KC, KC), KC)

    def fold(x, op):
        groups = x.shape[0] // SUBLANES
        chains = 4 if groups % 4 == 0 else 1
        x = x.reshape(groups // chains, chains, SUBLANES, qb)
        return op(op(x, axis=0), axis=0)

    iqt_heads = [iqt_ref[h * IDX_DIM:(h + 1) * IDX_DIM, :] for h in range(IDX_HEADS)]

    def scores(ik_rows):
        acc = None
        for h in range(IDX_HEADS):
            t = jnp.maximum(_dot(ik_rows, iqt_heads[h]), 0.0) * iwt_ref[h:h + 1, :]
            acc = t if acc is None else acc + t
        return acc

    def score_body(kc, carry):
        mn, mx = carry
        sc = scores(ik_ref[rows(kc), :])
        s16_scr[rows(kc), :] = sc.astype(BF16)
        return jnp.minimum(mn, fold(sc, jnp.min)), jnp.maximum(mx, fold(sc, jnp.max))

    def score_pair(i, carry):
        return score_body(2 * i + 1, score_body(2 * i, carry))

    mn8, mx8 = lax.fori_loop(0, j // 2, score_pair, (jnp.full((SUBLANES, qb), POS, F32),
                                                     jnp.full((SUBLANES, qb), NEG, F32)))
    mn8, mx8 = lax.cond(j % 2 == 1, lambda c: score_body(j - 1, c), lambda c: c, (mn8, mx8))
    sc_all = scores(jnp.concatenate([ik_ref[rows(j), :], ikm_ref[...]], axis=0))
    sc = sc_all[0:KC, :]
    sm = sc_all[KC:KC + N_META, :]
    s16_scr[meta_rows, :] = sm.astype(BF16)
    mn0 = jnp.min(sm, axis=0, keepdims=True)
    mx0 = jnp.max(sm, axis=0, keepdims=True)
    adm = (j * KC + lax.broadcasted_iota(jnp.int32, (KC, 1), 0)) // CHUNK <= qchunk
    s16_scr[rows(j), :] = jnp.where(adm, sc, NEG).astype(BF16)
    mn8 = jnp.minimum(mn8, fold(jnp.where(adm, sc, POS), jnp.min))
    mx8 = jnp.maximum(mx8, fold(jnp.where(adm, sc, NEG), jnp.max))
    mn = jnp.minimum(mn0, jnp.min(mn8, axis=0, keepdims=True))
    mx = jnp.maximum(mx0, jnp.max(mx8, axis=0, keepdims=True))

    pack = 2 * SUBLANES

    def as16(v):
        return jnp.broadcast_to(v, (pack, qb)).astype(BF16)

    def tree(parts, op):
        while len(parts) > 1:
            parts = ([op(parts[i], parts[i + 1]) for i in range(0, len(parts) - 1, 2)]
                     + ([parts[-1]] if len(parts) % 2 else []))
        return parts[0]

    def groups(x):
        return [x[i * pack:(i + 1) * pack, :] for i in range(x.shape[0] // pack)]

    one16 = jnp.ones((pack, qb), BF16)
    zero16 = jnp.zeros((pack, qb), BF16)

    def count_ge(th):
        th16 = as16(th)

        def ones(x):
            return tree([jnp.where(g >= th16, one16, zero16) for g in groups(x)], jnp.add)

        def body(kc, cnt):
            return cnt + ones(s16_scr[rows(kc), :])
        cnt = lax.fori_loop(0, nkc, body, ones(s16_scr[meta_rows, :]))
        return jnp.sum(cnt.astype(F32), axis=0, keepdims=True)

    def band_extent(lo, hi):
        lo16, hi16 = as16(lo), as16(hi)
        pos16 = jnp.full((pack, qb), POS, BF16)
        neg16 = jnp.full((pack, qb), NEG, BF16)

        def ext(x):
            return (tree([jnp.where(g >= lo16, g, pos16) for g in groups(x)], jnp.minimum),
                    tree([jnp.where(g < hi16, g, neg16) for g in groups(x)], jnp.maximum))

        def body(kc, carry):
            bmin, bmax = ext(s16_scr[rows(kc), :])
            return jnp.minimum(carry[0], bmin), jnp.maximum(carry[1], bmax)
        bmin, bmax = lax.fori_loop(0, nkc, body, ext(s16_scr[meta_rows, :]))
        return (jnp.min(bmin.astype(F32), axis=0, keepdims=True),
                jnp.max(bmax.astype(F32), axis=0, keepdims=True))

    def any_lane(flags):
        return jnp.max(flags)

    def bisect_cond(carry):
        it, _, _, _, pending = carry
        return jnp.logical_and(it < MAX_BISECT, pending > 0)

    def bisect_body(carry):
        it, lo, hi, cnt_lo, _ = carry
        for _ in range(BISECT_PER_CHECK):
            mid = (lo + (hi - lo) * 0.5).astype(BF16).astype(F32)
            cnt = count_ge(mid)
            ge = cnt >= k_top
            lo = jnp.where(ge, mid, lo)
            hi = jnp.where(ge, hi, mid)
            cnt_lo = jnp.where(ge, cnt, cnt_lo)
        it = it + BISECT_PER_CHECK
        over = jnp.where(cnt_lo > k_top, 1, 0)
        pending = any_lane(over)

        def tied_check():
            bmin, bmax = band_extent(lo, hi)
            return any_lane(jnp.where(bmin < bmax, over, 0))

        pending = lax.cond(jnp.logical_and(pending > 0, it >= FIRST_TIE_CHECK),
                           tied_check, lambda: pending)
        return it, lo, hi, cnt_lo, pending

    n_adm = (N_META + CHUNK * (qchunk + 1)).astype(F32)
    lo0 = mn.astype(BF16).astype(F32)
    hi0 = (mx + jnp.abs(mx) * (2.0 ** -6) + 1e-30).astype(BF16).astype(F32)
    _, lo, hi, cnt_lo, _ = lax.while_loop(
        bisect_cond, bisect_body,
        (jnp.int32(0), lo0, hi0, n_adm, any_lane(jnp.where(n_adm > k_top, 1, 0))))
    pending = any_lane(jnp.where(cnt_lo > k_top, 1, 0))

    @pl.when(pending == 0)
    def _():
        def body(kc, _):
            s_scr[rows(kc), :] = jnp.where(s16_scr[rows(kc), :].astype(F32) >= lo, 0.0, NEG)
            return 0
        lax.fori_loop(0, nkc, body, 0)
        s_scr[meta_rows, :] = jnp.where(s16_scr[meta_rows, :].astype(F32) >= lo, 0.0, NEG)

    @pl.when(pending > 0)
    def _():
        need = k_top - count_ge(hi)

        def pick(sv, tri, before):
            band = jnp.where(sv >= lo, jnp.where(sv < hi, 1.0, 0.0), 0.0)
            rank = _dot(tri, band.astype(BF16)) + before
            take = jnp.where(rank <= need, band, 0.0)
            bias = jnp.where(sv >= hi, 0.0, jnp.where(take > 0.5, 0.0, NEG))
            return bias, before + jnp.sum(fold(band, jnp.sum), axis=0, keepdims=True)

        def tri(n):
            return jnp.where(lax.broadcasted_iota(jnp.int32, (n, n), 0)
                             >= lax.broadcasted_iota(jnp.int32, (n, n), 1), 1.0, 0.0).astype(BF16)

        bias_m, before = pick(s16_scr[meta_rows, :].astype(F32), tri(N_META),
                              jnp.zeros((1, qb), F32))
        s_scr[meta_rows, :] = bias_m
        tri_kc = tri(KC)

        def body(kc, before):
            bias, before = pick(s16_scr[rows(kc), :].astype(F32), tri_kc, before)
            s_scr[rows(kc), :] = bias
            return before
        lax.fori_loop(0, nkc, body, before)

    def head(h):
        return slice(h * qb, (h + 1) * qb)

    for h in range(N_HEADS):
        qat_scr[:, head(h)] = (_dot(wuk_ref[h], qt_ref[h * HEAD_DIM:(h + 1) * HEAD_DIM, :])
                               * (ATTN_SCALE * LOG2E)).astype(BF16)
    m_scr[...] = jnp.full(m_scr.shape, 0.5 * NEG, F32)
    acc_scr[...] = jnp.zeros(acc_scr.shape, F32)

    def attend(c_rows, bias, values):
        r = c_rows.shape[0]
        lg_scr[0:r, :] = _dot(c_rows, qat_scr[...])
        for h in range(N_HEADS):
            lg = lg_scr[0:r, head(h)] + bias
            m_old = m_scr[:, head(h)]
            m_new = jnp.maximum(m_old, jnp.max(lg, axis=0, keepdims=True))
            p = jnp.exp2(lg - m_new).astype(BF16)
            m_scr[:, head(h)] = m_new
            pv = None
            for ct_cols, rs in values:
                t = _dot(ct_cols, p[rs, :])
                pv = t if pv is None else pv + t
            acc_scr[:, head(h)] = acc_scr[:, head(h)] * jnp.exp2(m_old - m_new) + pv

    def attend_chunk(kc):
        attend(c_ref[rows(kc), :], s_scr[rows(kc), :], [(ct_ref[kc], slice(0, KC))])

    def attend_pair(i, _):
        attend_chunk(2 * i)
        attend_chunk(2 * i + 1)
        return 0
    lax.fori_loop(0, j // 2, attend_pair, 0)

    @pl.when(j % 2 == 1)
    def _():
        attend_chunk(j - 1)

    attend(jnp.concatenate([c_ref[rows(j), :], cm_ref[...]], axis=0),
           jnp.concatenate([s_scr[rows(j), :], s_scr[meta_rows, :]], axis=0),
           [(ct_ref[j], slice(0, KC)), (cmt_ref[...], slice(KC, KC + N_META))])

    for h in range(N_HEADS):
        olat = acc_scr[0:KV_LATENT, head(h)] / acc_scr[KV_LATENT:KV_LATENT + 1, head(h)]
        ot_scr[h * HEAD_DIM:(h + 1) * HEAD_DIM, :] = _dot(wuvt_ref[h], olat.astype(BF16))
    o_ref[...] = ot_scr[...].T.astype(BF16)


def _mix_kernel(x_ref, g1_ref, wgate_ref, attn_ref, yp_ref, wba_ref, wbp_ref, wout_ref,
                g2_ref, wr_ref, wrhi_ref, br_ref, h1_ref, u2_ref, route_ref, count_ref,
                merged_scr):
    nsub = D_MODEL // LANES
    counts = None
    for r0 in range(0, x_ref.shape[0], MIX_ROWS):
        rows = pl.ds(r0, MIX_ROWS)
        c = _mix_group(x_ref.at[rows], g1_ref, wgate_ref, attn_ref.at[rows], yp_ref.at[rows],
                       wba_ref, wbp_ref, wout_ref, g2_ref, wr_ref, wrhi_ref, br_ref,
                       h1_ref.at[rows], u2_ref.at[pl.ds(r0 * nsub, MIX_ROWS * nsub)],
                       route_ref.at[:, rows], merged_scr.at[rows])
        counts = c if counts is None else counts + c
    count_ref[...] = jnp.broadcast_to(counts, count_ref.shape)


def _mix_group(x_ref, g1_ref, wgate_ref, attn_ref, yp_ref, wba_ref, wbp_ref, wout_ref,
               g2_ref, wr_ref, wrhi_ref, br_ref, h1_ref, u2_ref, route_ref, merged_scr):
    x = x_ref[...]
    u = _rms(x, g1_ref[...]).astype(BF16)
    attn = attn_ref[...]
    yp = yp_ref[...]
    for cb in range(D_MODEL // MIX_COLS):
        ca = slice(cb * MIX_COLS, (cb + 1) * MIX_COLS)
        cp = slice(D_MODEL + cb * MIX_COLS, D_MODEL + (cb + 1) * MIX_COLS)
        g_attn = 1.0 / (1.0 + jnp.exp(-_dot(u, wgate_ref[:, ca])))
        g_pool = 1.0 / (1.0 + jnp.exp(-_dot(u, wgate_ref[:, cp])))
        merged_scr[:, ca] = (g_attn * _dot(attn, wba_ref[:, ca])
                             + g_pool * _dot(yp, wbp_ref[:, ca])).astype(BF16)
    h1 = x + _dot(merged_scr[...], wout_ref[...])
    h1_ref[...] = h1
    u2 = _rms(h1, g2_ref[...])
    nsub = D_MODEL // LANES
    for s in range(nsub):
        u2_ref[pl.ds(s, x.shape[0], stride=nsub), :] = u2[:, s * LANES:(s + 1) * LANES]
    u2_hi = u2.astype(BF16)

    u2_lo = (u2 - u2_hi.astype(F32)).astype(BF16)
    hi_both = _dot(u2_hi, wr_ref[...])
    lg = hi_both[:, :LANES] + hi_both[:, LANES:] + _dot(u2_lo, wrhi_ref[...]) + br_ref[...]
    lane = lax.broadcasted_iota(jnp.int32, lg.shape, 1)
    is_g = lane < N_GROUPS
    gl = jnp.where(is_g, lg, NEG)
    gmax = jnp.max(gl, axis=1, keepdims=True)
    gidx = jnp.min(jnp.where(gl == gmax, lane, LANES), axis=1, keepdims=True)
    p_g = 1.0 / jnp.sum(jnp.where(is_g, jnp.exp(gl - gmax), 0.0), axis=1, keepdims=True)
    e_lane = lane - N_GROUPS
    lane_grp = jnp.where(e_lane >= 0,
                         jnp.where(e_lane < N_EXPERTS, e_lane // EXPERTS_PER_GROUP, -1), -1)
    in_grp = lane_grp == gidx
    el = jnp.where(in_grp, lg, NEG)
    t1 = jnp.max(el, axis=1, keepdims=True)
    i1 = jnp.min(jnp.where(el == t1, lane, LANES), axis=1, keepdims=True)
    el2 = jnp.where(lane == i1, NEG, el)
    t2 = jnp.max(el2, axis=1, keepdims=True)
    i2 = jnp.min(jnp.where(el2 == t2, lane, LANES), axis=1, keepdims=True)
    r = jnp.exp(t2 - t1)
    p1 = 1.0 / (1.0 + r)
    p2 = r * p1
    e1 = (i1 - N_GROUPS).astype(F32)
    e2 = (i2 - N_GROUPS).astype(F32)
    record = jnp.where(lane == 0, e1, jnp.where(lane == 1, e2, jnp.where(
        lane == 2, p1 * p_g, jnp.where(lane == 3, p2 * p_g, 0.0))))
    route_ref[...] = record.T[0:SUBLANES, :]
    chosen = jnp.where(lane == i1, 1.0, jnp.where(lane == i2, 1.0, 0.0))
    return jnp.sum(chosen, axis=0, keepdims=True)


def _moe_kernel(start_ref, cnt_ref, tok_ref, slot_ref, wt_ref, x2_ref, wg_hbm, wu_hbm, wd_hbm,
                y2_ref, r2_scr, ga_scr, gb_scr, rt_scr, wg_buf, wu_buf, wd_buf, w_sem):
    ts = x2_ref.shape[0] // SUBLANES
    tmx = ga_scr.shape[0] // SUBLANES
    st = pl.program_id(0)
    e = pl.program_id(1)
    n_e = pl.num_programs(1)
    nsub = D_MODEL // LANES
    step = st * n_e + e
    n_steps = pl.num_programs(0) * n_e

    def weight_copies(g):
        ex = g % n_e
        buf = g % MOE_WEIGHT_BUFFERS
        return [pltpu.make_async_copy(hbm.at[ex], vbuf.at[buf], w_sem.at[k, buf])
                for k, (hbm, vbuf) in enumerate(((wg_hbm, wg_buf), (wu_hbm, wu_buf),
                                                 (wd_hbm, wd_buf)))]

    @pl.when(step == 0)
    def _():
        for g in range(MOE_WEIGHT_BUFFERS - 1):
            for cp in weight_copies(g):
                cp.start()

    @pl.when(step + MOE_WEIGHT_BUFFERS - 1 < n_steps)
    def _():
        for cp in weight_copies(step + MOE_WEIGHT_BUFFERS - 1):
            cp.start()

    for cp in weight_copies(step):
        cp.wait()
    wbuf = step % MOE_WEIGHT_BUFFERS

    def vreg_rows(i):
        return pl.ds(pl.multiple_of(i * SUBLANES, SUBLANES), SUBLANES)

    def gather(g_scr, base):
        def gather_body(r8, _):
            for u in range(SUBLANES):
                r = r8 * SUBLANES + u
                g_scr[vreg_rows(r), :] = x2_ref[vreg_rows(tok_ref[base + r]), :]
            return 0
        lax.fori_loop(0, tmx // SUBLANES, gather_body, 0)

    def expert_ffn(g_scr, base):
        xg = jnp.concatenate([g_scr[pl.ds(s, tmx, stride=nsub), :] for s in range(nsub)],
                             axis=1).astype(BF16)
        a = _dot(xg, wg_buf[wbuf].astype(BF16))
        b = _dot(xg, wu_buf[wbuf].astype(BF16))
        hg = (a * (1.0 / (1.0 + jnp.exp(-a))) * b).astype(BF16)
        yr = _dot(hg, wd_buf[wbuf].astype(BF16))
        for s in range(nsub):
            rt_scr[pl.ds(s, tmx, stride=nsub), :] = yr[:, s * LANES:(s + 1) * LANES]
        r2_scr[pl.ds(pl.multiple_of(base * SUBLANES, SUBLANES), tmx * SUBLANES), :] = rt_scr[...]

    start = start_ref[step]
    cnt = cnt_ref[step]

    @pl.when(e == 0)
    def _():
        gather(ga_scr, start)

    def run(cur_scr, nxt_scr):
        expert_ffn(cur_scr, start)
        if nxt_scr is not None:
            nxt = start_ref[step + 1]
            for r in range(tmx):
                nxt_scr[r * SUBLANES:(r + 1) * SUBLANES, :] = (
                    x2_ref[vreg_rows(tok_ref[nxt + r]), :])

        def extra_tile(i, _):
            gather(cur_scr, start + i * tmx)
            expert_ffn(cur_scr, start + i * tmx)
            return 0
        lax.fori_loop(1, (cnt + tmx - 1) // tmx, extra_tile, 0)

    last = e == n_e - 1
    even = e % 2 == 0

    @pl.when(jnp.logical_and(even, jnp.logical_not(last)))
    def _():
        run(ga_scr, gb_scr)

    @pl.when(jnp.logical_and(jnp.logical_not(even), jnp.logical_not(last)))
    def _():
        run(gb_scr, ga_scr)

    @pl.when(last)
    def _():
        run(ga_scr if (N_EXPERTS - 1) % 2 == 0 else gb_scr, None)

        def combine_body(t8, _):
            for u in range(SUBLANES):
                t = t8 * SUBLANES + u
                y2_ref[vreg_rows(t), :] = (
                    wt_ref[t] * r2_scr[vreg_rows(slot_ref[t]), :]
                    + wt_ref[ts + t] * r2_scr[vreg_rows(slot_ref[ts + t]), :])
            return 0
        lax.fori_loop(0, ts // SUBLANES, combine_body, 0)


def _final_kernel(h1_ref, y2_ref, gf_ref, o_ref):
    nsub = D_MODEL // LANES
    y = jnp.concatenate([y2_ref[pl.ds(s, h1_ref.shape[0], stride=nsub), :] for s in range(nsub)],
                        axis=1)
    o_ref[...] = _rms(h1_ref[...] + y, gf_ref[...])


def _const_spec(shape):
    nd = len(shape)
    return pl.BlockSpec(shape, lambda *_: (0,) * nd, pipeline_mode=pl.Buffered(1))


def _params(n_axes):
    return pltpu.CompilerParams(dimension_semantics=("arbitrary",) * n_axes,
                                vmem_limit_bytes=VMEM_LIMIT)


def kernel(x, meta_tokens, norm1_g, w_in, kv_norm_g, w_uk, w_uv, w_pool, pool_scale,
           w_branch_attn, w_branch_pool, w_out, norm2_g, w_group_router, b_group_router,
           w_expert_router, b_expert_router, w_expert_gate, w_expert_up, w_expert_down,
           final_norm_g):
    B, S, D = x.shape
    assert D == D_MODEL and S % QB == 0 and S % TM_PROJ == 0 and w_in.shape[0] == 1
    assert QB == KC and TM_PROJ % KC == 0 and QB % CHUNK == 0
    N = B * S
    k_top = min(TOPK_MAX, S // 4)
    xr = x.reshape(N, D)

    wi = w_in[0]
    w1 = jnp.concatenate(
        [wi[:, 0:640], wi[:, 640:896], wi[:, 936:1448], wi[:, 896:936],
         jnp.zeros((D, W1_WIDTH - 1448), F32)], axis=1).astype(BF16)
    wgate = wi[:, 1448:].astype(BF16)
    g1 = norm1_g[0].reshape(1, D)
    kvg = kv_norm_g[0].reshape(1, KV_LATENT)
    wpool = w_pool[0].astype(BF16)
    pscale = pool_scale[0].reshape(1, POOL_WIDTH)
    wuk = jnp.transpose(w_uk[0], (1, 0, 2)).astype(BF16)
    wuvt = jnp.transpose(w_uv[0], (1, 2, 0)).astype(BF16)
    wr = jnp.concatenate(
        [w_group_router[0], w_expert_router[0].reshape(D, N_EXPERTS),
         jnp.zeros((D, LANES - N_GROUPS - N_EXPERTS), F32)], axis=1)
    br = jnp.concatenate(
        [b_group_router[0], b_expert_router[0].reshape(N_EXPERTS),
         jnp.zeros((LANES - N_GROUPS - N_EXPERTS,), F32)]).reshape(1, LANES)

    c_m, tail_m, pv_m = pl.pallas_call(
        _meta_kernel,
        out_shape=(jax.ShapeDtypeStruct((N_META, KV_LATENT), F32),
                   jax.ShapeDtypeStruct((N_META, LANES), F32),
                   jax.ShapeDtypeStruct((N_META, POOL_WIDTH), F32)),
        name="meta",
    )(meta_tokens, g1, w1, kvg)
    cm = c_m.astype(BF16)
    cmt = jnp.concatenate([cm.T, jnp.ones((1, N_META), BF16),
                           jnp.zeros((CT_ROWS - KV_LATENT - 1, N_META), BF16)], axis=0)
    ikm = tail_m[:, :IDX_DIM].astype(BF16)

    tpb = S // TM_PROJ
    tok = lambda w: pl.BlockSpec((TM_PROJ, w), lambda b, i: (b * tpb + i, 0))
    tok_t = lambda w: pl.BlockSpec((w, TM_PROJ), lambda b, i: (0, b * tpb + i))
    qt, c, ct, iqt, ik, iwt, yp = pl.pallas_call(
        _proj_kernel,
        grid=(B, tpb),
        in_specs=[tok(D), _const_spec((1, D)), _const_spec((D, W1_WIDTH)),
                  _const_spec((1, KV_LATENT)), _const_spec((N_META, POOL_WIDTH)),
                  _const_spec((len(POOL_WINDOWS), POOL_GROUP, POOL_GROUP)),
                  _const_spec((1, POOL_WIDTH))],
        out_specs=[tok_t(ATTN_WIDTH), tok(KV_LATENT),
                   pl.BlockSpec((None, TM_PROJ // KC, CT_ROWS, KC), lambda b, i: (b, i, 0, 0)),
                   tok_t(IDX_HEADS * IDX_DIM), tok(IDX_DIM), tok_t(IDX_HEADS), tok(POOL_WIDTH)],
        out_shape=(jax.ShapeDtypeStruct((ATTN_WIDTH, N), BF16),
                   jax.ShapeDtypeStruct((N, KV_LATENT), BF16),
                   jax.ShapeDtypeStruct((B, S // KC, CT_ROWS, KC), BF16),
                   jax.ShapeDtypeStruct((IDX_HEADS * IDX_DIM, N), BF16),
                   jax.ShapeDtypeStruct((N, IDX_DIM), BF16),
                   jax.ShapeDtypeStruct((IDX_HEADS, N), F32),
                   jax.ShapeDtypeStruct((N, POOL_WIDTH), BF16)),
        scratch_shapes=[pltpu.VMEM((TM_PROJ + N_META, POOL_WIDTH), F32)],
        compiler_params=_params(2),
        name="proj",
    )(xr, g1, w1, kvg, pv_m, wpool, pscale)

    nqb = S // QB
    qcol = lambda w: pl.BlockSpec((w, QB), lambda b, j: (0, b * nqb + j))
    attn = pl.pallas_call(
        functools.partial(_attn_kernel, k_top=float(k_top)),
        grid=(B, nqb),
        in_specs=[pl.BlockSpec((S, IDX_DIM), lambda b, j: (b, 0)),
                  _const_spec((N_META, IDX_DIM)),
                  pl.BlockSpec((S, KV_LATENT), lambda b, j: (b, 0)),
                  _const_spec((N_META, KV_LATENT)),
                  pl.BlockSpec((None, S // KC, CT_ROWS, KC), lambda b, j: (b, 0, 0, 0)),
                  _const_spec((CT_ROWS, N_META)),
                  qcol(IDX_HEADS * IDX_DIM), qcol(IDX_HEADS), qcol(ATTN_WIDTH),
                  _const_spec((N_HEADS, KV_LATENT, HEAD_DIM)),
                  _const_spec((N_HEADS, HEAD_DIM, KV_LATENT))],
        out_specs=pl.BlockSpec((QB, ATTN_WIDTH), lambda b, j: (b * nqb + j, 0)),
        out_shape=jax.ShapeDtypeStruct((N, ATTN_WIDTH), BF16),
        scratch_shapes=[pltpu.VMEM((S + N_META, QB), F32), pltpu.VMEM((S + N_META, QB), BF16),
                        pltpu.VMEM((KV_LATENT, N_HEADS * QB), BF16),
                        pltpu.VMEM((KC + N_META, N_HEADS * QB), F32),
                        pltpu.VMEM((1, N_HEADS * QB), F32),
                        pltpu.VMEM((CT_ROWS, N_HEADS * QB), F32),
                        pltpu.VMEM((ATTN_WIDTH, QB), F32)],
        compiler_params=_params(2),
        name="attn",
    )(ik, ikm, c, cm, ct, cmt, iqt, iwt, qt, wuk, wuvt)

    wr_hi = wr.astype(BF16)
    wr_lo = (wr - wr_hi.astype(F32)).astype(BF16)
    tok1 = lambda w: pl.BlockSpec((TM_MIX, w), lambda i: (i, 0))
    h1, u2, route, tile_counts = pl.pallas_call(
        _mix_kernel,
        grid=(N // TM_MIX,),
        in_specs=[tok1(D), _const_spec((1, D)), _const_spec((D, 2 * D)), tok1(ATTN_WIDTH),
                  tok1(POOL_WIDTH), _const_spec((ATTN_WIDTH, D)), _const_spec((POOL_WIDTH, D)),
                  _const_spec((D, D)), _const_spec((1, D)), _const_spec((D, 2 * LANES)),
                  _const_spec((D, LANES)), _const_spec((1, LANES))],
        out_specs=[tok1(D), pl.BlockSpec((TM_MIX * SUBLANES, LANES), lambda i: (i, 0)),
                   pl.BlockSpec((SUBLANES, TM_MIX), lambda i: (0, i)),
                   pl.BlockSpec((SUBLANES, LANES), lambda i: (i, 0))],
        out_shape=(jax.ShapeDtypeStruct((N, D), F32),
                   jax.ShapeDtypeStruct((N * SUBLANES, LANES), F32),
                   jax.ShapeDtypeStruct((SUBLANES, N), F32),
                   jax.ShapeDtypeStruct((N // TM_MIX * SUBLANES, LANES), F32)),
        scratch_shapes=[pltpu.VMEM((TM_MIX, D), BF16)],
        compiler_params=_params(1),
        name="mix",
    )(xr, g1, wgate, attn, yp, w_branch_attn[0].astype(BF16), w_branch_pool[0].astype(BF16),
      w_out[0].astype(BF16), norm2_g[0].reshape(1, D),
      jnp.concatenate([wr_hi, wr_lo], axis=1), wr_hi, br)

    nst = N // TS_MOE
    n_asg = 2 * TS_MOE
    per_tile = lambda a: a.reshape(2, nst, TS_MOE).transpose(1, 0, 2).reshape(nst, n_asg)
    eid = per_tile(route[0:2].astype(jnp.int32))
    wts = per_tile(route[2:4]).reshape(nst * n_asg)
    order = jnp.argsort(eid, axis=1, stable=True).astype(jnp.int32)
    slot = jnp.argsort(order, axis=1).astype(jnp.int32).reshape(nst * n_asg)
    tok_sorted = jnp.pad(order % TS_MOE, ((0, 0), (0, LIST_PAD))).reshape(
        nst * (n_asg + LIST_PAD))
    counts = tile_counts.reshape(nst, TS_MOE // TM_MIX, SUBLANES, LANES)[
        :, :, 0, N_GROUPS:N_GROUPS + N_EXPERTS].sum(axis=1).astype(jnp.int32)
    starts = (jnp.cumsum(counts, axis=1) - counts).reshape(nst * N_EXPERTS)
    counts = counts.reshape(nst * N_EXPERTS)

    weg = w_expert_gate[0].reshape(N_EXPERTS, D, EXPERT_HIDDEN)
    weu = w_expert_up[0].reshape(N_EXPERTS, D, EXPERT_HIDDEN)
    wed = w_expert_down[0].reshape(N_EXPERTS, EXPERT_HIDDEN, D)
    smem = lambda n: pl.BlockSpec((n,), lambda s, e, *_: (s,), memory_space=pltpu.SMEM)
    y = pl.pallas_call(
        _moe_kernel,
        grid_spec=pltpu.PrefetchScalarGridSpec(
            num_scalar_prefetch=2,
            grid=(nst, N_EXPERTS),
            in_specs=[smem(n_asg + LIST_PAD), smem(n_asg), smem(n_asg),
                      pl.BlockSpec((TS_MOE * SUBLANES, LANES), lambda s, e, *_: (s, 0)),
                      pl.BlockSpec(memory_space=pl.ANY), pl.BlockSpec(memory_space=pl.ANY),
                      pl.BlockSpec(memory_space=pl.ANY)],
            out_specs=pl.BlockSpec((TS_MOE * SUBLANES, LANES), lambda s, e, *_: (s, 0),
                                   pipeline_mode=pl.Buffered(1)),
            scratch_shapes=[pltpu.VMEM(((n_asg + TMX_MOE) * SUBLANES, LANES), F32),
                            pltpu.VMEM((TMX_MOE * SUBLANES, LANES), F32),
                            pltpu.VMEM((TMX_MOE * SUBLANES, LANES), F32),
                            pltpu.VMEM((TMX_MOE * SUBLANES, LANES), F32),
                            pltpu.VMEM((MOE_WEIGHT_BUFFERS, D, EXPERT_HIDDEN), F32),
                            pltpu.VMEM((MOE_WEIGHT_BUFFERS, D, EXPERT_HIDDEN), F32),
                            pltpu.VMEM((MOE_WEIGHT_BUFFERS, EXPERT_HIDDEN, D), F32),
                            pltpu.SemaphoreType.DMA((3, MOE_WEIGHT_BUFFERS))]),
        out_shape=jax.ShapeDtypeStruct((N * SUBLANES, LANES), F32),
        compiler_params=_params(2),
        name="moe",
    )(starts, counts, tok_sorted, slot, wts, u2, weg, weu, wed)

    ftok = pl.BlockSpec((TM_FINAL, D), lambda i: (i, 0))
    out = pl.pallas_call(
        _final_kernel,
        grid=(N // TM_FINAL,),
        in_specs=[ftok, pl.BlockSpec((TM_FINAL * SUBLANES, LANES), lambda i: (i, 0)),
                  _const_spec((1, D))],
        out_specs=ftok,
        out_shape=jax.ShapeDtypeStruct((N, D), F32),
        compiler_params=_params(1),
        name="final",
    )(h1, y, final_norm_g.reshape(1, D))
    return out.reshape(B, S, D)
```

```python
import functools

import jax
import jax.numpy as jnp
from jax import lax
from jax.experimental import pallas as pl
from jax.experimental.pallas import tpu as pltpu

F32 = jnp.float32
BF16 = jnp.bfloat16

D_MODEL = 1024
CHUNK = 64
N_META = 16
N_HEADS = 8
HEAD_DIM = 64
ATTN_WIDTH = N_HEADS * HEAD_DIM
KV_LATENT = 128
IDX_HEADS = 8
IDX_DIM = 32
TOPK_MAX = 256
ATTN_SCALE = HEAD_DIM ** -0.5
IDX_SCALE = (IDX_HEADS ** -0.5) * (IDX_DIM ** -0.5)
POOL_WINDOWS = (2, 4, 8, 16)
POOL_WIDTH = 512
POOL_GROUP = 128
N_GROUPS = 4
EXPERTS_PER_GROUP = 8
N_EXPERTS = N_GROUPS * EXPERTS_PER_GROUP
EXPERT_HIDDEN = 256
EPS = 1e-6

LANES = 128
SUBLANES = 8
W1_WIDTH = 1536
NEG = -1e30
POS = 1e30
TINY = 1e-30
BF16_CLEARANCE = 2.0 ** -6
VMEM_LIMIT = 56 * 1024 * 1024

TM_PROJ = 2048
TM_MIX = 1024
MIX_ROWS = 256
MIX_COLS = 256
QB = 256
KC = 256
MAX_BISECT = 40
BISECT_PER_CHECK = 8
FIRST_TIE_CHECK = 16
CT_ROWS = KV_LATENT + 2 * SUBLANES
LOG2E = 1.4426950408889634
TS_MOE = 2048
TMX_MOE = 160
MOE_WEIGHT_BUFFERS = 3
LIST_PAD = 1024
TM_FINAL = 1024


def _rms(x, g):
    return x * lax.rsqrt(jnp.mean(x * x, axis=-1, keepdims=True) + EPS) * g


def _dot(a, b):
    return jnp.dot(a, b, preferred_element_type=F32)


def _meta_kernel(m_ref, g1_ref, w1_ref, kvg_ref, c_ref, tail_ref, pv_ref):
    u = _rms(m_ref[...], g1_ref[...]).astype(BF16)
    p = _dot(u, w1_ref[...])
    c_ref[...] = _rms(p[:, 512:640], kvg_ref[...])
    pv_ref[...] = p[:, 896:1408]
    tail_ref[...] = p[:, 1408:1536]


def _proj_kernel(x_ref, g1_ref, w1_ref, kvg_ref, pvmeta_ref, wpool_ref, pscale_ref,
                 qt_ref, c_ref, ct_ref, iqt_ref, ik_ref, iwt_ref, yp_ref, ext_ref):
    tm = x_ref.shape[0]

    @pl.when(pl.program_id(1) == 0)
    def _():
        ext_ref[0:N_META, :] = pvmeta_ref[...]

    for r0 in range(0, tm, KC):
        rs = slice(r0, r0 + KC)
        u = _rms(x_ref[rs, :], g1_ref[...]).astype(BF16)
        p = _dot(u, w1_ref[...])
        qt_ref[:, rs] = p[:, 0:512].T.astype(BF16)
        c = _rms(p[:, 512:640], kvg_ref[...])
        c_ref[rs, :] = c.astype(BF16)
        ct_ref[r0 // KC] = jnp.concatenate(
            [c.T, jnp.ones((1, KC), F32), jnp.zeros((CT_ROWS - KV_LATENT - 1, KC), F32)],
            axis=0).astype(BF16)
        iqt_ref[:, rs] = p[:, 640:896].T.astype(BF16)
        tail = p[:, 1408:1536]
        ik_ref[rs, :] = tail[:, 0:IDX_DIM].astype(BF16)
        iwt_ref[:, rs] = tail.T[IDX_DIM:IDX_DIM + IDX_HEADS, :] * IDX_SCALE
        pv = p[:, 896:1408]
        ext_ref[N_META + r0:N_META + r0 + KC, :] = pv
        for g, w in enumerate(POOL_WINDOWS):
            cols = slice(g * POOL_GROUP, (g + 1) * POOL_GROUP)
            acc = pv[:, cols]
            for k in range(1, w):
                acc = acc + ext_ref[N_META + r0 - k:N_META + r0 - k + KC, cols]
            d = acc * (1.0 / w) - pv[:, cols]
            y = _dot(d.astype(BF16), wpool_ref[g]) * pscale_ref[:, cols]
            yp_ref[rs, cols] = y.astype(BF16)
    ext_ref[0:N_META, :] = ext_ref[tm:tm + N_META, :]


def _attn_kernel(ik_ref, ikm_ref, c_ref, cm_ref, ct_ref, cmt_ref, iqt_ref, iwt_ref, qt_ref,
                 wuk_ref, wuvt_ref, o_ref,
                 s_scr, s16_scr, qat_scr, lg_scr, m_scr, acc_scr, ot_scr, *, k_top):
    n_real = c_ref.shape[0]
    qb = iqt_ref.shape[1]
    j = pl.program_id(1)
    nkc = j + 1
    qchunk = (j * qb + lax.broadcasted_iota(jnp.int32, (1, qb), 1)) // CHUNK
    meta_rows = pl.ds(n_real, N_META)

    def rows(kc):
        return pl.ds(pl.multiple_of(kc * KC, KC), KC)

    def fold(x, op):
        groups = x.shape[0] // SUBLANES
        chains = 4 if groups % 4 == 0 else 1
        x = x.reshape(groups // chains, chains, SUBLANES, qb)
        return op(op(x, axis=0), axis=0)

    iqt_heads = [iqt_ref[h * IDX_DIM:(h + 1) * IDX_DIM, :] for h in range(IDX_HEADS)]

    def scores(ik_rows):
        acc = None
        for h in range(IDX_HEADS):
            t = jnp.maximum(_dot(ik_rows, iqt_heads[h]), 0.0) * iwt_ref[h:h + 1, :]
            acc = t if acc is None else acc + t
        return acc

    def score_body(kc, carry):
        mn, mx = carry
        sc = scores(ik_ref[rows(kc), :])
        s16_scr[rows(kc), :] = sc.astype(BF16)
        return jnp.minimum(mn, fold(sc, jnp.min)), jnp.maximum(mx, fold(sc, jnp.max))

    def score_pair(i, carry):
        return score_body(2 * i + 1, score_body(2 * i, carry))

    mn8, mx8 = lax.fori_loop(0, j // 2, score_pair, (jnp.full((SUBLANES, qb), POS, F32),
                                                     jnp.full((SUBLANES, qb), NEG, F32)))
    mn8, mx8 = lax.cond(j % 2 == 1, lambda c: score_body(j - 1, c), lambda c: c, (mn8, mx8))
    sc_all = scores(jnp.concatenate([ik_ref[rows(j), :], ikm_ref[...]], axis=0))
    sc = sc_all[0:KC, :]
    sm = sc_all[KC:KC + N_META, :]
    s16_scr[meta_rows, :] = sm.astype(BF16)
    mn0 = jnp.min(sm, axis=0, keepdims=True)
    mx0 = jnp.max(sm, axis=0, keepdims=True)
    adm = (j * KC + lax.broadcasted_iota(jnp.int32, (KC, 1), 0)) // CHUNK <= qchunk
    s16_scr[rows(j), :] = jnp.where(adm, sc, NEG).astype(BF16)
    mn8 = jnp.minimum(mn8, fold(jnp.where(adm, sc, POS), jnp.min))
    mx8 = jnp.maximum(mx8, fold(jnp.where(adm, sc, NEG), jnp.max))
    mn = jnp.minimum(mn0, jnp.min(mn8, axis=0, keepdims=True))
    mx = jnp.maximum(mx0, jnp.max(mx8, axis=0, keepdims=True))

    pack = 2 * SUBLANES

    def as16(v):
        return jnp.broadcast_to(v, (pack, qb)).astype(BF16)

    def tree(parts, op):
        while len(parts) > 1:
            parts = ([op(parts[i], parts[i + 1]) for i in range(0, len(parts) - 1, 2)]
                     + ([parts[-1]] if len(parts) % 2 else []))
        return parts[0]

    def groups(x):
        return [x[i * pack:(i + 1) * pack, :] for i in range(x.shape[0] // pack)]

    one16 = jnp.ones((pack, qb), BF16)
    zero16 = jnp.zeros((pack, qb), BF16)

    def count_ge(th):
        th16 = as16(th)

        def ones(x):
            return tree([jnp.where(g >= th16, one16, zero16) for g in groups(x)], jnp.add)

        def body(kc, cnt):
            return cnt + ones(s16_scr[rows(kc), :])
        cnt = lax.fori_loop(0, nkc, body, ones(s16_scr[meta_rows, :]))
        return jnp.sum(cnt.astype(F32), axis=0, keepdims=True)

    def band_extent(lo, hi):
        lo16, hi16 = as16(lo), as16(hi)
        pos16 = jnp.full((pack, qb), POS, BF16)
        neg16 = jnp.full((pack, qb), NEG, BF16)

        def ext(x):
            return (tree([jnp.where(g >= lo16, g, pos16) for g in groups(x)], jnp.minimum),
                    tree([jnp.where(g < hi16, g, neg16) for g in groups(x)], jnp.maximum))

        def body(kc, carry):
            bmin, bmax = ext(s16_scr[rows(kc), :])
            return jnp.minimum(carry[0], bmin), jnp.maximum(carry[1], bmax)
        bmin, bmax = lax.fori_loop(0, nkc, body, ext(s16_scr[meta_rows, :]))
        return (jnp.min(bmin.astype(F32), axis=0, keepdims=True),
                jnp.max(bmax.astype(F32), axis=0, keepdims=True))

    def any_lane(flags):
        return jnp.max(flags)

    def bisect_cond(carry):
        it, _, _, _, pending = carry
        return jnp.logical_and(it < MAX_BISECT, pending > 0)

    def bisect_body(carry):
        it, lo, hi, cnt_lo, _ = carry
        for _ in range(BISECT_PER_CHECK):
            mid = (lo + (hi - lo) * 0.5).astype(BF16).astype(F32)
            cnt = count_ge(mid)
            ge = cnt >= k_top
            lo = jnp.where(ge, mid, lo)
            hi = jnp.where(ge, hi, mid)
            cnt_lo = jnp.where(ge, cnt, cnt_lo)
        it = it + BISECT_PER_CHECK
        over = jnp.where(cnt_lo > k_top, 1, 0)
        pending = any_lane(over)

        def tied_check():
            bmin, bmax = band_extent(lo, hi)
            return any_lane(jnp.where(bmin < bmax, over, 0))

        pending = lax.cond(jnp.logical_and(pending > 0, it >= FIRST_TIE_CHECK),
                           tied_check, lambda: pending)
        return it, lo, hi, cnt_lo, pending

    n_adm = (N_META + CHUNK * (qchunk + 1)).astype(F32)
    lo0 = mn.astype(BF16).astype(F32)
    hi0 = (mx + jnp.abs(mx) * BF16_CLEARANCE + TINY).astype(BF16).astype(F32)
    _, lo, hi, cnt_lo, _ = lax.while_loop(
        bisect_cond, bisect_body,
        (jnp.int32(0), lo0, hi0, n_adm, any_lane(jnp.where(n_adm > k_top, 1, 0))))
    pending = any_lane(jnp.where(cnt_lo > k_top, 1, 0))

    @pl.when(pending == 0)
    def _():
        def body(kc, _):
            s_scr[rows(kc), :] = jnp.where(s16_scr[rows(kc), :].astype(F32) >= lo, 0.0, NEG)
            return 0
        lax.fori_loop(0, nkc, body, 0)
        s_scr[meta_rows, :] = jnp.where(s16_scr[meta_rows, :].astype(F32) >= lo, 0.0, NEG)

    @pl.when(pending > 0)
    def _():
        need = k_top - count_ge(hi)

        def pick(sv, tri, before):
            band = jnp.where(sv >= lo, jnp.where(sv < hi, 1.0, 0.0), 0.0)
            rank = _dot(tri, band.astype(BF16)) + before
            take = jnp.where(rank <= need, band, 0.0)
            bias = jnp.where(sv >= hi, 0.0, jnp.where(take > 0.5, 0.0, NEG))
            return bias, before + jnp.sum(fold(band, jnp.sum), axis=0, keepdims=True)

        def tri(n):
            return jnp.where(lax.broadcasted_iota(jnp.int32, (n, n), 0)
                             >= lax.broadcasted_iota(jnp.int32, (n, n), 1), 1.0, 0.0).astype(BF16)

        bias_m, before = pick(s16_scr[meta_rows, :].astype(F32), tri(N_META),
                              jnp.zeros((1, qb), F32))
        s_scr[meta_rows, :] = bias_m
        tri_kc = tri(KC)

        def body(kc, before):
            bias, before = pick(s16_scr[rows(kc), :].astype(F32), tri_kc, before)
            s_scr[rows(kc), :] = bias
            return before
        lax.fori_loop(0, nkc, body, before)

    def head(h):
        return slice(h * qb, (h + 1) * qb)

    for h in range(N_HEADS):
        qat_scr[:, head(h)] = (_dot(wuk_ref[h], qt_ref[h * HEAD_DIM:(h + 1) * HEAD_DIM, :])
                               * (ATTN_SCALE * LOG2E)).astype(BF16)
    m_scr[...] = jnp.full(m_scr.shape, 0.5 * NEG, F32)
    acc_scr[...] = jnp.zeros(acc_scr.shape, F32)

    def attend(c_rows, bias, values):
        r = c_rows.shape[0]
        lg_scr[0:r, :] = _dot(c_rows, qat_scr[...])
        for h in range(N_HEADS):
            lg = lg_scr[0:r, head(h)] + bias
            m_old = m_scr[:, head(h)]
            m_new = jnp.maximum(m_old, jnp.max(lg, axis=0, keepdims=True))
            p = jnp.exp2(lg - m_new).astype(BF16)
            m_scr[:, head(h)] = m_new
            pv = None
            for ct_cols, rs in values:
                t = _dot(ct_cols, p[rs, :])
                pv = t if pv is None else pv + t
            acc_scr[:, head(h)] = acc_scr[:, head(h)] * jnp.exp2(m_old - m_new) + pv

    def attend_chunk(kc):
        attend(c_ref[rows(kc), :], s_scr[rows(kc), :], [(ct_ref[kc], slice(0, KC))])

    def attend_pair(i, _):
        attend_chunk(2 * i)
        attend_chunk(2 * i + 1)
        return 0
    lax.fori_loop(0, j // 2, attend_pair, 0)

    @pl.when(j % 2 == 1)
    def _():
        attend_chunk(j - 1)

    attend(jnp.concatenate([c_ref[rows(j), :], cm_ref[...]], axis=0),
           jnp.concatenate([s_scr[rows(j), :], s_scr[meta_rows, :]], axis=0),
           [(ct_ref[j], slice(0, KC)), (cmt_ref[...], slice(KC, KC + N_META))])

    for h in range(N_HEADS):
        olat = acc_scr[0:KV_LATENT, head(h)] / acc_scr[KV_LATENT:KV_LATENT + 1, head(h)]
        ot_scr[h * HEAD_DIM:(h + 1) * HEAD_DIM, :] = _dot(wuvt_ref[h], olat.astype(BF16))
    o_ref[...] = ot_scr[...].T.astype(BF16)


def _mix_kernel(x_ref, g1_ref, wgate_ref, attn_ref, yp_ref, wba_ref, wbp_ref, wout_ref,
                g2_ref, wr_ref, wrhi_ref, br_ref, h1_ref, u2_ref, route_ref, count_ref,
                merged_scr):
    nsub = D_MODEL // LANES
    counts = None
    for r0 in range(0, x_ref.shape[0], MIX_ROWS):
        rows = pl.ds(r0, MIX_ROWS)
        c = _mix_group(x_ref.at[rows], g1_ref, wgate_ref, attn_ref.at[rows], yp_ref.at[rows],
                       wba_ref, wbp_ref, wout_ref, g2_ref, wr_ref, wrhi_ref, br_ref,
                       h1_ref.at[rows], u2_ref.at[pl.ds(r0 * nsub, MIX_ROWS * nsub)],
                       route_ref.at[:, rows], merged_scr.at[rows])
        counts = c if counts is None else counts + c
    count_ref[...] = jnp.broadcast_to(counts, count_ref.shape)


def _mix_group(x_ref, g1_ref, wgate_ref, attn_ref, yp_ref, wba_ref, wbp_ref, wout_ref,
               g2_ref, wr_ref, wrhi_ref, br_ref, h1_ref, u2_ref, route_ref, merged_scr):
    x = x_ref[...]
    u = _rms(x, g1_ref[...]).astype(BF16)
    attn = attn_ref[...]
    yp = yp_ref[...]
    for cb in range(D_MODEL // MIX_COLS):
        ca = slice(cb * MIX_COLS, (cb + 1) * MIX_COLS)
        cp = slice(D_MODEL + cb * MIX_COLS, D_MODEL + (cb + 1) * MIX_COLS)
        g_attn = 1.0 / (1.0 + jnp.exp(-_dot(u, wgate_ref[:, ca])))
        g_pool = 1.0 / (1.0 + jnp.exp(-_dot(u, wgate_ref[:, cp])))
        merged_scr[:, ca] = (g_attn * _dot(attn, wba_ref[:, ca])
                             + g_pool * _dot(yp, wbp_ref[:, ca])).astype(BF16)
    h1 = x + _dot(merged_scr[...], wout_ref[...])
    h1_ref[...] = h1
    u2 = _rms(h1, g2_ref[...])
    nsub = D_MODEL // LANES
    for s in range(nsub):
        u2_ref[pl.ds(s, x.shape[0], stride=nsub), :] = u2[:, s * LANES:(s + 1) * LANES]
    u2_hi = u2.astype(BF16)

    u2_lo = (u2 - u2_hi.astype(F32)).astype(BF16)
    hi_both = _dot(u2_hi, wr_ref[...])
    lg = hi_both[:, :LANES] + hi_both[:, LANES:] + _dot(u2_lo, wrhi_ref[...]) + br_ref[...]
    lane = lax.broadcasted_iota(jnp.int32, lg.shape, 1)
    is_g = lane < N_GROUPS
    gl = jnp.where(is_g, lg, NEG)
    gmax = jnp.max(gl, axis=1, keepdims=True)
    gidx = jnp.min(jnp.where(gl == gmax, lane, LANES), axis=1, keepdims=True)
    p_g = 1.0 / jnp.sum(jnp.where(is_g, jnp.exp(gl - gmax), 0.0), axis=1, keepdims=True)
    e_lane = lane - N_GROUPS
    lane_grp = jnp.where(e_lane >= 0,
                         jnp.where(e_lane < N_EXPERTS, e_lane // EXPERTS_PER_GROUP, -1), -1)
    in_grp = lane_grp == gidx
    el = jnp.where(in_grp, lg, NEG)
    t1 = jnp.max(el, axis=1, keepdims=True)
    i1 = jnp.min(jnp.where(el == t1, lane, LANES), axis=1, keepdims=True)
    el2 = jnp.where(lane == i1, NEG, el)
    t2 = jnp.max(el2, axis=1, keepdims=True)
    i2 = jnp.min(jnp.where(el2 == t2, lane, LANES), axis=1, keepdims=True)
    r = jnp.exp(t2 - t1)
    p1 = 1.0 / (1.0 + r)
    p2 = r * p1
    e1 = (i1 - N_GROUPS).astype(F32)
    e2 = (i2 - N_GROUPS).astype(F32)
    record = jnp.where(lane == 0, e1, jnp.where(lane == 1, e2, jnp.where(
        lane == 2, p1 * p_g, jnp.where(lane == 3, p2 * p_g, 0.0))))
    route_ref[...] = record.T[0:SUBLANES, :]
    chosen = jnp.where(lane == i1, 1.0, jnp.where(lane == i2, 1.0, 0.0))
    return jnp.sum(chosen, axis=0, keepdims=True)


def _moe_kernel(start_ref, cnt_ref, tok_ref, slot_ref, wt_ref, x2_ref, wg_hbm, wu_hbm, wd_hbm,
                y2_ref, r2_scr, ga_scr, gb_scr, rt_scr, wg_buf, wu_buf, wd_buf, w_sem):
    ts = x2_ref.shape[0] // SUBLANES
    tmx = ga_scr.shape[0] // SUBLANES
    st = pl.program_id(0)
    e = pl.program_id(1)
    n_e = pl.num_programs(1)
    nsub = D_MODEL // LANES
    step = st * n_e + e
    n_steps = pl.num_programs(0) * n_e

    def weight_copies(g):
        ex = g % n_e
        buf = g % MOE_WEIGHT_BUFFERS
        return [pltpu.make_async_copy(hbm.at[ex], vbuf.at[buf], w_sem.at[k, buf])
                for k, (hbm, vbuf) in enumerate(((wg_hbm, wg_buf), (wu_hbm, wu_buf),
                                                 (wd_hbm, wd_buf)))]

    @pl.when(step == 0)
    def _():
        for g in range(MOE_WEIGHT_BUFFERS - 1):
            for cp in weight_copies(g):
                cp.start()

    @pl.when(step + MOE_WEIGHT_BUFFERS - 1 < n_steps)
    def _():
        for cp in weight_copies(step + MOE_WEIGHT_BUFFERS - 1):
            cp.start()

    for cp in weight_copies(step):
        cp.wait()
    wbuf = step % MOE_WEIGHT_BUFFERS

    def vreg_rows(i):
        return pl.ds(pl.multiple_of(i * SUBLANES, SUBLANES), SUBLANES)

    def gather(g_scr, base):
        def gather_body(r8, _):
            for u in range(SUBLANES):
                r = r8 * SUBLANES + u
                g_scr[vreg_rows(r), :] = x2_ref[vreg_rows(tok_ref[base + r]), :]
            return 0
        lax.fori_loop(0, tmx // SUBLANES, gather_body, 0)

    def expert_ffn(g_scr, base):
        xg = jnp.concatenate([g_scr[pl.ds(s, tmx, stride=nsub), :] for s in range(nsub)],
                             axis=1).astype(BF16)
        a = _dot(xg, wg_buf[wbuf].astype(BF16))
        b = _dot(xg, wu_buf[wbuf].astype(BF16))
        hg = (a * (1.0 / (1.0 + jnp.exp(-a))) * b).astype(BF16)
        yr = _dot(hg, wd_buf[wbuf].astype(BF16))
        for s in range(nsub):
            rt_scr[pl.ds(s, tmx, stride=nsub), :] = yr[:, s * LANES:(s + 1) * LANES]
        r2_scr[pl.ds(pl.multiple_of(base * SUBLANES, SUBLANES), tmx * SUBLANES), :] = rt_scr[...]

    start = start_ref[step]
    cnt = cnt_ref[step]

    @pl.when(e == 0)
    def _():
        gather(ga_scr, start)

    def run(cur_scr, nxt_scr):
        expert_ffn(cur_scr, start)
        if nxt_scr is not None:
            nxt = start_ref[step + 1]
            for r in range(tmx):
                nxt_scr[r * SUBLANES:(r + 1) * SUBLANES, :] = (
                    x2_ref[vreg_rows(tok_ref[nxt + r]), :])

        def extra_tile(i, _):
            gather(cur_scr, start + i * tmx)
            expert_ffn(cur_scr, start + i * tmx)
            return 0
        lax.fori_loop(1, (cnt + tmx - 1) // tmx, extra_tile, 0)

    last = e == n_e - 1
    even = e % 2 == 0

    @pl.when(jnp.logical_and(even, jnp.logical_not(last)))
    def _():
        run(ga_scr, gb_scr)

    @pl.when(jnp.logical_and(jnp.logical_not(even), jnp.logical_not(last)))
    def _():
        run(gb_scr, ga_scr)

    @pl.when(last)
    def _():
        run(ga_scr if (N_EXPERTS - 1) % 2 == 0 else gb_scr, None)

        def combine_body(t8, _):
            for u in range(SUBLANES):
                t = t8 * SUBLANES + u
                y2_ref[vreg_rows(t), :] = (
                    wt_ref[t] * r2_scr[vreg_rows(slot_ref[t]), :]
                    + wt_ref[ts + t] * r2_scr[vreg_rows(slot_ref[ts + t]), :])
            return 0
        lax.fori_loop(0, ts // SUBLANES, combine_body, 0)


def _final_kernel(h1_ref, y2_ref, gf_ref, o_ref):
    nsub = D_MODEL // LANES
    y = jnp.concatenate([y2_ref[pl.ds(s, h1_ref.shape[0], stride=nsub), :] for s in range(nsub)],
                        axis=1)
    o_ref[...] = _rms(h1_ref[...] + y, gf_ref[...])


def _const_spec(shape):
    nd = len(shape)
    return pl.BlockSpec(shape, lambda *_: (0,) * nd, pipeline_mode=pl.Buffered(1))


def _params(n_axes):
    return pltpu.CompilerParams(dimension_semantics=("arbitrary",) * n_axes,
                                vmem_limit_bytes=VMEM_LIMIT)


def kernel(x, meta_tokens, norm1_g, w_in, kv_norm_g, w_uk, w_uv, w_pool, pool_scale,
           w_branch_attn, w_branch_pool, w_out, norm2_g, w_group_router, b_group_router,
           w_expert_router, b_expert_router, w_expert_gate, w_expert_up, w_expert_down,
           final_norm_g):
    B, S, D = x.shape
    assert D == D_MODEL and S % QB == 0 and S % TM_PROJ == 0 and w_in.shape[0] == 1
    assert QB == KC and TM_PROJ % KC == 0 and QB % CHUNK == 0
    assert (B * S) % TS_MOE == 0 and TS_MOE % TM_MIX == 0 and TM_MIX % MIX_ROWS == 0
    assert TMX_MOE <= LIST_PAD and MAX_BISECT % BISECT_PER_CHECK == 0 and N_EXPERTS % 2 == 0
    N = B * S
    k_top = min(TOPK_MAX, S // 4)
    xr = x.reshape(N, D)

    wi = w_in[0]
    w1 = jnp.concatenate(
        [wi[:, 0:640], wi[:, 640:896], wi[:, 936:1448], wi[:, 896:936],
         jnp.zeros((D, W1_WIDTH - 1448), F32)], axis=1).astype(BF16)
    wgate = wi[:, 1448:].astype(BF16)
    g1 = norm1_g[0].reshape(1, D)
    kvg = kv_norm_g[0].reshape(1, KV_LATENT)
    wpool = w_pool[0].astype(BF16)
    pscale = pool_scale[0].reshape(1, POOL_WIDTH)
    wuk = jnp.transpose(w_uk[0], (1, 0, 2)).astype(BF16)
    wuvt = jnp.transpose(w_uv[0], (1, 2, 0)).astype(BF16)
    wr = jnp.concatenate(
        [w_group_router[0], w_expert_router[0].reshape(D, N_EXPERTS),
         jnp.zeros((D, LANES - N_GROUPS - N_EXPERTS), F32)], axis=1)
    br = jnp.concatenate(
        [b_group_router[0], b_expert_router[0].reshape(N_EXPERTS),
         jnp.zeros((LANES - N_GROUPS - N_EXPERTS,), F32)]).reshape(1, LANES)

    c_m, tail_m, pv_m = pl.pallas_call(
        _meta_kernel,
        out_shape=(jax.ShapeDtypeStruct((N_META, KV_LATENT), F32),
                   jax.ShapeDtypeStruct((N_META, LANES), F32),
                   jax.ShapeDtypeStruct((N_META, POOL_WIDTH), F32)),
        name="meta",
    )(meta_tokens, g1, w1, kvg)
    cm = c_m.astype(BF16)
    cmt = jnp.concatenate([cm.T, jnp.ones((1, N_META), BF16),
                           jnp.zeros((CT_ROWS - KV_LATENT - 1, N_META), BF16)], axis=0)
    ikm = tail_m[:, :IDX_DIM].astype(BF16)

    tpb = S // TM_PROJ
    tok = lambda w: pl.BlockSpec((TM_PROJ, w), lambda b, i: (b * tpb + i, 0))
    tok_t = lambda w: pl.BlockSpec((w, TM_PROJ), lambda b, i: (0, b * tpb + i))
    qt, c, ct, iqt, ik, iwt, yp = pl.pallas_call(
        _proj_kernel,
        grid=(B, tpb),
        in_specs=[tok(D), _const_spec((1, D)), _const_spec((D, W1_WIDTH)),
                  _const_spec((1, KV_LATENT)), _const_spec((N_META, POOL_WIDTH)),
                  _const_spec((len(POOL_WINDOWS), POOL_GROUP, POOL_GROUP)),
                  _const_spec((1, POOL_WIDTH))],
        out_specs=[tok_t(ATTN_WIDTH), tok(KV_LATENT),
                   pl.BlockSpec((None, TM_PROJ // KC, CT_ROWS, KC), lambda b, i: (b, i, 0, 0)),
                   tok_t(IDX_HEADS * IDX_DIM), tok(IDX_DIM), tok_t(IDX_HEADS), tok(POOL_WIDTH)],
        out_shape=(jax.ShapeDtypeStruct((ATTN_WIDTH, N), BF16),
                   jax.ShapeDtypeStruct((N, KV_LATENT), BF16),
                   jax.ShapeDtypeStruct((B, S // KC, CT_ROWS, KC), BF16),
                   jax.ShapeDtypeStruct((IDX_HEADS * IDX_DIM, N), BF16),
                   jax.ShapeDtypeStruct((N, IDX_DIM), BF16),
                   jax.ShapeDtypeStruct((IDX_HEADS, N), F32),
                   jax.ShapeDtypeStruct((N, POOL_WIDTH), BF16)),
        scratch_shapes=[pltpu.VMEM((TM_PROJ + N_META, POOL_WIDTH), F32)],
        compiler_params=_params(2),
        name="proj",
    )(xr, g1, w1, kvg, pv_m, wpool, pscale)

    nqb = S // QB
    qcol = lambda w: pl.BlockSpec((w, QB), lambda b, j: (0, b * nqb + j))
    attn = pl.pallas_call(
        functools.partial(_attn_kernel, k_top=float(k_top)),
        grid=(B, nqb),
        in_specs=[pl.BlockSpec((S, IDX_DIM), lambda b, j: (b, 0)),
                  _const_spec((N_META, IDX_DIM)),
                  pl.BlockSpec((S, KV_LATENT), lambda b, j: (b, 0)),
                  _const_spec((N_META, KV_LATENT)),
                  pl.BlockSpec((None, S // KC, CT_ROWS, KC), lambda b, j: (b, 0, 0, 0)),
                  _const_spec((CT_ROWS, N_META)),
                  qcol(IDX_HEADS * IDX_DIM), qcol(IDX_HEADS), qcol(ATTN_WIDTH),
                  _const_spec((N_HEADS, KV_LATENT, HEAD_DIM)),
                  _const_spec((N_HEADS, HEAD_DIM, KV_LATENT))],
        out_specs=pl.BlockSpec((QB, ATTN_WIDTH), lambda b, j: (b * nqb + j, 0)),
        out_shape=jax.ShapeDtypeStruct((N, ATTN_WIDTH), BF16),
        scratch_shapes=[pltpu.VMEM((S + N_META, QB), F32), pltpu.VMEM((S + N_META, QB), BF16),
                        pltpu.VMEM((KV_LATENT, N_HEADS * QB), BF16),
                        pltpu.VMEM((KC + N_META, N_HEADS * QB), F32),
                        pltpu.VMEM((1, N_HEADS * QB), F32),
                        pltpu.VMEM((CT_ROWS, N_HEADS * QB), F32),
                        pltpu.VMEM((ATTN_WIDTH, QB), F32)],
        compiler_params=_params(2),
        name="attn",
    )(ik, ikm, c, cm, ct, cmt, iqt, iwt, qt, wuk, wuvt)

    wr_hi = wr.astype(BF16)
    wr_lo = (wr - wr_hi.astype(F32)).astype(BF16)
    tok1 = lambda w: pl.BlockSpec((TM_MIX, w), lambda i: (i, 0))
    h1, u2, route, tile_counts = pl.pallas_call(
        _mix_kernel,
        grid=(N // TM_MIX,),
        in_specs=[tok1(D), _const_spec((1, D)), _const_spec((D, 2 * D)), tok1(ATTN_WIDTH),
                  tok1(POOL_WIDTH), _const_spec((ATTN_WIDTH, D)), _const_spec((POOL_WIDTH, D)),
                  _const_spec((D, D)), _const_spec((1, D)), _const_spec((D, 2 * LANES)),
                  _const_spec((D, LANES)), _const_spec((1, LANES))],
        out_specs=[tok1(D), pl.BlockSpec((TM_MIX * SUBLANES, LANES), lambda i: (i, 0)),
                   pl.BlockSpec((SUBLANES, TM_MIX), lambda i: (0, i)),
                   pl.BlockSpec((SUBLANES, LANES), lambda i: (i, 0))],
        out_shape=(jax.ShapeDtypeStruct((N, D), F32),
                   jax.ShapeDtypeStruct((N * SUBLANES, LANES), F32),
                   jax.ShapeDtypeStruct((SUBLANES, N), F32),
                   jax.ShapeDtypeStruct((N // TM_MIX * SUBLANES, LANES), F32)),
        scratch_shapes=[pltpu.VMEM((TM_MIX, D), BF16)],
        compiler_params=_params(1),
        name="mix",
    )(xr, g1, wgate, attn, yp, w_branch_attn[0].astype(BF16), w_branch_pool[0].astype(BF16),
      w_out[0].astype(BF16), norm2_g[0].reshape(1, D),
      jnp.concatenate([wr_hi, wr_lo], axis=1), wr_hi, br)

    nst = N // TS_MOE
    n_asg = 2 * TS_MOE
    per_tile = lambda a: a.reshape(2, nst, TS_MOE).transpose(1, 0, 2).reshape(nst, n_asg)
    eid = per_tile(route[0:2].astype(jnp.int32))
    wts = per_tile(route[2:4]).reshape(nst * n_asg)
    order = jnp.argsort(eid, axis=1, stable=True).astype(jnp.int32)
    slot = jnp.argsort(order, axis=1).astype(jnp.int32).reshape(nst * n_asg)
    tok_sorted = jnp.pad(order % TS_MOE, ((0, 0), (0, LIST_PAD))).reshape(
        nst * (n_asg + LIST_PAD))
    counts = tile_counts.reshape(nst, TS_MOE // TM_MIX, SUBLANES, LANES)[
        :, :, 0, N_GROUPS:N_GROUPS + N_EXPERTS].sum(axis=1).astype(jnp.int32)
    starts = (jnp.cumsum(counts, axis=1) - counts).reshape(nst * N_EXPERTS)
    counts = counts.reshape(nst * N_EXPERTS)

    weg = w_expert_gate[0].reshape(N_EXPERTS, D, EXPERT_HIDDEN)
    weu = w_expert_up[0].reshape(N_EXPERTS, D, EXPERT_HIDDEN)
    wed = w_expert_down[0].reshape(N_EXPERTS, EXPERT_HIDDEN, D)
    smem = lambda n: pl.BlockSpec((n,), lambda s, e, *_: (s,), memory_space=pltpu.SMEM)
    y = pl.pallas_call(
        _moe_kernel,
        grid_spec=pltpu.PrefetchScalarGridSpec(
            num_scalar_prefetch=2,
            grid=(nst, N_EXPERTS),
            in_specs=[smem(n_asg + LIST_PAD), smem(n_asg), smem(n_asg),
                      pl.BlockSpec((TS_MOE * SUBLANES, LANES), lambda s, e, *_: (s, 0)),
                      pl.BlockSpec(memory_space=pl.ANY), pl.BlockSpec(memory_space=pl.ANY),
                      pl.BlockSpec(memory_space=pl.ANY)],
            out_specs=pl.BlockSpec((TS_MOE * SUBLANES, LANES), lambda s, e, *_: (s, 0),
                                   pipeline_mode=pl.Buffered(1)),
            scratch_shapes=[pltpu.VMEM(((n_asg + TMX_MOE) * SUBLANES, LANES), F32),
                            pltpu.VMEM((TMX_MOE * SUBLANES, LANES), F32),
                            pltpu.VMEM((TMX_MOE * SUBLANES, LANES), F32),
                            pltpu.VMEM((TMX_MOE * SUBLANES, LANES), F32),
                            pltpu.VMEM((MOE_WEIGHT_BUFFERS, D, EXPERT_HIDDEN), F32),
                            pltpu.VMEM((MOE_WEIGHT_BUFFERS, D, EXPERT_HIDDEN), F32),
                            pltpu.VMEM((MOE_WEIGHT_BUFFERS, EXPERT_HIDDEN, D), F32),
                            pltpu.SemaphoreType.DMA((3, MOE_WEIGHT_BUFFERS))]),
        out_shape=jax.ShapeDtypeStruct((N * SUBLANES, LANES), F32),
        compiler_params=_params(2),
        name="moe",
    )(starts, counts, tok_sorted, slot, wts, u2, weg, weu, wed)

    ftok = pl.BlockSpec((TM_FINAL, D), lambda i: (i, 0))
    out = pl.pallas_call(
        _final_kernel,
        grid=(N // TM_FINAL,),
        in_specs=[ftok, pl.BlockSpec((TM_FINAL * SUBLANES, LANES), lambda i: (i, 0)),
                  _const_spec((1, D))],
        out_specs=ftok,
        out_shape=jax.ShapeDtypeStruct((N, D), F32),
        compiler_params=_params(1),
        name="final",
    )(h1, y, final_norm_g.reshape(1, D))
    return out.reshape(B, S, D)
```

```python
import functools

import jax
import jax.numpy as jnp
from jax import lax
from jax.experimental import pallas as pl
from jax.experimental.pallas import tpu as pltpu

F32 = jnp.float32
BF16 = jnp.bfloat16

D_MODEL = 1024
CHUNK = 64
N_META = 16
N_HEADS = 8
HEAD_DIM = 64
ATTN_WIDTH = N_HEADS * HEAD_DIM
KV_LATENT = 128
IDX_HEADS = 8
IDX_DIM = 32
TOPK_MAX = 256
ATTN_SCALE = HEAD_DIM ** -0.5
IDX_SCALE = (IDX_HEADS ** -0.5) * (IDX_DIM ** -0.5)
POOL_WINDOWS = (2, 4, 8, 16)
POOL_WIDTH = 512
POOL_GROUP = 128
N_GROUPS = 4
EXPERTS_PER_GROUP = 8
N_EXPERTS = N_GROUPS * EXPERTS_PER_GROUP
EXPERT_HIDDEN = 256
EPS = 1e-6

LANES = 128
SUBLANES = 8
W1_WIDTH = 1536
NEG = -1e30
POS = 1e30
TINY = 1e-30
BF16_CLEARANCE = 2.0 ** -6
VMEM_LIMIT = 56 * 1024 * 1024

TM_PROJ = 2048
TM_MIX = 1024
MIX_ROWS = 256
MIX_COLS = 256
QB = 256
KC = 256
MAX_BISECT = 40
BISECT_WARMUP = 16
BISECT_PER_CHECK = 4
CT_ROWS = KV_LATENT + 2 * SUBLANES
LOG2E = 1.4426950408889634
TS_MOE = 2048
TMX_MOE = 160
MOE_WEIGHT_BUFFERS = 3
LIST_PAD = 1024
TM_FINAL = 1024


def _rms(x, g):
    return x * lax.rsqrt(jnp.mean(x * x, axis=-1, keepdims=True) + EPS) * g


def _dot(a, b):
    return jnp.dot(a, b, preferred_element_type=F32)


def _meta_kernel(m_ref, g1_ref, w1_ref, kvg_ref, c_ref, tail_ref, pv_ref):
    u = _rms(m_ref[...], g1_ref[...]).astype(BF16)
    p = _dot(u, w1_ref[...])
    c_ref[...] = _rms(p[:, 512:640], kvg_ref[...])
    pv_ref[...] = p[:, 896:1408]
    tail_ref[...] = p[:, 1408:1536]


def _proj_kernel(x_ref, g1_ref, w1_ref, kvg_ref, pvmeta_ref, wpool_ref, pscale_ref,
                 qt_ref, c_ref, ct_ref, iqt_ref, ik_ref, iwt_ref, yp_ref, ext_ref):
    tm = x_ref.shape[0]

    @pl.when(pl.program_id(1) == 0)
    def _():
        ext_ref[0:N_META, :] = pvmeta_ref[...]

    for r0 in range(0, tm, KC):
        rs = slice(r0, r0 + KC)
        u = _rms(x_ref[rs, :], g1_ref[...]).astype(BF16)
        p = _dot(u, w1_ref[...])
        qt_ref[:, rs] = p[:, 0:512].T.astype(BF16)
        c = _rms(p[:, 512:640], kvg_ref[...])
        c_ref[rs, :] = c.astype(BF16)
        ct_ref[r0 // KC] = jnp.concatenate(
            [c.T, jnp.ones((1, KC), F32), jnp.zeros((CT_ROWS - KV_LATENT - 1, KC), F32)],
            axis=0).astype(BF16)
        iqt_ref[:, rs] = p[:, 640:896].T.astype(BF16)
        tail = p[:, 1408:1536]
        ik_ref[rs, :] = tail[:, 0:IDX_DIM].astype(BF16)
        iwt_ref[:, rs] = tail.T[IDX_DIM:IDX_DIM + IDX_HEADS, :] * IDX_SCALE
        pv = p[:, 896:1408]
        ext_ref[N_META + r0:N_META + r0 + KC, :] = pv
        for g, w in enumerate(POOL_WINDOWS):
            cols = slice(g * POOL_GROUP, (g + 1) * POOL_GROUP)
            acc = pv[:, cols]
            for k in range(1, w):
                acc = acc + ext_ref[N_META + r0 - k:N_META + r0 - k + KC, cols]
            d = acc * (1.0 / w) - pv[:, cols]
            y = _dot(d.astype(BF16), wpool_ref[g]) * pscale_ref[:, cols]
            yp_ref[rs, cols] = y.astype(BF16)
    ext_ref[0:N_META, :] = ext_ref[tm:tm + N_META, :]


def _attn_kernel(ik_ref, ikm_ref, c_ref, cm_ref, ct_ref, cmt_ref, iqt_ref, iwt_ref, qt_ref,
                 wuk_ref, wuvt_ref, o_ref,
                 s_scr, s16_scr, qat_scr, lg_scr, m_scr, acc_scr, ot_scr, *, k_top):
    n_real = c_ref.shape[0]
    qb = iqt_ref.shape[1]
    j = pl.program_id(1)
    nkc = j + 1
    qchunk = (j * qb + lax.broadcasted_iota(jnp.int32, (1, qb), 1)) // CHUNK
    meta_rows = pl.ds(n_real, N_META)

    def rows(kc):
        return pl.ds(pl.multiple_of(kc * KC, KC), KC)

    def fold(x, op):
        groups = x.shape[0] // SUBLANES
        chains = 4 if groups % 4 == 0 else 1
        x = x.reshape(groups // chains, chains, SUBLANES, qb)
        return op(op(x, axis=0), axis=0)

    iqt_heads = [iqt_ref[h * IDX_DIM:(h + 1) * IDX_DIM, :] for h in range(IDX_HEADS)]

    def scores(ik_rows):
        acc = None
        for h in range(IDX_HEADS):
            t = jnp.maximum(_dot(ik_rows, iqt_heads[h]), 0.0) * iwt_ref[h:h + 1, :]
            acc = t if acc is None else acc + t
        return acc

    def score_body(kc, carry):
        mn, mx = carry
        sc = scores(ik_ref[rows(kc), :])
        s16_scr[rows(kc), :] = sc.astype(BF16)
        return jnp.minimum(mn, fold(sc, jnp.min)), jnp.maximum(mx, fold(sc, jnp.max))

    def score_pair(i, carry):
        return score_body(2 * i + 1, score_body(2 * i, carry))

    mn8, mx8 = lax.fori_loop(0, j // 2, score_pair, (jnp.full((SUBLANES, qb), POS, F32),
                                                     jnp.full((SUBLANES, qb), NEG, F32)))
    mn8, mx8 = lax.cond(j % 2 == 1, lambda c: score_body(j - 1, c), lambda c: c, (mn8, mx8))
    sc_all = scores(jnp.concatenate([ik_ref[rows(j), :], ikm_ref[...]], axis=0))
    sc = sc_all[0:KC, :]
    sm = sc_all[KC:KC + N_META, :]
    s16_scr[meta_rows, :] = sm.astype(BF16)
    mn0 = jnp.min(sm, axis=0, keepdims=True)
    mx0 = jnp.max(sm, axis=0, keepdims=True)
    adm = (j * KC + lax.broadcasted_iota(jnp.int32, (KC, 1), 0)) // CHUNK <= qchunk
    s16_scr[rows(j), :] = jnp.where(adm, sc, NEG).astype(BF16)
    mn8 = jnp.minimum(mn8, fold(jnp.where(adm, sc, POS), jnp.min))
    mx8 = jnp.maximum(mx8, fold(jnp.where(adm, sc, NEG), jnp.max))
    mn = jnp.minimum(mn0, jnp.min(mn8, axis=0, keepdims=True))
    mx = jnp.maximum(mx0, jnp.max(mx8, axis=0, keepdims=True))

    pack = 2 * SUBLANES

    def as16(v):
        return jnp.broadcast_to(v, (pack, qb)).astype(BF16)

    def tree(parts, op):
        while len(parts) > 1:
            parts = ([op(parts[i], parts[i + 1]) for i in range(0, len(parts) - 1, 2)]
                     + ([parts[-1]] if len(parts) % 2 else []))
        return parts[0]

    def groups(x):
        return [x[i * pack:(i + 1) * pack, :] for i in range(x.shape[0] // pack)]

    one16 = jnp.ones((pack, qb), BF16)
    zero16 = jnp.zeros((pack, qb), BF16)

    def count_ge(th):
        th16 = as16(th)

        def ones(x):
            return tree([jnp.where(g >= th16, one16, zero16) for g in groups(x)], jnp.add)

        def body(kc, cnt):
            return cnt + ones(s16_scr[rows(kc), :])
        cnt = lax.fori_loop(0, nkc, body, ones(s16_scr[meta_rows, :]))
        return jnp.sum(cnt.astype(F32), axis=0, keepdims=True)

    def band_extent(lo, hi):
        lo16, hi16 = as16(lo), as16(hi)
        pos16 = jnp.full((pack, qb), POS, BF16)
        neg16 = jnp.full((pack, qb), NEG, BF16)

        def ext(x):
            return (tree([jnp.where(g >= lo16, g, pos16) for g in groups(x)], jnp.minimum),
                    tree([jnp.where(g < hi16, g, neg16) for g in groups(x)], jnp.maximum))

        def body(kc, carry):
            bmin, bmax = ext(s16_scr[rows(kc), :])
            return jnp.minimum(carry[0], bmin), jnp.maximum(carry[1], bmax)
        bmin, bmax = lax.fori_loop(0, nkc, body, ext(s16_scr[meta_rows, :]))
        return (jnp.min(bmin.astype(F32), axis=0, keepdims=True),
                jnp.max(bmax.astype(F32), axis=0, keepdims=True))

    def any_lane(flags):
        return jnp.max(flags)

    def bisect_cond(carry):
        it, _, _, _, pending = carry
        return jnp.logical_and(it < MAX_BISECT, pending > 0)

    def halve(state):
        lo, hi, cnt_lo = state
        mid = (lo + (hi - lo) * 0.5).astype(BF16).astype(F32)
        cnt = count_ge(mid)
        ge = cnt >= k_top
        return jnp.where(ge, mid, lo), jnp.where(ge, hi, mid), jnp.where(ge, cnt, cnt_lo)

    def unresolved(lo, hi, cnt_lo):
        over = jnp.where(cnt_lo > k_top, 1, 0)

        def tied_check():
            bmin, bmax = band_extent(lo, hi)
            return any_lane(jnp.where(bmin < bmax, over, 0))

        return lax.cond(any_lane(over) > 0, tied_check, lambda: jnp.int32(0))

    def bisect_body(carry):
        it, lo, hi, cnt_lo, _ = carry
        state = (lo, hi, cnt_lo)
        for _ in range(BISECT_PER_CHECK):
            state = halve(state)
        return (it + BISECT_PER_CHECK,) + state + (unresolved(*state),)

    n_adm = (N_META + CHUNK * (qchunk + 1)).astype(F32)
    lo0 = mn.astype(BF16).astype(F32)
    hi0 = (mx + jnp.abs(mx) * BF16_CLEARANCE + TINY).astype(BF16).astype(F32)
    state = lax.fori_loop(0, BISECT_WARMUP, lambda _, s: halve(s), (lo0, hi0, n_adm))
    _, lo, hi, cnt_lo, _ = lax.while_loop(
        bisect_cond, bisect_body, (jnp.int32(BISECT_WARMUP),) + state + (unresolved(*state),))
    pending = any_lane(jnp.where(cnt_lo > k_top, 1, 0))

    @pl.when(pending == 0)
    def _():
        def body(kc, _):
            s_scr[rows(kc), :] = jnp.where(s16_scr[rows(kc), :].astype(F32) >= lo, 0.0, NEG)
            return 0
        lax.fori_loop(0, nkc, body, 0)
        s_scr[meta_rows, :] = jnp.where(s16_scr[meta_rows, :].astype(F32) >= lo, 0.0, NEG)

    @pl.when(pending > 0)
    def _():
        need = k_top - count_ge(hi)

        def pick(sv, tri, before):
            band = jnp.where(sv >= lo, jnp.where(sv < hi, 1.0, 0.0), 0.0)
            rank = _dot(tri, band.astype(BF16)) + before
            take = jnp.where(rank <= need, band, 0.0)
            bias = jnp.where(sv >= hi, 0.0, jnp.where(take > 0.5, 0.0, NEG))
            return bias, before + jnp.sum(fold(band, jnp.sum), axis=0, keepdims=True)

        def tri(n):
            return jnp.where(lax.broadcasted_iota(jnp.int32, (n, n), 0)
                             >= lax.broadcasted_iota(jnp.int32, (n, n), 1), 1.0, 0.0).astype(BF16)

        bias_m, before = pick(s16_scr[meta_rows, :].astype(F32), tri(N_META),
                              jnp.zeros((1, qb), F32))
        s_scr[meta_rows, :] = bias_m
        tri_kc = tri(KC)

        def body(kc, before):
            bias, before = pick(s16_scr[rows(kc), :].astype(F32), tri_kc, before)
            s_scr[rows(kc), :] = bias
            return before
        lax.fori_loop(0, nkc, body, before)

    def head(h):
        return slice(h * qb, (h + 1) * qb)

    for h in range(N_HEADS):
        qat_scr[:, head(h)] = (_dot(wuk_ref[h], qt_ref[h * HEAD_DIM:(h + 1) * HEAD_DIM, :])
                               * (ATTN_SCALE * LOG2E)).astype(BF16)
    m_scr[...] = jnp.full(m_scr.shape, 0.5 * NEG, F32)
    acc_scr[...] = jnp.zeros(acc_scr.shape, F32)

    def attend(c_rows, bias, values):
        r = c_rows.shape[0]
        lg_scr[0:r, :] = _dot(c_rows, qat_scr[...])
        for h in range(N_HEADS):
            lg = lg_scr[0:r, head(h)] + bias
            m_old = m_scr[:, head(h)]
            m_new = jnp.maximum(m_old, jnp.max(lg, axis=0, keepdims=True))
            p = jnp.exp2(lg - m_new).astype(BF16)
            m_scr[:, head(h)] = m_new
            pv = None
            for ct_cols, rs in values:
                t = _dot(ct_cols, p[rs, :])
                pv = t if pv is None else pv + t
            acc_scr[:, head(h)] = acc_scr[:, head(h)] * jnp.exp2(m_old - m_new) + pv

    def attend_chunk(kc):
        attend(c_ref[rows(kc), :], s_scr[rows(kc), :], [(ct_ref[kc], slice(0, KC))])

    def attend_pair(i, _):
        attend_chunk(2 * i)
        attend_chunk(2 * i + 1)
        return 0
    lax.fori_loop(0, j // 2, attend_pair, 0)

    @pl.when(j % 2 == 1)
    def _():
        attend_chunk(j - 1)

    attend(jnp.concatenate([c_ref[rows(j), :], cm_ref[...]], axis=0),
           jnp.concatenate([s_scr[rows(j), :], s_scr[meta_rows, :]], axis=0),
           [(ct_ref[j], slice(0, KC)), (cmt_ref[...], slice(KC, KC + N_META))])

    for h in range(N_HEADS):
        olat = acc_scr[0:KV_LATENT, head(h)] / acc_scr[KV_LATENT:KV_LATENT + 1, head(h)]
        ot_scr[h * HEAD_DIM:(h + 1) * HEAD_DIM, :] = _dot(wuvt_ref[h], olat.astype(BF16))
    o_ref[...] = ot_scr[...].T.astype(BF16)


def _mix_kernel(x_ref, g1_ref, wgate_ref, attn_ref, yp_ref, wba_ref, wbp_ref, wout_ref,
                g2_ref, wr_ref, wrhi_ref, br_ref, h1_ref, u2_ref, route_ref, count_ref,
                merged_scr):
    nsub = D_MODEL // LANES
    counts = None
    for r0 in range(0, x_ref.shape[0], MIX_ROWS):
        rows = pl.ds(r0, MIX_ROWS)
        c = _mix_group(x_ref.at[rows], g1_ref, wgate_ref, attn_ref.at[rows], yp_ref.at[rows],
                       wba_ref, wbp_ref, wout_ref, g2_ref, wr_ref, wrhi_ref, br_ref,
                       h1_ref.at[rows], u2_ref.at[pl.ds(r0 * nsub, MIX_ROWS * nsub)],
                       route_ref.at[:, rows], merged_scr.at[rows])
        counts = c if counts is None else counts + c
    count_ref[...] = jnp.broadcast_to(counts, count_ref.shape)


def _mix_group(x_ref, g1_ref, wgate_ref, attn_ref, yp_ref, wba_ref, wbp_ref, wout_ref,
               g2_ref, wr_ref, wrhi_ref, br_ref, h1_ref, u2_ref, route_ref, merged_scr):
    x = x_ref[...]
    u = _rms(x, g1_ref[...]).astype(BF16)
    attn = attn_ref[...]
    yp = yp_ref[...]
    for cb in range(D_MODEL // MIX_COLS):
        ca = slice(cb * MIX_COLS, (cb + 1) * MIX_COLS)
        cp = slice(D_MODEL + cb * MIX_COLS, D_MODEL + (cb + 1) * MIX_COLS)
        g_attn = 1.0 / (1.0 + jnp.exp(-_dot(u, wgate_ref[:, ca])))
        g_pool = 1.0 / (1.0 + jnp.exp(-_dot(u, wgate_ref[:, cp])))
        merged_scr[:, ca] = (g_attn * _dot(attn, wba_ref[:, ca])
                             + g_pool * _dot(yp, wbp_ref[:, ca])).astype(BF16)
    h1 = x + _dot(merged_scr[...], wout_ref[...])
    h1_ref[...] = h1
    u2 = _rms(h1, g2_ref[...])
    nsub = D_MODEL // LANES
    for s in range(nsub):
        u2_ref[pl.ds(s, x.shape[0], stride=nsub), :] = u2[:, s * LANES:(s + 1) * LANES]
    u2_hi = u2.astype(BF16)

    u2_lo = (u2 - u2_hi.astype(F32)).astype(BF16)
    hi_both = _dot(u2_hi, wr_ref[...])
    lg = hi_both[:, :LANES] + hi_both[:, LANES:] + _dot(u2_lo, wrhi_ref[...]) + br_ref[...]
    lane = lax.broadcasted_iota(jnp.int32, lg.shape, 1)
    is_g = lane < N_GROUPS
    gl = jnp.where(is_g, lg, NEG)
    gmax = jnp.max(gl, axis=1, keepdims=True)
    gidx = jnp.min(jnp.where(gl == gmax, lane, LANES), axis=1, keepdims=True)
    p_g = 1.0 / jnp.sum(jnp.where(is_g, jnp.exp(gl - gmax), 0.0), axis=1, keepdims=True)
    e_lane = lane - N_GROUPS
    lane_grp = jnp.where(e_lane >= 0,
                         jnp.where(e_lane < N_EXPERTS, e_lane // EXPERTS_PER_GROUP, -1), -1)
    in_grp = lane_grp == gidx
    el = jnp.where(in_grp, lg, NEG)
    t1 = jnp.max(el, axis=1, keepdims=True)
    i1 = jnp.min(jnp.where(el == t1, lane, LANES), axis=1, keepdims=True)
    el2 = jnp.where(lane == i1, NEG, el)
    t2 = jnp.max(el2, axis=1, keepdims=True)
    i2 = jnp.min(jnp.where(el2 == t2, lane, LANES), axis=1, keepdims=True)
    r = jnp.exp(t2 - t1)
    p1 = 1.0 / (1.0 + r)
    p2 = r * p1
    e1 = (i1 - N_GROUPS).astype(F32)
    e2 = (i2 - N_GROUPS).astype(F32)
    record = jnp.where(lane == 0, e1, jnp.where(lane == 1, e2, jnp.where(
        lane == 2, p1 * p_g, jnp.where(lane == 3, p2 * p_g, 0.0))))
    route_ref[...] = record.T[0:SUBLANES, :]
    chosen = jnp.where(lane == i1, 1.0, jnp.where(lane == i2, 1.0, 0.0))
    return jnp.sum(chosen, axis=0, keepdims=True)


def _moe_kernel(start_ref, cnt_ref, tok_ref, slot_ref, wt_ref, x2_ref, wg_hbm, wu_hbm, wd_hbm,
                y2_ref, r2_scr, ga_scr, gb_scr, rt_scr, wg_buf, wu_buf, wd_buf, w_sem):
    ts = x2_ref.shape[0] // SUBLANES
    tmx = ga_scr.shape[0] // SUBLANES
    st = pl.program_id(0)
    e = pl.program_id(1)
    n_e = pl.num_programs(1)
    nsub = D_MODEL // LANES
    step = st * n_e + e
    n_steps = pl.num_programs(0) * n_e

    def weight_copies(g):
        ex = g % n_e
        buf = g % MOE_WEIGHT_BUFFERS
        return [pltpu.make_async_copy(hbm.at[ex], vbuf.at[buf], w_sem.at[k, buf])
                for k, (hbm, vbuf) in enumerate(((wg_hbm, wg_buf), (wu_hbm, wu_buf),
                                                 (wd_hbm, wd_buf)))]

    @pl.when(step == 0)
    def _():
        for g in range(MOE_WEIGHT_BUFFERS - 1):
            for cp in weight_copies(g):
                cp.start()

    @pl.when(step + MOE_WEIGHT_BUFFERS - 1 < n_steps)
    def _():
        for cp in weight_copies(step + MOE_WEIGHT_BUFFERS - 1):
            cp.start()

    for cp in weight_copies(step):
        cp.wait()
    wbuf = step % MOE_WEIGHT_BUFFERS

    def vreg_rows(i):
        return pl.ds(pl.multiple_of(i * SUBLANES, SUBLANES), SUBLANES)

    def gather(g_scr, base):
        def gather_body(r8, _):
            for u in range(SUBLANES):
                r = r8 * SUBLANES + u
                g_scr[vreg_rows(r), :] = x2_ref[vreg_rows(tok_ref[base + r]), :]
            return 0
        lax.fori_loop(0, tmx // SUBLANES, gather_body, 0)

    def expert_ffn(g_scr, base):
        xg = jnp.concatenate([g_scr[pl.ds(s, tmx, stride=nsub), :] for s in range(nsub)],
                             axis=1).astype(BF16)
        a = _dot(xg, wg_buf[wbuf].astype(BF16))
        b = _dot(xg, wu_buf[wbuf].astype(BF16))
        hg = (a * (1.0 / (1.0 + jnp.exp(-a))) * b).astype(BF16)
        yr = _dot(hg, wd_buf[wbuf].astype(BF16))
        for s in range(nsub):
            rt_scr[pl.ds(s, tmx, stride=nsub), :] = yr[:, s * LANES:(s + 1) * LANES]
        r2_scr[pl.ds(pl.multiple_of(base * SUBLANES, SUBLANES), tmx * SUBLANES), :] = rt_scr[...]

    start = start_ref[step]
    cnt = cnt_ref[step]

    @pl.when(e == 0)
    def _():
        gather(ga_scr, start)

    def run(cur_scr, nxt_scr):
        expert_ffn(cur_scr, start)
        if nxt_scr is not None:
            nxt = start_ref[step + 1]
            for r in range(tmx):
                nxt_scr[r * SUBLANES:(r + 1) * SUBLANES, :] = (
                    x2_ref[vreg_rows(tok_ref[nxt + r]), :])

        def extra_tile(i, _):
            gather(cur_scr, start + i * tmx)
            expert_ffn(cur_scr, start + i * tmx)
            return 0
        lax.fori_loop(1, (cnt + tmx - 1) // tmx, extra_tile, 0)

    last = e == n_e - 1
    even = e % 2 == 0

    @pl.when(jnp.logical_and(even, jnp.logical_not(last)))
    def _():
        run(ga_scr, gb_scr)

    @pl.when(jnp.logical_and(jnp.logical_not(even), jnp.logical_not(last)))
    def _():
        run(gb_scr, ga_scr)

    @pl.when(last)
    def _():
        run(ga_scr if (N_EXPERTS - 1) % 2 == 0 else gb_scr, None)

        def combine_body(t8, _):
            for u in range(SUBLANES):
                t = t8 * SUBLANES + u
                y2_ref[vreg_rows(t), :] = (
                    wt_ref[t] * r2_scr[vreg_rows(slot_ref[t]), :]
                    + wt_ref[ts + t] * r2_scr[vreg_rows(slot_ref[ts + t]), :])
            return 0
        lax.fori_loop(0, ts // SUBLANES, combine_body, 0)


def _final_kernel(h1_ref, y2_ref, gf_ref, o_ref):
    nsub = D_MODEL // LANES
    y = jnp.concatenate([y2_ref[pl.ds(s, h1_ref.shape[0], stride=nsub), :] for s in range(nsub)],
                        axis=1)
    o_ref[...] = _rms(h1_ref[...] + y, gf_ref[...])


def _const_spec(shape):
    nd = len(shape)
    return pl.BlockSpec(shape, lambda *_: (0,) * nd, pipeline_mode=pl.Buffered(1))


def _params(n_axes):
    return pltpu.CompilerParams(dimension_semantics=("arbitrary",) * n_axes,
                                vmem_limit_bytes=VMEM_LIMIT)


def kernel(x, meta_tokens, norm1_g, w_in, kv_norm_g, w_uk, w_uv, w_pool, pool_scale,
           w_branch_attn, w_branch_pool, w_out, norm2_g, w_group_router, b_group_router,
           w_expert_router, b_expert_router, w_expert_gate, w_expert_up, w_expert_down,
           final_norm_g):
    B, S, D = x.shape
    assert D == D_MODEL and S % QB == 0 and S % TM_PROJ == 0 and w_in.shape[0] == 1
    assert QB == KC and TM_PROJ % KC == 0 and QB % CHUNK == 0
    assert (B * S) % TS_MOE == 0 and TS_MOE % TM_MIX == 0 and TM_MIX % MIX_ROWS == 0
    assert TMX_MOE <= LIST_PAD and N_EXPERTS % 2 == 0
    N = B * S
    k_top = min(TOPK_MAX, S // 4)
    xr = x.reshape(N, D)

    wi = w_in[0]
    w1 = jnp.concatenate(
        [wi[:, 0:640], wi[:, 640:896], wi[:, 936:1448], wi[:, 896:936],
         jnp.zeros((D, W1_WIDTH - 1448), F32)], axis=1).astype(BF16)
    wgate = wi[:, 1448:].astype(BF16)
    g1 = norm1_g[0].reshape(1, D)
    kvg = kv_norm_g[0].reshape(1, KV_LATENT)
    wpool = w_pool[0].astype(BF16)
    pscale = pool_scale[0].reshape(1, POOL_WIDTH)
    wuk = jnp.transpose(w_uk[0], (1, 0, 2)).astype(BF16)
    wuvt = jnp.transpose(w_uv[0], (1, 2, 0)).astype(BF16)
    wr = jnp.concatenate(
        [w_group_router[0], w_expert_router[0].reshape(D, N_EXPERTS),
         jnp.zeros((D, LANES - N_GROUPS - N_EXPERTS), F32)], axis=1)
    br = jnp.concatenate(
        [b_group_router[0], b_expert_router[0].reshape(N_EXPERTS),
         jnp.zeros((LANES - N_GROUPS - N_EXPERTS,), F32)]).reshape(1, LANES)

    c_m, tail_m, pv_m = pl.pallas_call(
        _meta_kernel,
        out_shape=(jax.ShapeDtypeStruct((N_META, KV_LATENT), F32),
                   jax.ShapeDtypeStruct((N_META, LANES), F32),
                   jax.ShapeDtypeStruct((N_META, POOL_WIDTH), F32)),
        name="meta",
    )(meta_tokens, g1, w1, kvg)
    cm = c_m.astype(BF16)
    cmt = jnp.concatenate([cm.T, jnp.ones((1, N_META), BF16),
                           jnp.zeros((CT_ROWS - KV_LATENT - 1, N_META), BF16)], axis=0)
    ikm = tail_m[:, :IDX_DIM].astype(BF16)

    tpb = S // TM_PROJ
    tok = lambda w: pl.BlockSpec((TM_PROJ, w), lambda b, i: (b * tpb + i, 0))
    tok_t = lambda w: pl.BlockSpec((w, TM_PROJ), lambda b, i: (0, b * tpb + i))
    qt, c, ct, iqt, ik, iwt, yp = pl.pallas_call(
        _proj_kernel,
        grid=(B, tpb),
        in_specs=[tok(D), _const_spec((1, D)), _const_spec((D, W1_WIDTH)),
                  _const_spec((1, KV_LATENT)), _const_spec((N_META, POOL_WIDTH)),
                  _const_spec((len(POOL_WINDOWS), POOL_GROUP, POOL_GROUP)),
                  _const_spec((1, POOL_WIDTH))],
        out_specs=[tok_t(ATTN_WIDTH), tok(KV_LATENT),
                   pl.BlockSpec((None, TM_PROJ // KC, CT_ROWS, KC), lambda b, i: (b, i, 0, 0)),
                   tok_t(IDX_HEADS * IDX_DIM), tok(IDX_DIM), tok_t(IDX_HEADS), tok(POOL_WIDTH)],
        out_shape=(jax.ShapeDtypeStruct((ATTN_WIDTH, N), BF16),
                   jax.ShapeDtypeStruct((N, KV_LATENT), BF16),
                   jax.ShapeDtypeStruct((B, S // KC, CT_ROWS, KC), BF16),
                   jax.ShapeDtypeStruct((IDX_HEADS * IDX_DIM, N), BF16),
                   jax.ShapeDtypeStruct((N, IDX_DIM), BF16),
                   jax.ShapeDtypeStruct((IDX_HEADS, N), F32),
                   jax.ShapeDtypeStruct((N, POOL_WIDTH), BF16)),
        scratch_shapes=[pltpu.VMEM((TM_PROJ + N_META, POOL_WIDTH), F32)],
        compiler_params=_params(2),
        name="proj",
    )(xr, g1, w1, kvg, pv_m, wpool, pscale)

    nqb = S // QB
    qcol = lambda w: pl.BlockSpec((w, QB), lambda b, j: (0, b * nqb + j))
    attn = pl.pallas_call(
        functools.partial(_attn_kernel, k_top=float(k_top)),
        grid=(B, nqb),
        in_specs=[pl.BlockSpec((S, IDX_DIM), lambda b, j: (b, 0)),
                  _const_spec((N_META, IDX_DIM)),
                  pl.BlockSpec((S, KV_LATENT), lambda b, j: (b, 0)),
                  _const_spec((N_META, KV_LATENT)),
                  pl.BlockSpec((None, S // KC, CT_ROWS, KC), lambda b, j: (b, 0, 0, 0)),
                  _const_spec((CT_ROWS, N_META)),
                  qcol(IDX_HEADS * IDX_DIM), qcol(IDX_HEADS), qcol(ATTN_WIDTH),
                  _const_spec((N_HEADS, KV_LATENT, HEAD_DIM)),
                  _const_spec((N_HEADS, HEAD_DIM, KV_LATENT))],
        out_specs=pl.BlockSpec((QB, ATTN_WIDTH), lambda b, j: (b * nqb + j, 0)),
        out_shape=jax.ShapeDtypeStruct((N, ATTN_WIDTH), BF16),
        scratch_shapes=[pltpu.VMEM((S + N_META, QB), F32), pltpu.VMEM((S + N_META, QB), BF16),
                        pltpu.VMEM((KV_LATENT, N_HEADS * QB), BF16),
                        pltpu.VMEM((KC + N_META, N_HEADS * QB), F32),
                        pltpu.VMEM((1, N_HEADS * QB), F32),
                        pltpu.VMEM((CT_ROWS, N_HEADS * QB), F32),
                        pltpu.VMEM((ATTN_WIDTH, QB), F32)],
        compiler_params=_params(2),
        name="attn",
    )(ik, ikm, c, cm, ct, cmt, iqt, iwt, qt, wuk, wuvt)

    wr_hi = wr.astype(BF16)
    wr_lo = (wr - wr_hi.astype(F32)).astype(BF16)
    tok1 = lambda w: pl.BlockSpec((TM_MIX, w), lambda i: (i, 0))
    h1, u2, route, tile_counts = pl.pallas_call(
        _mix_kernel,
        grid=(N // TM_MIX,),
        in_specs=[tok1(D), _const_spec((1, D)), _const_spec((D, 2 * D)), tok1(ATTN_WIDTH),
                  tok1(POOL_WIDTH), _const_spec((ATTN_WIDTH, D)), _const_spec((POOL_WIDTH, D)),
                  _const_spec((D, D)), _const_spec((1, D)), _const_spec((D, 2 * LANES)),
                  _const_spec((D, LANES)), _const_spec((1, LANES))],
        out_specs=[tok1(D), pl.BlockSpec((TM_MIX * SUBLANES, LANES), lambda i: (i, 0)),
                   pl.BlockSpec((SUBLANES, TM_MIX), lambda i: (0, i)),
                   pl.BlockSpec((SUBLANES, LANES), lambda i: (i, 0))],
        out_shape=(jax.ShapeDtypeStruct((N, D), F32),
                   jax.ShapeDtypeStruct((N * SUBLANES, LANES), F32),
                   jax.ShapeDtypeStruct((SUBLANES, N), F32),
                   jax.ShapeDtypeStruct((N // TM_MIX * SUBLANES, LANES), F32)),
        scratch_shapes=[pltpu.VMEM((TM_MIX, D), BF16)],
        compiler_params=_params(1),
        name="mix",
    )(xr, g1, wgate, attn, yp, w_branch_attn[0].astype(BF16), w_branch_pool[0].astype(BF16),
      w_out[0].astype(BF16), norm2_g[0].reshape(1, D),
      jnp.concatenate([wr_hi, wr_lo], axis=1), wr_hi, br)

    nst = N // TS_MOE
    n_asg = 2 * TS_MOE
    per_tile = lambda a: a.reshape(2, nst, TS_MOE).transpose(1, 0, 2).reshape(nst, n_asg)
    eid = per_tile(route[0:2].astype(jnp.int32))
    wts = per_tile(route[2:4]).reshape(nst * n_asg)
    order = jnp.argsort(eid, axis=1, stable=True).astype(jnp.int32)
    slot = jnp.argsort(order, axis=1).astype(jnp.int32).reshape(nst * n_asg)
    tok_sorted = jnp.pad(order % TS_MOE, ((0, 0), (0, LIST_PAD))).reshape(
        nst * (n_asg + LIST_PAD))
    counts = tile_counts.reshape(nst, TS_MOE // TM_MIX, SUBLANES, LANES)[
        :, :, 0, N_GROUPS:N_GROUPS + N_EXPERTS].sum(axis=1).astype(jnp.int32)
    starts = (jnp.cumsum(counts, axis=1) - counts).reshape(nst * N_EXPERTS)
    counts = counts.reshape(nst * N_EXPERTS)

    weg = w_expert_gate[0].reshape(N_EXPERTS, D, EXPERT_HIDDEN)
    weu = w_expert_up[0].reshape(N_EXPERTS, D, EXPERT_HIDDEN)
    wed = w_expert_down[0].reshape(N_EXPERTS, EXPERT_HIDDEN, D)
    smem = lambda n: pl.BlockSpec((n,), lambda s, e, *_: (s,), memory_space=pltpu.SMEM)
    y = pl.pallas_call(
        _moe_kernel,
        grid_spec=pltpu.PrefetchScalarGridSpec(
            num_scalar_prefetch=2,
            grid=(nst, N_EXPERTS),
            in_specs=[smem(n_asg + LIST_PAD), smem(n_asg), smem(n_asg),
                      pl.BlockSpec((TS_MOE * SUBLANES, LANES), lambda s, e, *_: (s, 0)),
                      pl.BlockSpec(memory_space=pl.ANY), pl.BlockSpec(memory_space=pl.ANY),
                      pl.BlockSpec(memory_space=pl.ANY)],
            out_specs=pl.BlockSpec((TS_MOE * SUBLANES, LANES), lambda s, e, *_: (s, 0),
                                   pipeline_mode=pl.Buffered(1)),
            scratch_shapes=[pltpu.VMEM(((n_asg + TMX_MOE) * SUBLANES, LANES), F32),
                            pltpu.VMEM((TMX_MOE * SUBLANES, LANES), F32),
                            pltpu.VMEM((TMX_MOE * SUBLANES, LANES), F32),
                            pltpu.VMEM((TMX_MOE * SUBLANES, LANES), F32),
                            pltpu.VMEM((MOE_WEIGHT_BUFFERS, D, EXPERT_HIDDEN), F32),
                            pltpu.VMEM((MOE_WEIGHT_BUFFERS, D, EXPERT_HIDDEN), F32),
                            pltpu.VMEM((MOE_WEIGHT_BUFFERS, EXPERT_HIDDEN, D), F32),
                            pltpu.SemaphoreType.DMA((3, MOE_WEIGHT_BUFFERS))]),
        out_shape=jax.ShapeDtypeStruct((N * SUBLANES, LANES), F32),
        compiler_params=_params(2),
        name="moe",
    )(starts, counts, tok_sorted, slot, wts, u2, weg, weu, wed)

    ftok = pl.BlockSpec((TM_FINAL, D), lambda i: (i, 0))
    out = pl.pallas_call(
        _final_kernel,
        grid=(N // TM_FINAL,),
        in_specs=[ftok, pl.BlockSpec((TM_FINAL * SUBLANES, LANES), lambda i: (i, 0)),
                  _const_spec((1, D))],
        out_specs=ftok,
        out_shape=jax.ShapeDtypeStruct((N, D), F32),
        compiler_params=_params(1),
        name="final",
    )(h1, y, final_norm_g.reshape(1, D))
    return out.reshape(B, S, D)
```
